```python
import math
import jax
import jax.numpy as jnp
from jax import lax
import numpy as np

D_MODEL = 1024
BATCH = 2
SEQ = 16384
DEPTH = 4

F32 = jnp.float32
GRID_W = 64
CTX_LEN = 256
N_MIXERS = 2
N_MLSTM_LAYERS = (DEPTH + 1) // 2
N_DIFF_LAYERS = DEPTH // 2
EPS = 1e-6

M_HEADS = 4
M_HEAD_DIM = D_MODEL // M_HEADS
M_CONV_W = 5
M_CHUNK = 64
M_GATES = 4 * M_HEADS
M_IN = 4 * D_MODEL + M_GATES

D_HEADS = 8
D_HEAD_DIM = D_MODEL // (2 * D_HEADS)
D_V_DIM = 2 * D_HEAD_DIM
D_IN = 3 * D_MODEL
Q_BLOCK = 128
ROPE_BASE = 10000.0
ROPE_FREQS = D_HEAD_DIM // 4

N_GROUPS = 4
EXPERTS_PER_GROUP = 8
N_EXPERTS = N_GROUPS * EXPERTS_PER_GROUP
TOP_K = 2
D_EXPERT = 512
MOE_BLOCK = 128

kernel_name = 'hybrid_mlstm_diffattn_hmoe'


def rmsnorm(x, g):
    xf = x.astype(F32)
    y = xf * lax.rsqrt(jnp.mean(xf * xf, axis=-1, keepdims=True) + EPS)
    return (y * g.astype(F32)).astype(x.dtype)


def modulate(h, shift, scale):
    return h * (1 + scale) + shift


def dwconv_centered(x, w, b):
    pad = M_CONV_W // 2
    y = lax.conv_general_dilated(x, w[:, None, :].astype(x.dtype), window_strides=(1,),
                                 padding=[(pad, pad)], dimension_numbers=('NWC', 'WIO', 'NWC'),
                                 feature_group_count=x.shape[-1])
    return y + b.astype(x.dtype)


def mlstm_scan(q, k, v, i_log, f_log, state):
    b_, h_, t_, dh = q.shape
    nc = t_ // M_CHUNK

    def chunks(a):
        return jnp.moveaxis(a.reshape(a.shape[:2] + (nc, M_CHUNK) + a.shape[3:]), 2, 0)

    tri = jnp.tril(jnp.ones((M_CHUNK, M_CHUNK), bool))

    def step(carry, xs):
        C, n, m = carry
        qc, kc, vc, ic, fc = xs
        bcum = jnp.cumsum(fc, axis=-1)
        dmat = jnp.where(tri, bcum[..., :, None] - bcum[..., None, :] + ic[..., None, :], -jnp.inf)
        m_inter = bcum + m[..., None]
        m_out = jnp.maximum(m_inter, dmat.max(-1))
        w_intra = jnp.exp(dmat - m_out[..., None])
        w_inter = jnp.exp(m_inter - m_out)
        s = jnp.einsum('bhld,bhsd->bhls', qc, kc) * w_intra
        num = w_inter[..., None] * jnp.einsum('bhvd,bhld->bhlv', C, qc) + jnp.einsum('bhls,bhsv->bhlv', s, vc)
        den = w_inter * jnp.einsum('bhd,bhld->bhl', n, qc) + s.sum(-1)
        hout = num / jnp.maximum(jnp.abs(den), jnp.exp(-m_out))[..., None]
        b_last = bcum[..., -1]
        g = b_last[..., None] - bcum + ic
        m_next = jnp.maximum(b_last + m, g.max(-1))
        decay = jnp.exp(b_last + m - m_next)
        wg = jnp.exp(g - m_next[..., None])
        C = decay[..., None, None] * C + jnp.einsum('bhlv,bhld->bhvd', wg[..., None] * vc, kc)
        n = decay[..., None] * n + jnp.einsum('bhl,bhld->bhd', wg, kc)
        return (C, n, m_next), hout

    state, hs = lax.scan(step, state, tuple(chunks(a) for a in (q, k, v, i_log, f_log)))
    return jnp.moveaxis(hs, 0, 2).reshape(b_, h_, t_, dh), state


def mlstm_project(h, w_in, conv_w, conv_b, gate_b):
    b_, t_, _ = h.shape
    p = h @ w_in
    qk_pre, v, o_pre, gates = jnp.split(p, [2 * D_MODEL, 3 * D_MODEL, 4 * D_MODEL], axis=-1)
    qk = jax.nn.silu(dwconv_centered(qk_pre, conv_w, conv_b))
    q, k = jnp.split(qk, 2, axis=-1)

    def to_heads(a):
        return a.reshape(b_, t_, M_HEADS, M_HEAD_DIM).transpose(0, 2, 1, 3).astype(F32)

    q, k, v = to_heads(q), to_heads(k) * (M_HEAD_DIM ** -0.5), to_heads(v)
    gates = (gates.astype(F32) + gate_b.astype(F32)).reshape(b_, t_, 4, M_HEADS).transpose(2, 0, 3, 1)
    fwd = (gates[0], jax.nn.log_sigmoid(gates[2]))
    bwd = (gates[1], jax.nn.log_sigmoid(gates[3]))
    return q, k, v, jax.nn.sigmoid(o_pre), fwd, bwd


def mlstm_mixer(h_ctx, h_lat, w_in, conv_w, conv_b, gate_b, out_g, w_out):
    qc, kc, vc, oc, gfc, gbc = mlstm_project(h_ctx, w_in, conv_w, conv_b, gate_b)
    ql, kl, vl, ol, gfl, gbl = mlstm_project(h_lat, w_in, conv_w, conv_b, gate_b)
    b_ = h_lat.shape[0]
    zero = (jnp.zeros((b_, M_HEADS, M_HEAD_DIM, M_HEAD_DIM), F32),
            jnp.zeros((b_, M_HEADS, M_HEAD_DIM), F32), jnp.zeros((b_, M_HEADS), F32))
    rc_f, st_f = mlstm_scan(qc, kc, vc, gfc[0], gfc[1], zero)
    rl_f, _ = mlstm_scan(ql, kl, vl, gfl[0], gfl[1], st_f)
    fl = lambda a: jnp.flip(a, axis=2)
    rc_b, st_b = mlstm_scan(fl(qc), fl(kc), fl(vc), fl(gbc[0]), fl(gbc[1]), zero)
    rl_b, _ = mlstm_scan(fl(ql), fl(kl), fl(vl), fl(gbl[0]), fl(gbl[1]), st_b)

    def finish(r, o):
        r = r * lax.rsqrt(jnp.mean(r * r, axis=-1, keepdims=True) + EPS)
        r = r.transpose(0, 2, 1, 3).reshape(o.shape) * out_g.astype(F32)
        return (o * r.astype(o.dtype)) @ w_out

    return finish(rc_f + fl(rc_b), oc), finish(rl_f + fl(rl_b), ol)


def axial_rope_tables(rows):
    row = jnp.repeat(jnp.arange(rows), GRID_W)
    col = jnp.tile(jnp.arange(GRID_W), rows)
    inv = ROPE_BASE ** (-jnp.arange(ROPE_FREQS, dtype=F32) / ROPE_FREQS)
    ang = jnp.stack([row, col], axis=-1).astype(F32)[..., None] * inv
    return jnp.cos(ang), jnp.sin(ang)


def apply_rope(x, cos, sin):
    xs = x.reshape(x.shape[:-1] + (2, 2, ROPE_FREQS)).astype(F32)
    x1, x2 = xs[..., 0, :], xs[..., 1, :]
    cb, sb = cos[:, None], sin[:, None]
    out = jnp.stack([x1 * cb - x2 * sb, x1 * sb + x2 * cb], axis=-2)
    return out.reshape(x.shape).astype(x.dtype)


def diff_project(h, w_in, q_g, k_g):
    b_, t_, _ = h.shape
    q, k, v = jnp.split(h @ w_in, 3, axis=-1)

    def qk_heads(a):
        return a.reshape(b_, t_, D_HEADS, 2, D_HEAD_DIM).transpose(0, 2, 1, 3, 4)

    q = rmsnorm(qk_heads(q), q_g)
    k = rmsnorm(qk_heads(k), k_g)
    v = v.reshape(b_, t_, D_HEADS, D_V_DIM).transpose(0, 2, 1, 3)
    return q, k, v


def diff_attend(q, k, v, lam):
    s = jnp.einsum('bhqmd,bhkmd->mbhqk', q, k).astype(F32)
    p = jax.nn.softmax(s, axis=-1)
    a = p[0] - lam * p[1]
    return jnp.einsum('bhqk,bhkv->bhqv', a.astype(v.dtype), v)


def diff_mixer(h_ctx, h_lat, rope_cs, w_in, q_g, k_g, lq1, lk1, lq2, lk2, sub_g, w_out, lambda_init, need_ctx):
    scale = D_HEAD_DIM ** -0.5
    qc, kc, vc = diff_project(h_ctx, w_in, q_g, k_g)
    ql, kl, vl = diff_project(h_lat, w_in, q_g, k_g)
    cos, sin = rope_cs
    ql = apply_rope(ql, cos, sin) * scale
    kl = apply_rope(kl, cos, sin)
    lam = (jnp.exp(jnp.sum(lq1.astype(F32) * lk1.astype(F32)))
           - jnp.exp(jnp.sum(lq2.astype(F32) * lk2.astype(F32))) + lambda_init)
    k_all = jnp.concatenate([kc, kl], axis=2)
    v_all = jnp.concatenate([vc, vl], axis=2)
    b_, h_, t_ = ql.shape[:3]
    nb = t_ // Q_BLOCK
    qb = jnp.moveaxis(ql.reshape(b_, h_, nb, Q_BLOCK, 2, D_HEAD_DIM), 2, 0)
    ol = lax.map(lambda qq: diff_attend(qq, k_all, v_all, lam), qb)
    ol = jnp.moveaxis(ol, 0, 2).reshape(b_, h_, t_, D_V_DIM)

    def finish(o):
        o = rmsnorm(o, sub_g) * (1.0 - lambda_init)
        return o.transpose(0, 2, 1, 3).reshape(o.shape[0], o.shape[2], D_MODEL) @ w_out

    yl = finish(ol)
    yc = finish(diff_attend(qc * scale, kc, vc, lam)) if need_ctx else None
    return yc, yl


def hier_moe(xt, grp_w, grp_b, exp_w, exp_b, w_gu, w_dn):
    n_tok = xt.shape[0]
    gprob = jax.nn.softmax((xt @ grp_w).astype(F32) + grp_b.astype(F32), axis=-1)
    gp_top, g_idx = lax.top_k(gprob, 1)
    elog = ((xt @ exp_w).astype(F32) + exp_b.astype(F32)).reshape(n_tok, N_GROUPS, EXPERTS_PER_GROUP)
    elog_g = elog[jnp.arange(n_tok), g_idx[:, 0]]
    e_val, e_idx = lax.top_k(elog_g, TOP_K)
    gate = (gp_top * jax.nn.softmax(e_val, axis=-1)).reshape(-1)
    eid = (g_idx * EXPERTS_PER_GROUP + e_idx).reshape(-1)
    tok = jnp.repeat(jnp.arange(n_tok, dtype=jnp.int32), TOP_K)
    n_assign = n_tok * TOP_K
    order = jnp.argsort(eid)
    e_s, t_s, g_s = eid[order], tok[order], gate[order]
    counts = jnp.bincount(eid, length=N_EXPERTS)
    padded = (counts + MOE_BLOCK - 1) // MOE_BLOCK * MOE_BLOCK
    pad_end = jnp.cumsum(padded)
    pad_start = pad_end - padded
    raw_start = jnp.cumsum(counts) - counts
    dest = pad_start[e_s] + jnp.arange(n_assign) - raw_start[e_s]
    cap = -(-n_assign // MOE_BLOCK) * MOE_BLOCK + N_EXPERTS * MOE_BLOCK
    nblk = cap // MOE_BLOCK
    tok_buf = jnp.zeros((cap,), jnp.int32).at[dest].set(t_s)
    gate_buf = jnp.zeros((cap,), F32).at[dest].set(g_s)
    blk_e = jnp.minimum(jnp.searchsorted(pad_end, jnp.arange(nblk) * MOE_BLOCK, side='right'), N_EXPERTS - 1)

    def expert_block(args):
        t_blk, e = args
        xb = xt[t_blk]
        g_, u_ = jnp.split(xb @ w_gu[e], 2, axis=-1)
        return (jax.nn.silu(g_) * u_) @ w_dn[e]

    y = lax.map(expert_block, (tok_buf.reshape(nblk, MOE_BLOCK), blk_e))
    y = y.reshape(cap, -1) * gate_buf[:, None].astype(y.dtype)
    return jax.ops.segment_sum(y, tok_buf, num_segments=n_tok)


def setup_inputs(seed: int = 0) -> dict:
    key = jax.random.key(seed)
    ks = jax.random.split(key, 32)
    nrm = lambda k, shape, s: s * jax.random.normal(k, shape, F32)
    d = D_MODEL
    f_bias = jnp.tile(jnp.linspace(3.0, 6.0, M_HEADS), 2)[None]
    return {
        'x': nrm(ks[0], (BATCH, SEQ, d), 1.0),
        'c': nrm(ks[1], (BATCH, d), 1.0),
        'ctx': nrm(ks[2], (BATCH, CTX_LEN, d), 1.0),
        'c_ctx': nrm(ks[3], (d,), 1.0),
        'ada_w': nrm(ks[4], (DEPTH, d, 6 * d), 0.5 * d ** -0.5),
        'ada_b': nrm(ks[5], (DEPTH, 6 * d), 0.02),
        'norm1_g': 1.0 + nrm(ks[6], (DEPTH, d), 0.1),
        'norm2_g': 1.0 + nrm(ks[7], (DEPTH, d), 0.1),
        'm_w_in': nrm(ks[8], (N_MLSTM_LAYERS, d, M_IN), d ** -0.5),
        'm_conv_w': nrm(ks[9], (N_MLSTM_LAYERS, M_CONV_W, 2 * d), M_CONV_W ** -0.5),
        'm_conv_b': nrm(ks[10], (N_MLSTM_LAYERS, 2 * d), 0.02),
        'm_gate_b': jnp.concatenate([nrm(ks[11], (N_MLSTM_LAYERS, 2 * M_HEADS), 0.1),
                                     f_bias + nrm(ks[12], (N_MLSTM_LAYERS, 2 * M_HEADS), 0.01)], axis=-1),
        'm_out_g': 1.0 + nrm(ks[13], (N_MLSTM_LAYERS, d), 0.1),
        'm_w_out': nrm(ks[14], (N_MLSTM_LAYERS, d, d), d ** -0.5),
        'd_w_in': nrm(ks[15], (N_DIFF_LAYERS, d, D_IN), d ** -0.5),
        'd_q_g': 1.0 + nrm(ks[16], (N_DIFF_LAYERS, D_HEAD_DIM), 0.1),
        'd_k_g': 1.0 + nrm(ks[17], (N_DIFF_LAYERS, D_HEAD_DIM), 0.1),
        'd_lq1': nrm(ks[18], (N_DIFF_LAYERS, D_HEAD_DIM), 0.1),
        'd_lk1': nrm(ks[19], (N_DIFF_LAYERS, D_HEAD_DIM), 0.1),
        'd_lq2': nrm(ks[20], (N_DIFF_LAYERS, D_HEAD_DIM), 0.1),
        'd_lk2': nrm(ks[21], (N_DIFF_LAYERS, D_HEAD_DIM), 0.1),
        'd_sub_g': 1.0 + nrm(ks[22], (N_DIFF_LAYERS, D_V_DIM), 0.1),
        'd_w_out': nrm(ks[23], (N_DIFF_LAYERS, d, d), d ** -0.5),
        'r_grp_w': nrm(ks[24], (DEPTH, d, N_GROUPS), d ** -0.5),
        'r_grp_b': nrm(ks[25], (DEPTH, N_GROUPS), 0.01),
        'r_exp_w': nrm(ks[26], (DEPTH, d, N_EXPERTS), d ** -0.5),
        'r_exp_b': nrm(ks[27], (DEPTH, N_EXPERTS), 0.01),
        'e_w_gu': nrm(ks[28], (DEPTH, N_EXPERTS, d, 2 * D_EXPERT), d ** -0.5),
        'e_w_dn': nrm(ks[29], (DEPTH, N_EXPERTS, D_EXPERT, d), D_EXPERT ** -0.5),
    }


def reference(x, c, ctx, c_ctx, ada_w, ada_b, norm1_g, norm2_g, m_w_in, m_conv_w, m_conv_b, m_gate_b,
              m_out_g, m_w_out, d_w_in, d_q_g, d_k_g, d_lq1, d_lk1, d_lq2, d_lk2, d_sub_g, d_w_out,
              r_grp_w, r_grp_b, r_exp_w, r_exp_b, e_w_gu, e_w_dn):
    b_, t_, d_ = x.shape
    rows = t_ // GRID_W
    rope_cs = axial_rope_tables(rows)
    sc = jax.nn.silu(c)
    scc = jax.nn.silu(c_ctx)
    xl, xc = x, ctx
    n_ctx = b_ * ctx.shape[1]
    for i in range(DEPTH):
        last = i == DEPTH - 1
        j = i // N_MIXERS
        mod = (sc @ ada_w[i] + ada_b[i]).reshape(b_, 6, 1, d_).transpose(1, 0, 2, 3)
        mod_c = (scc @ ada_w[i] + ada_b[i]).reshape(6, 1, 1, d_)
        hl = modulate(rmsnorm(xl, norm1_g[i]), mod[0], mod[1])
        hc = modulate(rmsnorm(xc, norm1_g[i]), mod_c[0], mod_c[1])
        if i % N_MIXERS == 0:
            yc, yl = mlstm_mixer(hc, hl, m_w_in[j], m_conv_w[j], m_conv_b[j], m_gate_b[j], m_out_g[j], m_w_out[j])
        else:
            yc, yl = diff_mixer(hc, hl, rope_cs, d_w_in[j], d_q_g[j], d_k_g[j], d_lq1[j], d_lk1[j], d_lq2[j],
                                d_lk2[j], d_sub_g[j], d_w_out[j], 0.8 - 0.6 * math.exp(-0.3 * i), not last)
        xl = xl + mod[2] * yl
        hl = modulate(rmsnorm(xl, norm2_g[i]), mod[3], mod[4]).reshape(-1, d_)
        if last:
            y = hier_moe(hl, r_grp_w[i], r_grp_b[i], r_exp_w[i], r_exp_b[i], e_w_gu[i], e_w_dn[i])
            xl = xl + mod[5] * y.reshape(b_, t_, d_)
        else:
            xc = xc + mod_c[2] * yc
            hc = modulate(rmsnorm(xc, norm2_g[i]), mod_c[3], mod_c[4]).reshape(-1, d_)
            y = hier_moe(jnp.concatenate([hc, hl], axis=0), r_grp_w[i], r_grp_b[i], r_exp_w[i], r_exp_b[i],
                         e_w_gu[i], e_w_dn[i])
            xc = xc + mod_c[5] * y[:n_ctx].reshape(xc.shape)
            xl = xl + mod[5] * y[n_ctx:].reshape(b_, t_, d_)
    return xl
```

```python
import functools
import math

import jax
import jax.numpy as jnp
from jax import lax
from jax.experimental import pallas as pl
from jax.experimental.pallas import tpu as pltpu

F32 = jnp.float32
BF16 = jnp.bfloat16
I32 = jnp.int32
HIGHEST = lax.Precision.HIGHEST

EPS = 1e-6
GRID_W = 64
ROPE_BASE = 10000.0
M_HEADS = 4
M_CONV_W = 5
D_HEADS = 8
D_HEAD_DIM = 64
N_GROUPS = 4
EXPERTS_PER_GROUP = 8
N_EXPERTS = 32
D_EXPERT = 512

LANES = 128
CHUNK = 256
KV_CHUNK = 1024
MOE_BLOCK = 256
NEG = -1e30
VMEM_LIMIT = 56 * 1024 * 1024


def _cparams(sem):
    return pltpu.CompilerParams(dimension_semantics=sem, vmem_limit_bytes=VMEM_LIMIT)


def _nt_dot(a, b):
    return lax.dot_general(a, b, (((1,), (1,)), ((), ())), preferred_element_type=F32)


def _tn_dot(a, b):
    return lax.dot_general(a, b, (((0,), (0,)), ((), ())), preferred_element_type=F32)


def _mod_kernel(s_ref, w_ref, b_ref, o_ref):
    s = s_ref[...]
    s = s * jax.nn.sigmoid(s)
    o_ref[...] = jnp.dot(s, w_ref[...], precision=HIGHEST, preferred_element_type=F32) + b_ref[...]


def _modulation(cond, ada_w, ada_b):
    depth, d, n = ada_w.shape
    tn = n // 4
    out = pl.pallas_call(
        _mod_kernel,
        grid=(depth, n // tn),
        in_specs=[pl.BlockSpec((8, d), lambda l, j: (0, 0)),
                  pl.BlockSpec((None, d, tn), lambda l, j: (l, 0, j)),
                  pl.BlockSpec((None, 1, tn), lambda l, j: (l, 0, j))],
        out_specs=pl.BlockSpec((None, 8, tn), lambda l, j: (l, 0, j)),
        out_shape=jax.ShapeDtypeStruct((depth, 8, n), F32),
        compiler_params=_cparams(("arbitrary", "arbitrary")),
    )(cond, ada_w, ada_b.reshape(depth, 1, n))
    return out.reshape(depth, 8, 6, d)


def _row_mod(mb_ref, mc_ref, is_ctx, r):
    return jnp.where(is_ctx, mc_ref[r:r + 1, :], mb_ref[r:r + 1, :])


def _ctx_rows(tile_in_batch, tm, cl):
    row = tile_in_batch * tm + lax.broadcasted_iota(I32, (tm, 1), 0)
    return row < cl


def _in_kernel(*refs, tm, nt, cl, has_gates):
    if has_gates:
        x_ref, g_ref, mb_ref, mc_ref, w_ref, wg_ref, gb_ref, o_ref, go_ref, h_ref = refs
    else:
        x_ref, g_ref, mb_ref, mc_ref, w_ref, o_ref, h_ref = refs
    i = pl.program_id(0)
    j = pl.program_id(1)

    @pl.when(j == 0)
    def _():
        x = x_ref[...]
        y = x * lax.rsqrt(jnp.mean(x * x, axis=-1, keepdims=True) + EPS) * g_ref[...]
        is_ctx = _ctx_rows(i % nt, tm, cl)
        h = y * (1.0 + _row_mod(mb_ref, mc_ref, is_ctx, 1)) + _row_mod(mb_ref, mc_ref, is_ctx, 0)
        hb = h.astype(BF16)
        h_ref[...] = hb
        if has_gates:
            go_ref[...] = jnp.dot(hb, wg_ref[...], preferred_element_type=F32) + gb_ref[...]

    o_ref[...] = jnp.dot(h_ref[...], w_ref[...], preferred_element_type=F32).astype(o_ref.dtype)


def _in_proj(x, g, mods, w, *, b, s, cl, tm, tn, gates=None):
    ntot, d = x.shape
    n = w.shape[1]
    nt = s // tm
    has_gates = gates is not None
    in_specs = [pl.BlockSpec((tm, d), lambda i, j: (i, 0)),
                pl.BlockSpec((1, d), lambda i, j: (0, 0)),
                pl.BlockSpec((None, 6, d), lambda i, j: (i // nt, 0, 0)),
                pl.BlockSpec((None, 6, d), lambda i, j: (b, 0, 0)),
                pl.BlockSpec((d, tn), lambda i, j: (0, j))]
    args = [x, g.reshape(1, d), mods, mods, w]
    out_specs = pl.BlockSpec((tm, tn), lambda i, j: (i, j))
    out_shape = jax.ShapeDtypeStruct((ntot, n), BF16)
    if has_gates:
        wg, gb = gates
        in_specs += [pl.BlockSpec((d, LANES), lambda i, j: (0, 0)),
                     pl.BlockSpec((1, LANES), lambda i, j: (0, 0))]
        args += [wg, gb]
        out_specs = [out_specs, pl.BlockSpec((tm, LANES), lambda i, j: (i, 0))]
        out_shape = [out_shape, jax.ShapeDtypeStruct((ntot, LANES), F32)]
    return pl.pallas_call(
        functools.partial(_in_kernel, tm=tm, nt=nt, cl=cl, has_gates=has_gates),
        grid=(ntot // tm, n // tn),
        in_specs=in_specs,
        out_specs=out_specs,
        out_shape=out_shape,
        scratch_shapes=[pltpu.VMEM((tm, d), BF16)],
        compiler_params=_cparams(("arbitrary", "arbitrary")),
    )(*args)


def _conv_kernel(xm_ref, xp_ref, xn_ref, w_ref, b_ref, o_ref, *, nc, ncol_q, k_scale):
    i = pl.program_id(0)
    j = pl.program_id(1)
    c = i % nc
    left_ok = c >= 2
    right_ok = jnp.logical_and(c >= 1, c < nc - 1)
    xm = xm_ref[...].astype(F32)
    xp = jnp.where(left_ok, xp_ref[...].astype(F32), 0.0)[14:16, :]
    xn = jnp.where(right_ok, xn_ref[...].astype(F32), 0.0)[0:2, :]
    ext = jnp.concatenate([xp, xm, xn], axis=0)
    rows = xm.shape[0]
    w = w_ref[...]
    y = b_ref[...] + w[0:1, :] * ext[0:rows, :]
    for k in range(1, M_CONV_W):
        y = y + w[k:k + 1, :] * ext[k:k + rows, :]
    y = y * jax.nn.sigmoid(y)
    y = y * jnp.where(j >= ncol_q, k_scale, 1.0)
    o_ref[...] = y.astype(o_ref.dtype)


def _mlstm_conv(p, conv_w, conv_b, *, nc, d):
    ntot = p.shape[0]
    tn = 512
    nrb = ntot // 16
    return pl.pallas_call(
        functools.partial(_conv_kernel, nc=nc, ncol_q=d // tn, k_scale=(d // M_HEADS) ** -0.5),
        grid=(ntot // CHUNK, 2 * d // tn),
        in_specs=[pl.BlockSpec((CHUNK, tn), lambda i, j: (i, j)),
                  pl.BlockSpec((16, tn), lambda i, j: (jnp.maximum(i * (CHUNK // 16) - 1, 0), j)),
                  pl.BlockSpec((16, tn), lambda i, j: (jnp.minimum((i + 1) * (CHUNK // 16), nrb - 1), j)),
                  pl.BlockSpec((M_CONV_W, tn), lambda i, j: (0, j)),
                  pl.BlockSpec((1, tn), lambda i, j: (0, j))],
        out_specs=pl.BlockSpec((CHUNK, tn), lambda i, j: (i, j)),
        out_shape=jax.ShapeDtypeStruct((ntot, 2 * d), BF16),
        compiler_params=_cparams(("arbitrary", "arbitrary")),
    )(p, p, p, conv_w, conv_b.reshape(1, 2 * d))


def _log_sigmoid(x):
    return jnp.minimum(x, 0.0) - jnp.log(1.0 + jnp.exp(-jnp.abs(x)))


def _mlstm_chunk(q, k, v, gcol, grow, igate, fgate, mask, mask_t, ct_ref, n_ref, m_ref, o_ref):
    i_row = grow[igate:igate + 1, :]
    i_col = gcol[:, igate:igate + 1]
    f_row = _log_sigmoid(grow[fgate:fgate + 1, :])
    f_col = _log_sigmoid(gcol[:, fgate:fgate + 1])
    cum_col = jnp.sum(jnp.where(mask, f_row, 0.0), axis=1, keepdims=True)
    cum_row = jnp.sum(jnp.where(mask_t, f_col, 0.0), axis=0, keepdims=True)
    tot = jnp.sum(f_col, axis=0, keepdims=True)
    m_prev = m_ref[...]
    dmat = jnp.where(mask, cum_col - cum_row + i_row, NEG)
    m_inter = cum_col + m_prev
    m_out = jnp.maximum(m_inter, jnp.max(dmat, axis=1, keepdims=True))
    w_intra = jnp.exp(dmat - m_out)
    w_inter = jnp.exp(m_inter - m_out)
    sc = _nt_dot(q, k) * w_intra
    ct = ct_ref[...]
    n_row = n_ref[...]
    qf = q.astype(F32)
    kf = k.astype(F32)
    vf = v.astype(F32)
    num = (w_inter * jnp.dot(q, ct.astype(BF16), preferred_element_type=F32)
           + jnp.dot(sc.astype(BF16), v, preferred_element_type=F32))
    den = w_inter * jnp.sum(qf * n_row, axis=1, keepdims=True) + jnp.sum(sc, axis=1, keepdims=True)
    o_ref[...] = (num / jnp.maximum(jnp.abs(den), jnp.exp(-m_out))).astype(o_ref.dtype)
    g_col = tot - cum_col + i_col
    m_next = jnp.maximum(tot + m_prev, jnp.max(g_col, axis=0, keepdims=True))
    decay = jnp.exp(tot + m_prev - m_next)
    wg = jnp.exp(g_col - m_next)
    ct_ref[...] = decay * ct + _tn_dot(k, (wg * vf).astype(BF16))
    n_ref[...] = decay * n_row + jnp.sum(wg * kf, axis=0, keepdims=True)
    m_ref[...] = m_next


def _scan_kernel(qf_ref, kf_ref, vf_ref, gcf_ref, grf_ref, qb_ref, kb_ref, vb_ref, gcb_ref, grb_ref,
                 of_ref, ob_ref, ctf_ref, nf_ref, mf_ref, ctb_ref, nb_ref, mb_ref):
    @pl.when(pl.program_id(2) == 0)
    def _():
        for r in (ctf_ref, nf_ref, mf_ref, ctb_ref, nb_ref, mb_ref):
            r[...] = jnp.zeros(r.shape, r.dtype)

    t_i = lax.broadcasted_iota(I32, (CHUNK, CHUNK), 0)
    s_i = lax.broadcasted_iota(I32, (CHUNK, CHUNK), 1)
    lower = s_i <= t_i
    upper = s_i >= t_i
    _mlstm_chunk(qf_ref[...], kf_ref[...], vf_ref[...], gcf_ref[...], grf_ref[...], 0, 2,
                 lower, upper, ctf_ref, nf_ref, mf_ref, of_ref)
    _mlstm_chunk(qb_ref[...], kb_ref[...], vb_ref[...], gcb_ref[...], grb_ref[...], 1, 3,
                 upper, lower, ctb_ref, nb_ref, mb_ref, ob_ref)


def _mlstm_scan(qk, p, gcol, grow, *, b, nc, d):
    ntot = qk.shape[0]
    dh = d // M_HEADS

    def fwd(s):
        return s

    def bwd(s):
        return jnp.where(s == 0, 0, nc - s)

    def specs(order):
        return [pl.BlockSpec((CHUNK, dh), lambda bi, h, s: (bi * nc + order(s), h)),
                pl.BlockSpec((CHUNK, dh), lambda bi, h, s: (bi * nc + order(s), M_HEADS + h)),
                pl.BlockSpec((CHUNK, dh), lambda bi, h, s: (bi * nc + order(s), 2 * M_HEADS + h)),
                pl.BlockSpec((None, CHUNK, 4), lambda bi, h, s: (h, bi * nc + order(s), 0)),
                pl.BlockSpec((None, 4, CHUNK), lambda bi, h, s: (h, 0, bi * nc + order(s)))]

    def ospec(order):
        return pl.BlockSpec((CHUNK, dh), lambda bi, h, s: (bi * nc + order(s), h))

    state = [pltpu.VMEM((dh, dh), F32), pltpu.VMEM((1, dh), F32), pltpu.VMEM((1, 1), F32)]
    return pl.pallas_call(
        _scan_kernel,
        grid=(b, M_HEADS, nc),
        in_specs=specs(fwd) + specs(bwd),
        out_specs=[ospec(fwd), ospec(bwd)],
        out_shape=[jax.ShapeDtypeStruct((ntot, d), F32)] * 2,
        scratch_shapes=state + state,
        compiler_params=_cparams(("arbitrary", "arbitrary", "arbitrary")),
    )(qk, qk, p, gcol, grow, qk, qk, p, gcol, grow)


def _qkprep_kernel(p_ref, gq_ref, gk_ref, cos_ref, sin_ref, o_ref, *, nc, nblk_q, q_scale):
    i = pl.program_id(0)
    is_ctx = (i % nc) == 0
    cos = jnp.where(is_ctx, 1.0, cos_ref[...])
    sin = jnp.where(is_ctx, 0.0, sin_ref[...])
    lane = lax.broadcasted_iota(I32, (CHUNK, LANES), 1)
    first_half = (lane % 32) < 16
    r_i = lax.broadcasted_iota(I32, (LANES, LANES), 0)
    c_i = lax.broadcasted_iota(I32, (LANES, LANES), 1)
    group = jnp.where((r_i // D_HEAD_DIM) == (c_i // D_HEAD_DIM), 1.0, 0.0).astype(BF16)
    for blk in range(2 * nblk_q):
        x = p_ref[:, blk * LANES:(blk + 1) * LANES].astype(F32)
        ss = x * x
        ss_hi = ss.astype(BF16)
        ss_lo = (ss - ss_hi.astype(F32)).astype(BF16)
        ssum = (jnp.dot(ss_hi, group, preferred_element_type=F32)
                + jnp.dot(ss_lo, group, preferred_element_type=F32))
        g = gq_ref[...] if blk < nblk_q else gk_ref[...]
        xn = x * lax.rsqrt(ssum * (1.0 / D_HEAD_DIM) + EPS) * g
        partner = jnp.where(first_half, pltpu.roll(xn, LANES - 16, 1), pltpu.roll(xn, 16, 1))
        y = xn * cos + partner * sin
        if blk < nblk_q:
            y = y * q_scale
        o_ref[:, blk * LANES:(blk + 1) * LANES] = y.astype(o_ref.dtype)


def _diff_qkprep(p, q_g, k_g, cos_t, sin_t, *, nc, d):
    ntot = p.shape[0]
    g2 = lambda g: jnp.tile(g.astype(F32), 2).reshape(1, LANES)
    return pl.pallas_call(
        functools.partial(_qkprep_kernel, nc=nc, nblk_q=d // LANES, q_scale=D_HEAD_DIM ** -0.5),
        grid=(ntot // CHUNK,),
        in_specs=[pl.BlockSpec((CHUNK, 2 * d), lambda i: (i, 0)),
                  pl.BlockSpec((1, LANES), lambda i: (0, 0)),
                  pl.BlockSpec((1, LANES), lambda i: (0, 0)),
                  pl.BlockSpec((CHUNK, LANES), lambda i: (jnp.maximum(i % nc - 1, 0), 0)),
                  pl.BlockSpec((CHUNK, LANES), lambda i: (jnp.maximum(i % nc - 1, 0), 0))],
        out_specs=pl.BlockSpec((CHUNK, 2 * d), lambda i: (i, 0)),
        out_shape=jax.ShapeDtypeStruct((ntot, 2 * d), BF16),
        compiler_params=_cparams(("arbitrary",)),
    )(p, g2(q_g), g2(k_g), cos_t, sin_t)


def _rope_tables(t):
    rows = t // GRID_W
    row = jnp.repeat(jnp.arange(rows), GRID_W).astype(F32)[:, None]
    col = jnp.tile(jnp.arange(GRID_W), rows).astype(F32)[:, None]
    nf = D_HEAD_DIM // 4
    inv = ROPE_BASE ** (-jnp.arange(nf, dtype=F32) / nf)
    ar, ac = row * inv, col * inv
    cos64 = jnp.concatenate([jnp.cos(ar), jnp.cos(ar), jnp.cos(ac), jnp.cos(ac)], axis=-1)
    sin64 = jnp.concatenate([-jnp.sin(ar), jnp.sin(ar), -jnp.sin(ac), jnp.sin(ac)], axis=-1)
    return jnp.tile(cos64, (1, 2)), jnp.tile(sin64, (1, 2))


def _attn_kernel(lam_ref, q_ref, k_ref, v_ref, sg_ref, o_ref, m1, l1, a1, m2, l2, a2, *, n_lat, out_scale):
    c = pl.program_id(2)
    q = q_ref[...]
    lane = lax.broadcasted_iota(I32, q.shape, 1)
    zero = jnp.zeros_like(q)
    qs = (jnp.where(lane < D_HEAD_DIM, q, zero), jnp.where(lane >= D_HEAD_DIM, q, zero))
    for m_r, l_r, a_r in ((m1, l1, a1), (m2, l2, a2)):
        m_r[...] = jnp.full(m_r.shape, NEG, F32)
        l_r[...] = jnp.zeros(l_r.shape, F32)
        a_r[...] = jnp.zeros(a_r.shape, F32)

    def update(kc, vc):
        for qm, m_r, l_r, a_r in ((qs[0], m1, l1, a1), (qs[1], m2, l2, a2)):
            s = _nt_dot(qm, kc)
            m_old = m_r[...]
            m_new = jnp.maximum(m_old, jnp.max(s, axis=1, keepdims=True))
            alpha = jnp.exp(m_old - m_new)
            p = jnp.exp(s - m_new)
            l_r[...] = alpha * l_r[...] + jnp.sum(p, axis=1, keepdims=True)
            a_r[...] = alpha * a_r[...] + jnp.dot(p.astype(BF16), vc, preferred_element_type=F32)
            m_r[...] = m_new

    update(k_ref[0:CHUNK, :], v_ref[0:CHUNK, :])

    def body(j, carry):
        start = pl.multiple_of(CHUNK + j * KV_CHUNK, CHUNK)
        update(k_ref[pl.ds(start, KV_CHUNK), :], v_ref[pl.ds(start, KV_CHUNK), :])
        return carry

    lax.fori_loop(0, jnp.where(c == 0, 0, n_lat), body, 0)
    o = a1[...] / l1[...] - lam_ref[0] * (a2[...] / l2[...])
    o = o * lax.rsqrt(jnp.mean(o * o, axis=-1, keepdims=True) + EPS) * sg_ref[...] * out_scale
    o_ref[...] = o.astype(o_ref.dtype)


def _diff_attention(qk, p, lam, sub_g, *, b, s, nc, d, lambda_init):
    ntot = qk.shape[0]
    nh = d // LANES
    n_lat = (s - CHUNK) // KV_CHUNK
    stat = pltpu.VMEM((CHUNK, 1), F32)
    acc = pltpu.VMEM((CHUNK, LANES), F32)
    grid_spec = pltpu.PrefetchScalarGridSpec(
        num_scalar_prefetch=1,
        grid=(b, nh, nc),
        in_specs=[pl.BlockSpec((CHUNK, LANES), lambda bi, h, c, lam_r: (bi * nc + c, h)),
                  pl.BlockSpec((s, LANES), lambda bi, h, c, lam_r: (bi, nh + h)),
                  pl.BlockSpec((s, LANES), lambda bi, h, c, lam_r: (bi, 2 * nh + h)),
                  pl.BlockSpec((1, LANES), lambda bi, h, c, lam_r: (0, 0))],
        out_specs=pl.BlockSpec((CHUNK, LANES), lambda bi, h, c, lam_r: (bi * nc + c, h)),
        scratch_shapes=[stat, stat, acc, stat, stat, acc])
    return pl.pallas_call(
        functools.partial(_attn_kernel, n_lat=n_lat, out_scale=1.0 - lambda_init),
        grid_spec=grid_spec,
        out_shape=jax.ShapeDtypeStruct((ntot, d), BF16),
        compiler_params=_cparams(("arbitrary", "arbitrary", "arbitrary")),
    )(lam, qk, qk, p, sub_g.reshape(1, LANES).astype(F32))


def _out_kernel(*refs, tm, nt, cl, mlstm):
    if mlstm:
        (rf_ref, rb_ref, op_ref, og_ref, w_ref, x_ref, g2_ref, mb_ref, mc_ref, rwh_ref, rwl_ref, rb2_ref,
         xo_ref, h_ref, lg_ref) = refs
        r = rf_ref[...] + rb_ref[...]
        dh = r.shape[1] // M_HEADS
        parts = []
        for hh in range(M_HEADS):
            rr = r[:, hh * dh:(hh + 1) * dh]
            parts.append(rr * lax.rsqrt(jnp.mean(rr * rr, axis=-1, keepdims=True) + EPS))
        rn = jnp.concatenate(parts, axis=1) * og_ref[...]
        u = (jax.nn.sigmoid(op_ref[...].astype(F32)) * rn).astype(BF16)
    else:
        (u_ref, w_ref, x_ref, g2_ref, mb_ref, mc_ref, rwh_ref, rwl_ref, rb2_ref,
         xo_ref, h_ref, lg_ref) = refs
        u = u_ref[...]
    i = pl.program_id(0)
    is_ctx = _ctx_rows(i % nt, tm, cl)
    y = jnp.dot(u, w_ref[...], preferred_element_type=F32)
    x = x_ref[...] + _row_mod(mb_ref, mc_ref, is_ctx, 2) * y
    xo_ref[...] = x
    hn = x * lax.rsqrt(jnp.mean(x * x, axis=-1, keepdims=True) + EPS) * g2_ref[...]
    h = hn * (1.0 + _row_mod(mb_ref, mc_ref, is_ctx, 4)) + _row_mod(mb_ref, mc_ref, is_ctx, 3)
    h_ref[...] = h
    h_hi = h.astype(BF16)
    h_lo = (h - h_hi.astype(F32)).astype(BF16)
    lg_ref[...] = (jnp.dot(h_hi, rwh_ref[...], preferred_element_type=F32)
                   + jnp.dot(h_lo, rwh_ref[...], preferred_element_type=F32)
                   + jnp.dot(h_hi, rwl_ref[...], preferred_element_type=F32) + rb2_ref[...])


def _out_proj(u_args, w_out, x, g2, mods, rw_hi, rw_lo, r_bias, *, b, s, cl, tm, mlstm):
    ntot, d = x.shape
    nt = s // tm
    row = lambda i: (i, 0)
    const = lambda i: (0, 0)
    if mlstm:
        rf, rb, p, out_g = u_args
        u_specs = [pl.BlockSpec((tm, d), row), pl.BlockSpec((tm, d), row),
                   pl.BlockSpec((tm, d), lambda i: (i, 3)), pl.BlockSpec((1, d), const)]
        u_in = [rf, rb, p, out_g.reshape(1, d)]
    else:
        u_specs = [pl.BlockSpec((tm, d), row)]
        u_in = list(u_args)
    in_specs = u_specs + [pl.BlockSpec((d, d), const),
                          pl.BlockSpec((tm, d), row),
                          pl.BlockSpec((1, d), const),
                          pl.BlockSpec((None, 6, d), lambda i: (i // nt, 0, 0)),
                          pl.BlockSpec((None, 6, d), lambda i: (b, 0, 0)),
                          pl.BlockSpec((d, LANES), const),
                          pl.BlockSpec((d, LANES), const),
                          pl.BlockSpec((1, LANES), const)]
    return pl.pallas_call(
        functools.partial(_out_kernel, tm=tm, nt=nt, cl=cl, mlstm=mlstm),
        grid=(ntot // tm,),
        in_specs=in_specs,
        out_specs=[pl.BlockSpec((tm, d), row), pl.BlockSpec((tm, d), row), pl.BlockSpec((tm, LANES), row)],
        out_shape=[jax.ShapeDtypeStruct((ntot, d), F32), jax.ShapeDtypeStruct((ntot, d), F32),
                   jax.ShapeDtypeStruct((ntot, LANES), F32)],
        compiler_params=_cparams(("arbitrary",)),
    )(*u_in, w_out, x, g2.reshape(1, d), mods, mods, rw_hi, rw_lo, r_bias)


def _lane_pick(lane, pairs):
    out = jnp.zeros(lane.shape, F32)
    for idx, val in pairs:
        out = jnp.where(lane == idx, val, out)
    return out


def _route_kernel(lg_ref, rt_ref, cnt_ref):
    @pl.when(pl.program_id(0) == 0)
    def _():
        cnt_ref[...] = jnp.zeros(cnt_ref.shape, F32)

    lg = lg_ref[...]
    lane = lax.broadcasted_iota(I32, lg.shape, 1)
    lane_f = lane.astype(F32)
    big = 1e9
    is_g = lane < N_GROUPS
    gl = jnp.where(is_g, lg, NEG)
    gmax = jnp.max(gl, axis=1, keepdims=True)
    gidx = jnp.min(jnp.where(gl == gmax, lane_f, big), axis=1, keepdims=True)
    gsum = jnp.sum(jnp.where(is_g, jnp.exp(gl - gmax), 0.0), axis=1, keepdims=True)
    gp_top = 1.0 / gsum
    e_lane = lane_f - float(N_GROUPS)
    lo = gidx * float(EXPERTS_PER_GROUP)
    in_grp = jnp.where(e_lane >= lo, jnp.where(e_lane < lo + float(EXPERTS_PER_GROUP), 1.0, 0.0), 0.0) > 0.5
    el = jnp.where(in_grp, lg, NEG)
    v0 = jnp.max(el, axis=1, keepdims=True)
    i0 = jnp.min(jnp.where(el == v0, lane_f, big), axis=1, keepdims=True)
    el2 = jnp.where(lane_f == i0, NEG, el)
    v1 = jnp.max(el2, axis=1, keepdims=True)
    i1 = jnp.min(jnp.where(el2 == v1, lane_f, big), axis=1, keepdims=True)
    e1 = jnp.exp(v1 - v0)
    g0 = gp_top / (1.0 + e1)
    g1 = gp_top * e1 / (1.0 + e1)
    eid0 = i0 - float(N_GROUPS)
    eid1 = i1 - float(N_GROUPS)
    rt_ref[...] = _lane_pick(lane, ((0, eid0), (1, eid1), (2, g0), (3, g1)))
    hot = jnp.where(lane_f == eid0, 1.0, 0.0) + jnp.where(lane_f == eid1, 1.0, 0.0)
    cnt_ref[0:1, :] += jnp.sum(hot, axis=0, keepdims=True)


def _route(logits):
    ntot = logits.shape[0]
    return pl.pallas_call(
        _route_kernel,
        grid=(ntot // CHUNK,),
        in_specs=[pl.BlockSpec((CHUNK, LANES), lambda i: (i, 0))],
        out_specs=[pl.BlockSpec((CHUNK, LANES), lambda i: (i, 0)), pl.BlockSpec((8, LANES), lambda i: (0, 0))],
        out_shape=[jax.ShapeDtypeStruct((ntot, LANES), F32), jax.ShapeDtypeStruct((8, LANES), F32)],
        compiler_params=_cparams(("arbitrary",)),
    )(logits)


def _slot_kernel(rt_ref, cnt_ref, o_ref, run_ref):
    @pl.when(pl.program_id(0) == 0)
    def _():
        cnt = cnt_ref[0:1, :].astype(I32)
        padded = (((cnt + (MOE_BLOCK - 1)) // MOE_BLOCK) * MOE_BLOCK).astype(F32)
        r_i = lax.broadcasted_iota(I32, (LANES, LANES), 0)
        c_i = lax.broadcasted_iota(I32, (LANES, LANES), 1)
        col = jnp.sum(jnp.where(r_i == c_i, jnp.broadcast_to(padded, (LANES, LANES)), 0.0), axis=1, keepdims=True)
        run_ref[...] = jnp.sum(jnp.where(r_i < c_i, col, 0.0), axis=0, keepdims=True)

    rt = rt_ref[...]
    lane = lax.broadcasted_iota(I32, rt.shape, 1)
    lane_f = lane.astype(F32)
    hot0 = lane_f == rt[:, 0:1]
    hot1 = lane_f == rt[:, 1:2]
    both = jnp.where(hot0, 1.0, 0.0) + jnp.where(hot1, 1.0, 0.0)
    t_i = lax.broadcasted_iota(I32, (CHUNK, CHUNK), 0)
    s_i = lax.broadcasted_iota(I32, (CHUNK, CHUNK), 1)
    before = jnp.where(s_i < t_i, 1.0, 0.0).astype(BF16)
    base = run_ref[...] + jnp.dot(before, both.astype(BF16), preferred_element_type=F32)
    d0 = jnp.sum(jnp.where(hot0, base, 0.0), axis=1, keepdims=True)
    d1 = jnp.sum(jnp.where(hot1, base, 0.0), axis=1, keepdims=True)
    o_ref[...] = _lane_pick(lane, ((0, d0), (1, d1))).astype(I32)
    run_ref[...] += jnp.sum(both, axis=0, keepdims=True)


def _slots(route, counts):
    ntot = route.shape[0]
    return pl.pallas_call(
        _slot_kernel,
        grid=(ntot // CHUNK,),
        in_specs=[pl.BlockSpec((CHUNK, LANES), lambda i: (i, 0)), pl.BlockSpec((8, LANES), lambda i: (0, 0))],
        out_specs=pl.BlockSpec((CHUNK, LANES), lambda i: (i, 0)),
        out_shape=jax.ShapeDtypeStruct((ntot, LANES), I32),
        scratch_shapes=[pltpu.VMEM((1, LANES), F32)],
        compiler_params=_cparams(("arbitrary",)),
    )(route, counts)


def _dispatch_kernel(dest_ref, h_ref, xs_in_ref, xs_ref, sem):
    del xs_in_ref

    def row_copy(r, k):
        return pltpu.make_async_copy(h_ref.at[pl.ds(r, 1), :], xs_ref.at[pl.ds(dest_ref[0, 2 * r + k], 1), :], sem)

    def start(r, carry):
        row_copy(r, 0).start()
        row_copy(r, 1).start()
        return carry

    def wait(r, carry):
        row_copy(r, 0).wait()
        row_copy(r, 1).wait()
        return carry

    lax.fori_loop(0, CHUNK, start, 0)
    lax.fori_loop(0, CHUNK, wait, 0)


def _dispatch(dest, h, cap):
    ntot, d = h.shape
    return pl.pallas_call(
        _dispatch_kernel,
        grid=(ntot // CHUNK,),
        in_specs=[pl.BlockSpec((None, 1, 2 * CHUNK), lambda i: (i, 0, 0), memory_space=pltpu.SMEM),
                  pl.BlockSpec((CHUNK, d), lambda i: (i, 0)),
                  pl.BlockSpec(memory_space=pl.ANY)],
        out_specs=pl.BlockSpec(memory_space=pl.ANY),
        out_shape=jax.ShapeDtypeStruct((cap, d), F32),
        scratch_shapes=[pltpu.SemaphoreType.DMA],
        input_output_aliases={2: 0},
        compiler_params=_cparams(("arbitrary",)),
    )(dest, h, jnp.zeros((cap, d), F32))


def _expert_kernel(be_ref, xs_ref, wgu_ref, wdn_ref, ys_ref, wgu_bf, wdn_bf):
    j = pl.program_id(0)
    changed = jnp.logical_or(j == 0, be_ref[j] != be_ref[jnp.maximum(j - 1, 0)])

    @pl.when(changed)
    def _():
        wgu_bf[...] = wgu_ref[...].astype(BF16)
        wdn_bf[...] = wdn_ref[...].astype(BF16)

    gu = jnp.dot(xs_ref[...].astype(BF16), wgu_bf[...], preferred_element_type=F32)
    g = gu[:, :D_EXPERT]
    a = (g * jax.nn.sigmoid(g) * gu[:, D_EXPERT:]).astype(BF16)
    ys_ref[...] = jnp.dot(a, wdn_bf[...], preferred_element_type=F32)


def _experts(blk_e, xs, w_gu, w_dn):
    cap, d = xs.shape
    grid_spec = pltpu.PrefetchScalarGridSpec(
        num_scalar_prefetch=1,
        grid=(cap // MOE_BLOCK,),
        in_specs=[pl.BlockSpec((MOE_BLOCK, d), lambda j, be: (j, 0)),
                  pl.BlockSpec((None, d, 2 * D_EXPERT), lambda j, be: (be[j], 0, 0)),
                  pl.BlockSpec((None, D_EXPERT, d), lambda j, be: (be[j], 0, 0))],
        out_specs=pl.BlockSpec((MOE_BLOCK, d), lambda j, be: (j, 0)),
        scratch_shapes=[pltpu.VMEM((d, 2 * D_EXPERT), BF16), pltpu.VMEM((D_EXPERT, d), BF16)])
    return pl.pallas_call(
        _expert_kernel,
        grid_spec=grid_spec,
        out_shape=jax.ShapeDtypeStruct((cap, d), F32),
        compiler_params=_cparams(("arbitrary",)),
    )(blk_e, xs, w_gu, w_dn)


def _combine_kernel(dest_ref, rt_ref, x_ref, mb_ref, mc_ref, ys_ref, xo_ref, buf, sem, *, nc):
    def row_copy(r, k):
        return pltpu.make_async_copy(ys_ref.at[pl.ds(dest_ref[0, 2 * r + k], 1), :],
                                     buf.at[pl.ds(k * CHUNK + r, 1), :], sem)

    def start(r, carry):
        row_copy(r, 0).start()
        row_copy(r, 1).start()
        return carry

    def wait(r, carry):
        row_copy(r, 0).wait()
        row_copy(r, 1).wait()
        return carry

    lax.fori_loop(0, CHUNK, start, 0)
    lax.fori_loop(0, CHUNK, wait, 0)
    rt = rt_ref[...]
    y = rt[:, 2:3] * buf[0:CHUNK, :] + rt[:, 3:4] * buf[CHUNK:2 * CHUNK, :]
    is_ctx = (pl.program_id(0) % nc) == 0
    gate = jnp.where(is_ctx, mc_ref[5:6, :], mb_ref[5:6, :])
    xo_ref[...] = x_ref[...] + gate * y


def _combine(dest, route, x, mods, ys, *, b, nc):
    ntot, d = x.shape
    return pl.pallas_call(
        functools.partial(_combine_kernel, nc=nc),
        grid=(ntot // CHUNK,),
        in_specs=[pl.BlockSpec((None, 1, 2 * CHUNK), lambda i: (i, 0, 0), memory_space=pltpu.SMEM),
                  pl.BlockSpec((CHUNK, LANES), lambda i: (i, 0)),
                  pl.BlockSpec((CHUNK, d), lambda i: (i, 0)),
                  pl.BlockSpec((None, 6, d), lambda i: (i // nc, 0, 0)),
                  pl.BlockSpec((None, 6, d), lambda i: (b, 0, 0)),
                  pl.BlockSpec(memory_space=pl.ANY)],
        out_specs=pl.BlockSpec((CHUNK, d), lambda i: (i, 0)),
        out_shape=jax.ShapeDtypeStruct((ntot, d), F32),
        scratch_shapes=[pltpu.VMEM((2 * CHUNK, d), F32), pltpu.SemaphoreType.DMA],
        compiler_params=_cparams(("arbitrary",)),
    )(dest, route, x, mods, mods, ys)


def _moe(x, h, logits, mods, w_gu, w_dn, *, b, nc):
    ntot = x.shape[0]
    route, counts = _route(logits)
    slot = _slots(route, counts)
    dest = slot[:, :2].reshape(ntot // CHUNK, 1, 2 * CHUNK)
    n_assign = 2 * ntot
    cap = -(-n_assign // MOE_BLOCK) * MOE_BLOCK + N_EXPERTS * MOE_BLOCK
    cnt = counts[0, :N_EXPERTS].astype(I32)
    pad_end = jnp.cumsum((cnt + MOE_BLOCK - 1) // MOE_BLOCK * MOE_BLOCK)
    blk_start = jnp.arange(cap // MOE_BLOCK, dtype=I32) * MOE_BLOCK
    blk_e = jnp.minimum(jnp.sum((pad_end[None, :] <= blk_start[:, None]).astype(I32), axis=1), N_EXPERTS - 1)
    xs = _dispatch(dest, h, cap)
    ys = _experts(blk_e.astype(I32), xs, w_gu, w_dn)
    return _combine(dest, route, x, mods, ys, b=b, nc=nc)


def _pick_tile(s, options):
    for t in options:
        if s % t == 0:
            return t
    raise ValueError(f"no row tile for sequence length {s}")


def kernel(x, c, ctx, c_ctx, ada_w, ada_b, norm1_g, norm2_g, m_w_in, m_conv_w, m_conv_b, m_gate_b, m_out_g,
           m_w_out, d_w_in, d_q_g, d_k_g, d_lq1, d_lk1, d_lq2, d_lk2, d_sub_g, d_w_out, r_grp_w, r_grp_b,
           r_exp_w, r_exp_b, e_w_gu, e_w_dn):
    b, t, d = x.shape
    cl = ctx.shape[1]
    depth = ada_w.shape[0]
    assert cl == CHUNK and t % KV_CHUNK == 0 and d == 8 * LANES and b + 1 <= 8
    s = cl + t
    nc = s // CHUNK
    tm_in = _pick_tile(s, (1280, 640, 256))
    tm_out = _pick_tile(s, (640, 256))

    xs = jnp.concatenate([ctx, x], axis=1).reshape(b * s, d)
    cond = jnp.zeros((8, d), F32).at[:b].set(c).at[b].set(c_ctx)
    mods_all = _modulation(cond, ada_w, ada_b)
    cos_t, sin_t = _rope_tables(t)

    n_route = N_GROUPS + N_EXPERTS
    for i in range(depth):
        jm = i // 2
        mods = mods_all[i]
        rw = jnp.zeros((d, LANES), F32).at[:, :N_GROUPS].set(r_grp_w[i]).at[:, N_GROUPS:n_route].set(r_exp_w[i])
        rw_hi = rw.astype(BF16)
        rw_lo = (rw - rw_hi.astype(F32)).astype(BF16)
        r_bias = jnp.zeros((1, LANES), F32).at[0, :N_GROUPS].set(r_grp_b[i]).at[0, N_GROUPS:n_route].set(r_exp_b[i])
        if i % 2 == 0:
            w_in = m_w_in[jm]
            wg = jnp.zeros((d, LANES), F32).at[:, :4 * M_HEADS].set(w_in[:, 4 * d:]).astype(BF16)
            gb = jnp.zeros((1, LANES), F32).at[0, :4 * M_HEADS].set(m_gate_b[jm])
            p, gates = _in_proj(xs, norm1_g[i], mods, w_in[:, :4 * d].astype(BF16), b=b, s=s, cl=cl,
                                tm=tm_in, tn=1024, gates=(wg, gb))
            qk = _mlstm_conv(p, m_conv_w[jm], m_conv_b[jm], nc=nc, d=d)
            g4 = gates[:, :4 * M_HEADS].reshape(b * s, 4, M_HEADS)
            gcol = g4.transpose(2, 0, 1)
            grow = g4.transpose(2, 1, 0)
            rf, rb = _mlstm_scan(qk, p, gcol, grow, b=b, nc=nc, d=d)
            xs, h, logits = _out_proj((rf, rb, p, m_out_g[jm]), m_w_out[jm].astype(BF16), xs, norm2_g[i], mods,
                                      rw_hi, rw_lo, r_bias, b=b, s=s, cl=cl, tm=tm_out, mlstm=True)
        else:
            lambda_init = 0.8 - 0.6 * math.exp(-0.3 * i)
            p = _in_proj(xs, norm1_g[i], mods, d_w_in[jm].astype(BF16), b=b, s=s, cl=cl, tm=tm_in, tn=1024)
            qk = _diff_qkprep(p, d_q_g[jm], d_k_g[jm], cos_t, sin_t, nc=nc, d=d)
            lam = (jnp.exp(jnp.sum(d_lq1[jm] * d_lk1[jm])) - jnp.exp(jnp.sum(d_lq2[jm] * d_lk2[jm]))
                   + lambda_init).reshape(1).astype(F32)
            o = _diff_attention(qk, p, lam, d_sub_g[jm], b=b, s=s, nc=nc, d=d, lambda_init=lambda_init)
            xs, h, logits = _out_proj((o,), d_w_out[jm].astype(BF16), xs, norm2_g[i], mods,
                                      rw_hi, rw_lo, r_bias, b=b, s=s, cl=cl, tm=tm_out, mlstm=False)
        xs = _moe(xs, h, logits, mods, e_w_gu[i], e_w_dn[i], b=b, nc=nc)
    return xs.reshape(b, s, d)[:, cl:, :]
```

```python
import functools
import math

import jax
import jax.numpy as jnp
from jax import lax
from jax.experimental import pallas as pl
from jax.experimental.pallas import tpu as pltpu

F32 = jnp.float32
BF16 = jnp.bfloat16
I32 = jnp.int32
HIGHEST = lax.Precision.HIGHEST

EPS = 1e-6
GRID_W = 64
ROPE_BASE = 10000.0
M_HEADS = 4
M_CONV_W = 5
D_HEADS = 8
D_HEAD_DIM = 64
N_GROUPS = 4
EXPERTS_PER_GROUP = 8
N_EXPERTS = 32
D_EXPERT = 512

LANES = 128
CHUNK = 256
ROW_TILE = 512
Q_TILE = 512
KV_CHUNK = 512
SOFTMAX_ROWS = 64
MOE_BLOCK = 256
DMA_UNROLL = 8
NEG = -1e30
VMEM_LIMIT = 56 * 1024 * 1024


def _cparams(sem):
    return pltpu.CompilerParams(dimension_semantics=sem, vmem_limit_bytes=VMEM_LIMIT)


def _nt_dot(a, b):
    return lax.dot_general(a, b, (((1,), (1,)), ((), ())), preferred_element_type=F32)


def _tn_dot(a, b):
    return lax.dot_general(a, b, (((0,), (0,)), ((), ())), preferred_element_type=F32)


def _mod_spec(d, tiles_per_batch, n_lat_tiles, b):
    return pl.BlockSpec((None, 6, d), lambda i, *_: (jnp.where(i < n_lat_tiles, i // tiles_per_batch, b), 0, 0))


def _mod_kernel(s_ref, w_ref, b_ref, o_ref):
    s = s_ref[...]
    s = s * jax.nn.sigmoid(s)
    o_ref[...] = jnp.dot(s, w_ref[...], precision=HIGHEST, preferred_element_type=F32) + b_ref[...]


def _modulation(cond, ada_w, ada_b):
    depth, d, n = ada_w.shape
    tn = n // 4
    out = pl.pallas_call(
        _mod_kernel,
        grid=(depth, n // tn),
        in_specs=[pl.BlockSpec((8, d), lambda l, j: (0, 0)),
                  pl.BlockSpec((None, d, tn), lambda l, j: (l, 0, j)),
                  pl.BlockSpec((None, 1, tn), lambda l, j: (l, 0, j))],
        out_specs=pl.BlockSpec((None, 8, tn), lambda l, j: (l, 0, j)),
        out_shape=jax.ShapeDtypeStruct((depth, 8, n), F32),
        compiler_params=_cparams(("arbitrary", "arbitrary")),
    )(cond, ada_w, ada_b.reshape(depth, 1, n))
    return out.reshape(depth, 8, 6, d)


def _in_kernel(*refs, has_gates):
    if has_gates:
        x_ref, g_ref, mod_ref, w_ref, wg_ref, gb_ref, o_ref, go_ref, h_ref = refs
    else:
        x_ref, g_ref, mod_ref, w_ref, o_ref, h_ref = refs

    @pl.when(pl.program_id(1) == 0)
    def _():
        x = x_ref[...]
        y = x * lax.rsqrt(jnp.mean(x * x, axis=-1, keepdims=True) + EPS) * g_ref[...]
        hb = (y * (1.0 + mod_ref[1:2, :]) + mod_ref[0:1, :]).astype(BF16)
        h_ref[...] = hb
        if has_gates:
            go_ref[...] = jnp.dot(hb, wg_ref[...], preferred_element_type=F32) + gb_ref[...]

    o_ref[...] = jnp.dot(h_ref[...], w_ref[...], preferred_element_type=F32).astype(o_ref.dtype)


def _in_proj(x, g, mods, w, *, b, t, tn, gates=None):
    ntot, d = x.shape
    n = w.shape[1]
    tm = ROW_TILE
    has_gates = gates is not None
    in_specs = [pl.BlockSpec((tm, d), lambda i, j: (i, 0)),
                pl.BlockSpec((1, d), lambda i, j: (0, 0)),
                _mod_spec(d, t // tm, b * t // tm, b),
                pl.BlockSpec((d, tn), lambda i, j: (0, j))]
    args = [x, g.reshape(1, d), mods, w]
    out_specs = pl.BlockSpec((tm, tn), lambda i, j: (i, j))
    out_shape = jax.ShapeDtypeStruct((ntot, n), BF16)
    if has_gates:
        wg, gb = gates
        in_specs += [pl.BlockSpec((d, LANES), lambda i, j: (0, 0)),
                     pl.BlockSpec((1, LANES), lambda i, j: (0, 0))]
        args += [wg, gb]
        out_specs = [out_specs, pl.BlockSpec((tm, LANES), lambda i, j: (i, 0))]
        out_shape = [out_shape, jax.ShapeDtypeStruct((ntot, LANES), F32)]
    return pl.pallas_call(
        functools.partial(_in_kernel, has_gates=has_gates),
        grid=(ntot // tm, n // tn),
        in_specs=in_specs,
        out_specs=out_specs,
        out_shape=out_shape,
        scratch_shapes=[pltpu.VMEM((tm, d), BF16)],
        compiler_params=_cparams(("arbitrary", "arbitrary")),
    )(*args)


def _conv_kernel(xm_ref, xp_ref, xn_ref, w_ref, b_ref, o_ref, *, ncl, n_lat_chunks, ncol_q, k_scale):
    i = pl.program_id(0)
    j = pl.program_id(1)
    c = i % ncl
    is_lat = i < n_lat_chunks
    left_ok = jnp.logical_and(is_lat, c >= 1)
    right_ok = jnp.logical_and(is_lat, c < ncl - 1)
    xm = xm_ref[...].astype(F32)
    xp = jnp.where(left_ok, xp_ref[...].astype(F32), 0.0)[14:16, :]
    xn = jnp.where(right_ok, xn_ref[...].astype(F32), 0.0)[0:2, :]
    ext = jnp.concatenate([xp, xm, xn], axis=0)
    rows = xm.shape[0]
    w = w_ref[...]
    y = b_ref[...] + w[0:1, :] * ext[0:rows, :]
    for k in range(1, M_CONV_W):
        y = y + w[k:k + 1, :] * ext[k:k + rows, :]
    y = y * jax.nn.sigmoid(y)
    y = y * jnp.where(j >= ncol_q, k_scale, 1.0)
    o_ref[...] = y.astype(o_ref.dtype)


def _mlstm_conv(p, conv_w, conv_b, *, b, t, d):
    ntot = p.shape[0]
    tn = 512
    nrb = ntot // 16
    return pl.pallas_call(
        functools.partial(_conv_kernel, ncl=t // CHUNK, n_lat_chunks=b * t // CHUNK, ncol_q=d // tn,
                          k_scale=(d // M_HEADS) ** -0.5),
        grid=(ntot // CHUNK, 2 * d // tn),
        in_specs=[pl.BlockSpec((CHUNK, tn), lambda i, j: (i, j)),
                  pl.BlockSpec((16, tn), lambda i, j: (jnp.maximum(i * (CHUNK // 16) - 1, 0), j)),
                  pl.BlockSpec((16, tn), lambda i, j: (jnp.minimum((i + 1) * (CHUNK // 16), nrb - 1), j)),
                  pl.BlockSpec((M_CONV_W, tn), lambda i, j: (0, j)),
                  pl.BlockSpec((1, tn), lambda i, j: (0, j))],
        out_specs=pl.BlockSpec((CHUNK, tn), lambda i, j: (i, j)),
        out_shape=jax.ShapeDtypeStruct((ntot, 2 * d), BF16),
        compiler_params=_cparams(("arbitrary", "arbitrary")),
    )(p, p, p, conv_w, conv_b.reshape(1, 2 * d))


def _log_sigmoid(x):
    return jnp.minimum(x, 0.0) - jnp.log(1.0 + jnp.exp(-jnp.abs(x)))


def _mlstm_chunk(q, k, v, gcol, grow, igate, fgate, mask, mask_t, ct_ref, n_ref, m_ref, o_ref):
    i_row = grow[igate:igate + 1, :]
    i_col = gcol[:, igate:igate + 1]
    f_row = _log_sigmoid(grow[fgate:fgate + 1, :])
    f_col = _log_sigmoid(gcol[:, fgate:fgate + 1])
    cum_col = jnp.sum(jnp.where(mask, f_row, 0.0), axis=1, keepdims=True)
    cum_row = jnp.sum(jnp.where(mask_t, f_col, 0.0), axis=0, keepdims=True)
    tot = jnp.sum(f_col, axis=0, keepdims=True)
    m_prev = m_ref[...]
    dmat = jnp.where(mask, cum_col - cum_row + i_row, NEG)
    m_inter = cum_col + m_prev
    m_out = jnp.maximum(m_inter, jnp.max(dmat, axis=1, keepdims=True))
    w_intra = jnp.exp(dmat - m_out)
    w_inter = jnp.exp(m_inter - m_out)
    sc = _nt_dot(q, k) * w_intra
    ct = ct_ref[...]
    n_row = n_ref[...]
    qf = q.astype(F32)
    kf = k.astype(F32)
    vf = v.astype(F32)
    num = (w_inter * jnp.dot(q, ct.astype(BF16), preferred_element_type=F32)
           + jnp.dot(sc.astype(BF16), v, preferred_element_type=F32))
    den = w_inter * jnp.sum(qf * n_row, axis=1, keepdims=True) + jnp.sum(sc, axis=1, keepdims=True)
    o_ref[...] = (num / jnp.maximum(jnp.abs(den), jnp.exp(-m_out))).astype(o_ref.dtype)
    g_col = tot - cum_col + i_col
    m_next = jnp.maximum(tot + m_prev, jnp.max(g_col, axis=0, keepdims=True))
    decay = jnp.exp(tot + m_prev - m_next)
    wg = jnp.exp(g_col - m_next)
    ct_ref[...] = decay * ct + _tn_dot(k, (wg * vf).astype(BF16))
    n_ref[...] = decay * n_row + jnp.sum(wg * kf, axis=0, keepdims=True)
    m_ref[...] = m_next


def _scan_kernel(qf_ref, kf_ref, vf_ref, gcf_ref, grf_ref, qb_ref, kb_ref, vb_ref, gcb_ref, grb_ref,
                 of_ref, ob_ref, ctf_ref, nf_ref, mf_ref, ctb_ref, nb_ref, mb_ref):
    @pl.when(pl.program_id(2) == 0)
    def _():
        for r in (ctf_ref, nf_ref, mf_ref, ctb_ref, nb_ref, mb_ref):
            r[...] = jnp.zeros(r.shape, r.dtype)

    t_i = lax.broadcasted_iota(I32, (CHUNK, CHUNK), 0)
    s_i = lax.broadcasted_iota(I32, (CHUNK, CHUNK), 1)
    lower = s_i <= t_i
    upper = s_i >= t_i
    _mlstm_chunk(qf_ref[...], kf_ref[...], vf_ref[...], gcf_ref[...], grf_ref[...], 0, 2,
                 lower, upper, ctf_ref, nf_ref, mf_ref, of_ref)
    _mlstm_chunk(qb_ref[...], kb_ref[...], vb_ref[...], gcb_ref[...], grb_ref[...], 1, 3,
                 upper, lower, ctb_ref, nb_ref, mb_ref, ob_ref)


def _mlstm_scan(qk, p, gcol, grow, *, b, t, d):
    ntot = qk.shape[0]
    dh = d // M_HEADS
    ncl = t // CHUNK
    ctx0 = b * ncl

    def fwd(bi, s):
        return jnp.where(s == 0, ctx0 + bi, bi * ncl + s - 1)

    def bwd(bi, s):
        return jnp.where(s == 0, ctx0 + bi, bi * ncl + ncl - s)

    def specs(order):
        return [pl.BlockSpec((CHUNK, dh), lambda bi, h, s: (order(bi, s), h)),
                pl.BlockSpec((CHUNK, dh), lambda bi, h, s: (order(bi, s), M_HEADS + h)),
                pl.BlockSpec((CHUNK, dh), lambda bi, h, s: (order(bi, s), 2 * M_HEADS + h)),
                pl.BlockSpec((None, CHUNK, 4), lambda bi, h, s: (h, order(bi, s), 0)),
                pl.BlockSpec((None, 4, CHUNK), lambda bi, h, s: (h, 0, order(bi, s)))]

    def ospec(order):
        return pl.BlockSpec((CHUNK, dh), lambda bi, h, s: (order(bi, s), h))

    state = [pltpu.VMEM((dh, dh), F32), pltpu.VMEM((1, dh), F32), pltpu.VMEM((1, 1), F32)]
    return pl.pallas_call(
        _scan_kernel,
        grid=(b, M_HEADS, ncl + 1),
        in_specs=specs(fwd) + specs(bwd),
        out_specs=[ospec(fwd), ospec(bwd)],
        out_shape=[jax.ShapeDtypeStruct((ntot, d), F32)] * 2,
        scratch_shapes=state + state,
        compiler_params=_cparams(("arbitrary", "arbitrary", "arbitrary")),
    )(qk, qk, p, gcol, grow, qk, qk, p, gcol, grow)


def _qkprep_kernel(p_ref, gq_ref, gk_ref, cos_ref, sin_ref, o_ref, *, n_lat_chunks, nblk_q, q_scale):
    is_ctx = pl.program_id(0) >= n_lat_chunks
    cos = jnp.where(is_ctx, 1.0, cos_ref[...])
    sin = jnp.where(is_ctx, 0.0, sin_ref[...])
    lane = lax.broadcasted_iota(I32, (CHUNK, LANES), 1)
    first_half = (lane % 32) < 16
    r_i = lax.broadcasted_iota(I32, (LANES, LANES), 0)
    c_i = lax.broadcasted_iota(I32, (LANES, LANES), 1)
    group = jnp.where((r_i // D_HEAD_DIM) == (c_i // D_HEAD_DIM), 1.0, 0.0).astype(BF16)
    for blk in range(2 * nblk_q):
        x = p_ref[:, blk * LANES:(blk + 1) * LANES].astype(F32)
        ss = x * x
        ss_hi = ss.astype(BF16)
        ss_lo = (ss - ss_hi.astype(F32)).astype(BF16)
        ssum = (jnp.dot(ss_hi, group, preferred_element_type=F32)
                + jnp.dot(ss_lo, group, preferred_element_type=F32))
        g = gq_ref[...] if blk < nblk_q else gk_ref[...]
        xn = x * lax.rsqrt(ssum * (1.0 / D_HEAD_DIM) + EPS) * g
        partner = jnp.where(first_half, pltpu.roll(xn, LANES - 16, 1), pltpu.roll(xn, 16, 1))
        y = xn * cos + partner * sin
        if blk < nblk_q:
            y = y * q_scale
        o_ref[:, blk * LANES:(blk + 1) * LANES] = y.astype(o_ref.dtype)


def _diff_qkprep(p, q_g, k_g, cos_t, sin_t, *, b, t, d):
    ntot = p.shape[0]
    ncl = t // CHUNK
    g2 = lambda g: jnp.tile(g.astype(F32), 2).reshape(1, LANES)
    return pl.pallas_call(
        functools.partial(_qkprep_kernel, n_lat_chunks=b * ncl, nblk_q=d // LANES,
                          q_scale=D_HEAD_DIM ** -0.5 * math.log2(math.e)),
        grid=(ntot // CHUNK,),
        in_specs=[pl.BlockSpec((CHUNK, 2 * d), lambda i: (i, 0)),
                  pl.BlockSpec((1, LANES), lambda i: (0, 0)),
                  pl.BlockSpec((1, LANES), lambda i: (0, 0)),
                  pl.BlockSpec((CHUNK, LANES), lambda i: (i % ncl, 0)),
                  pl.BlockSpec((CHUNK, LANES), lambda i: (i % ncl, 0))],
        out_specs=pl.BlockSpec((CHUNK, 2 * d), lambda i: (i, 0)),
        out_shape=jax.ShapeDtypeStruct((ntot, 2 * d), BF16),
        compiler_params=_cparams(("arbitrary",)),
    )(p, g2(q_g), g2(k_g), cos_t, sin_t)


def _rope_tables(t):
    rows = t // GRID_W
    row = jnp.repeat(jnp.arange(rows), GRID_W).astype(F32)[:, None]
    col = jnp.tile(jnp.arange(GRID_W), rows).astype(F32)[:, None]
    nf = D_HEAD_DIM // 4
    inv = ROPE_BASE ** (-jnp.arange(nf, dtype=F32) / nf)
    ar, ac = row * inv, col * inv
    cos64 = jnp.concatenate([jnp.cos(ar), jnp.cos(ar), jnp.cos(ac), jnp.cos(ac)], axis=-1)
    sin64 = jnp.concatenate([-jnp.sin(ar), jnp.sin(ar), -jnp.sin(ac), jnp.sin(ac)], axis=-1)
    return jnp.tile(cos64, (1, 2)), jnp.tile(sin64, (1, 2))


def _attn_kernel(*refs, n_lat, out_scale):
    if n_lat:
        lam_ref, q_ref, kc_ref, vtc_ref, kl_ref, vtl_ref, sg_ref, o_ref, m1, l1, a1, m2, l2, a2 = refs[:14]
        bufs = refs[14:]
    else:
        lam_ref, q_ref, kc_ref, vtc_ref, sg_ref, o_ref, m1, l1, a1, m2, l2, a2 = refs[:12]
        bufs = refs[12:]
    sets = [tuple(bufs[i + 2 * k:i + 2 * k + 2] for k in range(4)) for i in range(0, len(bufs), 8)]
    q = q_ref[...]
    lane = lax.broadcasted_iota(I32, q.shape, 1)
    zero = jnp.zeros_like(q)
    qs = (jnp.where(lane < D_HEAD_DIM, q, zero), jnp.where(lane >= D_HEAD_DIM, q, zero))
    stats = ((m1, l1), (m2, l2))
    accs = (a1, a2)
    for (m_r, l_r), a_r in zip(stats, accs):
        m_r[...] = jnp.full(m_r.shape, NEG, F32)
        l_r[...] = jnp.zeros(l_r.shape, F32)
        a_r[...] = jnp.zeros(a_r.shape, F32)

    def scores(keys, s_bufs, mx_bufs):
        for qm, s_ref, mx_ref in zip(qs, s_bufs, mx_bufs):
            st = _nt_dot(keys, qm)
            s_ref[...] = st
            mx = st[0:8]
            for r in range(8, st.shape[0], 8):
                mx = jnp.maximum(mx, st[r:r + 8])
            mx_ref[...] = mx

    def softmax(s_bufs, mx_bufs, p_bufs, al_bufs):
        for s_ref, mx_ref, p_ref, al_ref, (m_r, l_r) in zip(s_bufs, mx_bufs, p_bufs, al_bufs, stats):
            m_old = m_r[...]
            m_new = jnp.maximum(m_old, jnp.max(mx_ref[...], axis=0, keepdims=True))
            alpha = jnp.exp2(m_old - m_new)
            m_r[...] = m_new
            al_ref[...] = alpha
            part = jnp.zeros(mx_ref.shape, F32)
            for r0 in range(0, s_ref.shape[0], SOFTMAX_ROWS):
                p = jnp.exp2(s_ref[r0:r0 + SOFTMAX_ROWS, :] - m_new)
                for r in range(0, SOFTMAX_ROWS, 8):
                    part = part + p[r:r + 8]
                p_ref[r0:r0 + SOFTMAX_ROWS, :] = p.astype(BF16)
            l_r[...] = alpha * l_r[...] + jnp.sum(part, axis=0, keepdims=True)

    def weigh(vt, p_bufs, al_bufs):
        for p_ref, al_ref, a_r in zip(p_bufs, al_bufs, accs):
            a_r[...] = al_ref[...] * a_r[...] + jnp.dot(vt, p_ref[...], preferred_element_type=F32)

    s_c, mx_c, p_c, al_c = sets[0]
    scores(kc_ref[...], s_c, mx_c)
    softmax(s_c, mx_c, p_c, al_c)
    weigh(vtc_ref[...], p_c, al_c)

    if n_lat:
        def keys(j):
            return kl_ref[pl.ds(pl.multiple_of(j * KV_CHUNK, KV_CHUNK), KV_CHUNK), :]

        (s_a, mx_a, p_a, al_a), (s_b, mx_b, p_b, al_b) = sets[1:]
        scores(keys(0), s_a, mx_a)
        softmax(s_a, mx_a, p_a, al_a)
        scores(keys(1), s_b, mx_b)

        def body(jj, carry):
            j = 2 * jj
            softmax(s_b, mx_b, p_b, al_b)
            scores(keys(j + 2), s_a, mx_a)
            weigh(vtl_ref[j], p_a, al_a)
            softmax(s_a, mx_a, p_a, al_a)
            scores(keys(j + 3), s_b, mx_b)
            weigh(vtl_ref[j + 1], p_b, al_b)
            return carry

        lax.fori_loop(0, n_lat // 2 - 1, body, 0)
        softmax(s_b, mx_b, p_b, al_b)
        weigh(vtl_ref[n_lat - 2], p_a, al_a)
        weigh(vtl_ref[n_lat - 1], p_b, al_b)
    o = a1[...] / l1[...] - lam_ref[0] * (a2[...] / l2[...])
    o = o * lax.rsqrt(jnp.mean(o * o, axis=0, keepdims=True) + EPS) * sg_ref[...] * out_scale
    o_ref[...] = o.T.astype(o_ref.dtype)


def _diff_attention(qk, vt_lat, vt_ctx, lam, sub_g, *, b, t, d, lambda_init, latent):
    nh = d // LANES
    ctx0 = b * t // CHUNK
    n_lat = t // KV_CHUNK if latent else 0
    tq = Q_TILE if latent else CHUNK
    nq = t // tq if latent else 1
    q_row = (lambda bi, c: bi * nq + c) if latent else (lambda bi, c: ctx0 + bi)
    in_specs = [pl.BlockSpec((tq, LANES), lambda bi, h, c, lam_r: (q_row(bi, c), h)),
                pl.BlockSpec((CHUNK, LANES), lambda bi, h, c, lam_r: (ctx0 + bi, nh + h)),
                pl.BlockSpec((None, LANES, CHUNK), lambda bi, h, c, lam_r: (bi, h, 0))]
    args = [qk, qk, vt_ctx]
    if latent:
        in_specs += [pl.BlockSpec((t, LANES), lambda bi, h, c, lam_r: (bi, nh + h)),
                     pl.BlockSpec((n_lat, LANES, KV_CHUNK), lambda bi, h, c, lam_r: (bi, h, 0))]
        args += [qk, vt_lat]
    in_specs.append(pl.BlockSpec((LANES, 1), lambda bi, h, c, lam_r: (0, 0)))
    args.append(sub_g.reshape(LANES, 1).astype(F32))
    stat = pltpu.VMEM((1, tq), F32)
    acc = pltpu.VMEM((LANES, tq), F32)

    def buffer_set(tk):
        return ([pltpu.VMEM((tk, tq), F32)] * 2 + [pltpu.VMEM((8, tq), F32)] * 2
                + [pltpu.VMEM((tk, tq), BF16)] * 2 + [stat] * 2)

    grid_spec = pltpu.PrefetchScalarGridSpec(
        num_scalar_prefetch=1,
        grid=(b, nh, nq),
        in_specs=in_specs,
        out_specs=pl.BlockSpec((tq, LANES), lambda bi, h, c, lam_r: (bi * nq + c, h)),
        scratch_shapes=[stat, stat, acc, stat, stat, acc] + buffer_set(CHUNK) + (buffer_set(KV_CHUNK) * 2 if latent else []))
    return pl.pallas_call(
        functools.partial(_attn_kernel, n_lat=n_lat, out_scale=1.0 - lambda_init),
        grid_spec=grid_spec,
        out_shape=jax.ShapeDtypeStruct((b * nq * tq, d), BF16),
        compiler_params=_cparams(("arbitrary", "arbitrary", "arbitrary")),
    )(lam, *args)


def _out_kernel(*refs, mlstm):
    if mlstm:
        (rf_ref, rb_ref, op_ref, og_ref, w_ref, x_ref, g2_ref, mod_ref, rwh_ref, rwl_ref, rb2_ref,
         xo_ref, h_ref, lg_ref) = refs
        r = rf_ref[...] + rb_ref[...]
        dh = r.shape[1] // M_HEADS
        parts = []
        for hh in range(M_HEADS):
            rr = r[:, hh * dh:(hh + 1) * dh]
            parts.append(rr * lax.rsqrt(jnp.mean(rr * rr, axis=-1, keepdims=True) + EPS))
        rn = jnp.concatenate(parts, axis=1) * og_ref[...]
        u = (jax.nn.sigmoid(op_ref[...].astype(F32)) * rn).astype(BF16)
    else:
        (u_ref, w_ref, x_ref, g2_ref, mod_ref, rwh_ref, rwl_ref, rb2_ref, xo_ref, h_ref, lg_ref) = refs
        u = u_ref[...]
    y = jnp.dot(u, w_ref[...], preferred_element_type=F32)
    x = x_ref[...] + mod_ref[2:3, :] * y
    xo_ref[...] = x
    hn = x * lax.rsqrt(jnp.mean(x * x, axis=-1, keepdims=True) + EPS) * g2_ref[...]
    h = hn * (1.0 + mod_ref[4:5, :]) + mod_ref[3:4, :]
    h_ref[...] = h
    h_hi = h.astype(BF16)
    h_lo = (h - h_hi.astype(F32)).astype(BF16)
    lg_ref[...] = (jnp.dot(h_hi, rwh_ref[...], preferred_element_type=F32)
                   + jnp.dot(h_lo, rwh_ref[...], preferred_element_type=F32)
                   + jnp.dot(h_hi, rwl_ref[...], preferred_element_type=F32) + rb2_ref[...])


def _out_proj(u_args, w_out, x, g2, mods, rw_hi, rw_lo, r_bias, *, b, t, mlstm):
    ntot, d = x.shape
    tm = ROW_TILE
    row = lambda i: (i, 0)
    const = lambda i: (0, 0)
    if mlstm:
        rf, rb, p, out_g = u_args
        u_specs = [pl.BlockSpec((tm, d), row), pl.BlockSpec((tm, d), row),
                   pl.BlockSpec((tm, d), lambda i: (i, 3)), pl.BlockSpec((1, d), const)]
        u_in = [rf, rb, p, out_g.reshape(1, d)]
    else:
        u_specs = [pl.BlockSpec((tm, d), row)]
        u_in = list(u_args)
    in_specs = u_specs + [pl.BlockSpec((d, d), const),
                          pl.BlockSpec((tm, d), row),
                          pl.BlockSpec((1, d), const),
                          _mod_spec(d, t // tm, b * t // tm, b),
                          pl.BlockSpec((d, LANES), const),
                          pl.BlockSpec((d, LANES), const),
                          pl.BlockSpec((1, LANES), const)]
    return pl.pallas_call(
        functools.partial(_out_kernel, mlstm=mlstm),
        grid=(ntot // tm,),
        in_specs=in_specs,
        out_specs=[pl.BlockSpec((tm, d), row), pl.BlockSpec((tm, d), row), pl.BlockSpec((tm, LANES), row)],
        out_shape=[jax.ShapeDtypeStruct((ntot, d), F32), jax.ShapeDtypeStruct((ntot, d), F32),
                   jax.ShapeDtypeStruct((ntot, LANES), F32)],
        compiler_params=_cparams(("arbitrary",)),
    )(*u_in, w_out, x, g2.reshape(1, d), mods, rw_hi, rw_lo, r_bias)


def _lane_pick(lane, pairs):
    out = jnp.zeros(lane.shape, F32)
    for idx, val in pairs:
        out = jnp.where(lane == idx, val, out)
    return out


def _route_kernel(lg_ref, rt_ref, cnt_ref):
    @pl.when(pl.program_id(0) == 0)
    def _():
        cnt_ref[...] = jnp.zeros(cnt_ref.shape, F32)

    lg = lg_ref[...]
    lane = lax.broadcasted_iota(I32, lg.shape, 1)
    lane_f = lane.astype(F32)
    big = 1e9
    is_g = lane < N_GROUPS
    gl = jnp.where(is_g, lg, NEG)
    gmax = jnp.max(gl, axis=1, keepdims=True)
    gidx = jnp.min(jnp.where(gl == gmax, lane_f, big), axis=1, keepdims=True)
    gsum = jnp.sum(jnp.where(is_g, jnp.exp(gl - gmax), 0.0), axis=1, keepdims=True)
    gp_top = 1.0 / gsum
    e_lane = lane_f - float(N_GROUPS)
    lo = gidx * float(EXPERTS_PER_GROUP)
    in_grp = jnp.where(e_lane >= lo, jnp.where(e_lane < lo + float(EXPERTS_PER_GROUP), 1.0, 0.0), 0.0) > 0.5
    el = jnp.where(in_grp, lg, NEG)
    v0 = jnp.max(el, axis=1, keepdims=True)
    i0 = jnp.min(jnp.where(el == v0, lane_f, big), axis=1, keepdims=True)
    el2 = jnp.where(lane_f == i0, NEG, el)
    v1 = jnp.max(el2, axis=1, keepdims=True)
    i1 = jnp.min(jnp.where(el2 == v1, lane_f, big), axis=1, keepdims=True)
    e1 = jnp.exp(v1 - v0)
    g0 = gp_top / (1.0 + e1)
    g1 = gp_top * e1 / (1.0 + e1)
    eid0 = i0 - float(N_GROUPS)
    eid1 = i1 - float(N_GROUPS)
    rt_ref[...] = _lane_pick(lane, ((0, eid0), (1, eid1), (2, g0), (3, g1)))
    hot = jnp.where(lane_f == eid0, 1.0, 0.0) + jnp.where(lane_f == eid1, 1.0, 0.0)
    cnt_ref[0:1, :] += jnp.sum(hot, axis=0, keepdims=True)


def _route(logits):
    ntot = logits.shape[0]
    return pl.pallas_call(
        _route_kernel,
        grid=(ntot // CHUNK,),
        in_specs=[pl.BlockSpec((CHUNK, LANES), lambda i: (i, 0))],
        out_specs=[pl.BlockSpec((CHUNK, LANES), lambda i: (i, 0)), pl.BlockSpec((8, LANES), lambda i: (0, 0))],
        out_shape=[jax.ShapeDtypeStruct((ntot, LANES), F32), jax.ShapeDtypeStruct((8, LANES), F32)],
        compiler_params=_cparams(("arbitrary",)),
    )(logits)


def _slot_kernel(rt_ref, cnt_ref, o_ref, run_ref):
    @pl.when(pl.program_id(0) == 0)
    def _():
        cnt = cnt_ref[0:1, :].astype(I32)
        padded = (((cnt + (MOE_BLOCK - 1)) // MOE_BLOCK) * MOE_BLOCK).astype(F32)
        r_i = lax.broadcasted_iota(I32, (LANES, LANES), 0)
        c_i = lax.broadcasted_iota(I32, (LANES, LANES), 1)
        col = jnp.sum(jnp.where(r_i == c_i, jnp.broadcast_to(padded, (LANES, LANES)), 0.0), axis=1, keepdims=True)
        run_ref[...] = jnp.sum(jnp.where(r_i < c_i, col, 0.0), axis=0, keepdims=True)

    rt = rt_ref[...]
    lane = lax.broadcasted_iota(I32, rt.shape, 1)
    lane_f = lane.astype(F32)
    hot0 = lane_f == rt[:, 0:1]
    hot1 = lane_f == rt[:, 1:2]
    both = jnp.where(hot0, 1.0, 0.0) + jnp.where(hot1, 1.0, 0.0)
    t_i = lax.broadcasted_iota(I32, (CHUNK, CHUNK), 0)
    s_i = lax.broadcasted_iota(I32, (CHUNK, CHUNK), 1)
    before = jnp.where(s_i < t_i, 1.0, 0.0).astype(BF16)
    base = run_ref[...] + jnp.dot(before, both.astype(BF16), preferred_element_type=F32)
    d0 = jnp.sum(jnp.where(hot0, base, 0.0), axis=1, keepdims=True)
    d1 = jnp.sum(jnp.where(hot1, base, 0.0), axis=1, keepdims=True)
    o_ref[...] = _lane_pick(lane, ((0, d0), (1, d1))).astype(I32)
    run_ref[...] += jnp.sum(both, axis=0, keepdims=True)


def _slots(route, counts):
    ntot = route.shape[0]
    return pl.pallas_call(
        _slot_kernel,
        grid=(ntot // CHUNK,),
        in_specs=[pl.BlockSpec((CHUNK, LANES), lambda i: (i, 0)), pl.BlockSpec((8, LANES), lambda i: (0, 0))],
        out_specs=pl.BlockSpec((CHUNK, LANES), lambda i: (i, 0)),
        out_shape=jax.ShapeDtypeStruct((ntot, LANES), I32),
        scratch_shapes=[pltpu.VMEM((1, LANES), F32)],
        compiler_params=_cparams(("arbitrary",)),
    )(route, counts)


def _dispatch_kernel(dest_ref, h_ref, xs_in_ref, xs_ref, sem):
    del xs_in_ref

    def row_copy(r, k):
        return pltpu.make_async_copy(h_ref.at[pl.ds(r, 1), :], xs_ref.at[pl.ds(dest_ref[0, 2 * r + k], 1), :], sem)

    def start(r, carry):
        row_copy(r, 0).start()
        row_copy(r, 1).start()
        return carry

    def wait(r, carry):
        row_copy(r, 0).wait()
        row_copy(r, 1).wait()
        return carry

    lax.fori_loop(0, CHUNK, start, 0, unroll=DMA_UNROLL)
    lax.fori_loop(0, CHUNK, wait, 0, unroll=DMA_UNROLL)


def _dispatch(dest, h, cap):
    ntot, d = h.shape
    return pl.pallas_call(
        _dispatch_kernel,
        grid=(ntot // CHUNK,),
        in_specs=[pl.BlockSpec((None, 1, 2 * CHUNK), lambda i: (i, 0, 0), memory_space=pltpu.SMEM),
                  pl.BlockSpec((CHUNK, d), lambda i: (i, 0)),
                  pl.BlockSpec(memory_space=pl.ANY)],
        out_specs=pl.BlockSpec(memory_space=pl.ANY),
        out_shape=jax.ShapeDtypeStruct((cap, d), F32),
        scratch_shapes=[pltpu.SemaphoreType.DMA],
        input_output_aliases={2: 0},
        compiler_params=_cparams(("arbitrary",)),
    )(dest, h, jnp.zeros((cap, d), F32))


def _expert_kernel(be_ref, xs_ref, wgu_ref, wdn_ref, ys_ref, wgu_bf, wdn_bf):
    j = pl.program_id(0)
    changed = jnp.logical_or(j == 0, be_ref[j] != be_ref[jnp.maximum(j - 1, 0)])

    @pl.when(changed)
    def _():
        wgu_bf[...] = wgu_ref[...].astype(BF16)
        wdn_bf[...] = wdn_ref[...].astype(BF16)

    gu = jnp.dot(xs_ref[...].astype(BF16), wgu_bf[...], preferred_element_type=F32)
    g = gu[:, :D_EXPERT]
    a = (g * jax.nn.sigmoid(g) * gu[:, D_EXPERT:]).astype(BF16)
    ys_ref[...] = jnp.dot(a, wdn_bf[...], preferred_element_type=F32)


def _experts(blk_e, xs, w_gu, w_dn):
    cap, d = xs.shape
    grid_spec = pltpu.PrefetchScalarGridSpec(
        num_scalar_prefetch=1,
        grid=(cap // MOE_BLOCK,),
        in_specs=[pl.BlockSpec((MOE_BLOCK, d), lambda j, be: (j, 0)),
                  pl.BlockSpec((None, d, 2 * D_EXPERT), lambda j, be: (be[j], 0, 0)),
                  pl.BlockSpec((None, D_EXPERT, d), lambda j, be: (be[j], 0, 0))],
        out_specs=pl.BlockSpec((MOE_BLOCK, d), lambda j, be: (j, 0)),
        scratch_shapes=[pltpu.VMEM((d, 2 * D_EXPERT), BF16), pltpu.VMEM((D_EXPERT, d), BF16)])
    return pl.pallas_call(
        _expert_kernel,
        grid_spec=grid_spec,
        out_shape=jax.ShapeDtypeStruct((cap, d), F32),
        compiler_params=_cparams(("arbitrary",)),
    )(blk_e, xs, w_gu, w_dn)


def _combine_kernel(dest_ref, rt_ref, x_ref, mod_ref, ys_ref, xo_ref, buf, sem):
    def row_copy(r, k):
        return pltpu.make_async_copy(ys_ref.at[pl.ds(dest_ref[0, 2 * r + k], 1), :],
                                     buf.at[pl.ds(k * CHUNK + r, 1), :], sem)

    def start(r, carry):
        row_copy(r, 0).start()
        row_copy(r, 1).start()
        return carry

    def wait(r, carry):
        row_copy(r, 0).wait()
        row_copy(r, 1).wait()
        return carry

    lax.fori_loop(0, CHUNK, start, 0, unroll=DMA_UNROLL)
    lax.fori_loop(0, CHUNK, wait, 0, unroll=DMA_UNROLL)
    rt = rt_ref[...]
    y = rt[:, 2:3] * buf[0:CHUNK, :] + rt[:, 3:4] * buf[CHUNK:2 * CHUNK, :]
    xo_ref[...] = x_ref[...] + mod_ref[5:6, :] * y


def _combine(dest, route, x, mods, ys, *, b, t, n_rows):
    d = x.shape[1]
    ncl = t // CHUNK
    return pl.pallas_call(
        _combine_kernel,
        grid=(n_rows // CHUNK,),
        in_specs=[pl.BlockSpec((None, 1, 2 * CHUNK), lambda i: (i, 0, 0), memory_space=pltpu.SMEM),
                  pl.BlockSpec((CHUNK, LANES), lambda i: (i, 0)),
                  pl.BlockSpec((CHUNK, d), lambda i: (i, 0)),
                  _mod_spec(d, ncl, b * ncl, b),
                  pl.BlockSpec(memory_space=pl.ANY)],
        out_specs=pl.BlockSpec((CHUNK, d), lambda i: (i, 0)),
        out_shape=jax.ShapeDtypeStruct((n_rows, d), F32),
        scratch_shapes=[pltpu.VMEM((2 * CHUNK, d), F32), pltpu.SemaphoreType.DMA],
        compiler_params=_cparams(("arbitrary",)),
    )(dest, route, x, mods, ys)


def _moe(x, h, logits, mods, w_gu, w_dn, *, b, t, n_rows):
    ntot = x.shape[0]
    route, counts = _route(logits)
    slot = _slots(route, counts)
    dest = slot[:, :2].reshape(ntot // CHUNK, 1, 2 * CHUNK)
    n_assign = 2 * ntot
    cap = -(-n_assign // MOE_BLOCK) * MOE_BLOCK + N_EXPERTS * MOE_BLOCK
    cnt = counts[0, :N_EXPERTS].astype(I32)
    pad_end = jnp.cumsum((cnt + MOE_BLOCK - 1) // MOE_BLOCK * MOE_BLOCK)
    blk_start = jnp.arange(cap // MOE_BLOCK, dtype=I32) * MOE_BLOCK
    blk_e = jnp.minimum(jnp.sum((pad_end[None, :] <= blk_start[:, None]).astype(I32), axis=1), N_EXPERTS - 1)
    xs = _dispatch(dest, h, cap)
    ys = _experts(blk_e.astype(I32), xs, w_gu, w_dn)
    return _combine(dest, route, x, mods, ys, b=b, t=t, n_rows=n_rows)


def kernel(x, c, ctx, c_ctx, ada_w, ada_b, norm1_g, norm2_g, m_w_in, m_conv_w, m_conv_b, m_gate_b, m_out_g,
           m_w_out, d_w_in, d_q_g, d_k_g, d_lq1, d_lk1, d_lq2, d_lk2, d_sub_g, d_w_out, r_grp_w, r_grp_b,
           r_exp_w, r_exp_b, e_w_gu, e_w_dn):
    b, t, d = x.shape
    cl = ctx.shape[1]
    depth = ada_w.shape[0]
    assert cl == CHUNK and d == 8 * LANES and b + 1 <= 8
    assert t % Q_TILE == 0 and t % KV_CHUNK == 0 and t % ROW_TILE == 0 and (b * cl) % ROW_TILE == 0
    n_lat = b * t
    ntot = n_lat + b * cl

    xs = jnp.concatenate([x.reshape(n_lat, d), ctx.reshape(b * cl, d)], axis=0)
    cond = jnp.zeros((8, d), F32).at[:b].set(c).at[b].set(c_ctx)
    mods_all = _modulation(cond, ada_w, ada_b)
    cos_t, sin_t = _rope_tables(t)

    n_route = N_GROUPS + N_EXPERTS
    for i in range(depth):
        jm = i // 2
        mods = mods_all[i]
        rw = jnp.zeros((d, LANES), F32).at[:, :N_GROUPS].set(r_grp_w[i]).at[:, N_GROUPS:n_route].set(r_exp_w[i])
        rw_hi = rw.astype(BF16)
        rw_lo = (rw - rw_hi.astype(F32)).astype(BF16)
        r_bias = jnp.zeros((1, LANES), F32).at[0, :N_GROUPS].set(r_grp_b[i]).at[0, N_GROUPS:n_route].set(r_exp_b[i])
        if i % 2 == 0:
            w_in = m_w_in[jm]
            wg = jnp.zeros((d, LANES), F32).at[:, :4 * M_HEADS].set(w_in[:, 4 * d:]).astype(BF16)
            gb = jnp.zeros((1, LANES), F32).at[0, :4 * M_HEADS].set(m_gate_b[jm])
            p, gates = _in_proj(xs, norm1_g[i], mods, w_in[:, :4 * d].astype(BF16), b=b, t=t, tn=1024,
                                gates=(wg, gb))
            qk = _mlstm_conv(p, m_conv_w[jm], m_conv_b[jm], b=b, t=t, d=d)
            g4 = gates[:, :4 * M_HEADS].reshape(ntot, 4, M_HEADS)
            rf, rb = _mlstm_scan(qk, p, g4.transpose(2, 0, 1), g4.transpose(2, 1, 0), b=b, t=t, d=d)
            xs, h, logits = _out_proj((rf, rb, p, m_out_g[jm]), m_w_out[jm].astype(BF16), xs, norm2_g[i], mods,
                                      rw_hi, rw_lo, r_bias, b=b, t=t, mlstm=True)
        else:
            lambda_init = 0.8 - 0.6 * math.exp(-0.3 * i)
            p = _in_proj(xs, norm1_g[i], mods, d_w_in[jm].astype(BF16), b=b, t=t, tn=1024)
            qk = _diff_qkprep(p, d_q_g[jm], d_k_g[jm], cos_t, sin_t, b=b, t=t, d=d)
            lam = (jnp.exp(jnp.sum(d_lq1[jm] * d_lk1[jm])) - jnp.exp(jnp.sum(d_lq2[jm] * d_lk2[jm]))
                   + lambda_init).reshape(1).astype(F32)
            v = p[:, 2 * d:]
            vt_lat = v[:n_lat].reshape(n_lat // KV_CHUNK, KV_CHUNK, d).transpose(0, 2, 1)
            vt_ctx = v[n_lat:].reshape(b, cl, d).transpose(0, 2, 1)
            attn = functools.partial(_diff_attention, qk, vt_lat, vt_ctx, lam, d_sub_g[jm], b=b, t=t, d=d,
                                     lambda_init=lambda_init)
            o = jnp.concatenate([attn(latent=True), attn(latent=False)], axis=0)
            xs, h, logits = _out_proj((o,), d_w_out[jm].astype(BF16), xs, norm2_g[i], mods,
                                      rw_hi, rw_lo, r_bias, b=b, t=t, mlstm=False)
        last = i == depth - 1
        xs = _moe(xs, h, logits, mods, e_w_gu[i], e_w_dn[i], b=b, t=t, n_rows=n_lat if last else ntot)
    return xs.reshape(b, t, d)
```

```python
import functools
import math

import jax
import jax.numpy as jnp
from jax import lax
from jax.experimental import pallas as pl
from jax.experimental.pallas import tpu as pltpu

F32 = jnp.float32
BF16 = jnp.bfloat16
I32 = jnp.int32
HIGHEST = lax.Precision.HIGHEST

EPS = 1e-6
GRID_W = 64
ROPE_BASE = 10000.0
M_HEADS = 4
M_CONV_W = 5
D_HEADS = 8
D_HEAD_DIM = 64
N_GROUPS = 4
EXPERTS_PER_GROUP = 8
N_EXPERTS = 32
D_EXPERT = 512

LANES = 128
CHUNK = 256
ROW_TILE = 512
Q_TILE = 512
KV_CHUNK = 512
SOFTMAX_ROWS = 64
V_EXT = LANES + 16
MOE_BLOCK = 256
DMA_UNROLL = 8
NEG = -1e30
VMEM_LIMIT = 56 * 1024 * 1024


def _cparams(sem):
    return pltpu.CompilerParams(dimension_semantics=sem, vmem_limit_bytes=VMEM_LIMIT)


def _nt_dot(a, b):
    return lax.dot_general(a, b, (((1,), (1,)), ((), ())), preferred_element_type=F32)


def _tn_dot(a, b):
    return lax.dot_general(a, b, (((0,), (0,)), ((), ())), preferred_element_type=F32)


def _mod_spec(d, tiles_per_batch, n_lat_tiles, b):
    return pl.BlockSpec((None, 6, d), lambda i, *_: (jnp.where(i < n_lat_tiles, i // tiles_per_batch, b), 0, 0))


def _mod_kernel(s_ref, w_ref, b_ref, o_ref):
    s = s_ref[...]
    s = s * jax.nn.sigmoid(s)
    o_ref[...] = jnp.dot(s, w_ref[...], precision=HIGHEST, preferred_element_type=F32) + b_ref[...]


def _modulation(cond, ada_w, ada_b):
    depth, d, n = ada_w.shape
    tn = n // 4
    out = pl.pallas_call(
        _mod_kernel,
        grid=(depth, n // tn),
        in_specs=[pl.BlockSpec((8, d), lambda l, j: (0, 0)),
                  pl.BlockSpec((None, d, tn), lambda l, j: (l, 0, j)),
                  pl.BlockSpec((None, 1, tn), lambda l, j: (l, 0, j))],
        out_specs=pl.BlockSpec((None, 8, tn), lambda l, j: (l, 0, j)),
        out_shape=jax.ShapeDtypeStruct((depth, 8, n), F32),
        compiler_params=_cparams(("arbitrary", "arbitrary")),
    )(cond, ada_w, ada_b.reshape(depth, 1, n))
    return out.reshape(depth, 8, 6, d)


def _in_kernel(*refs, has_gates):
    if has_gates:
        x_ref, g_ref, mod_ref, w_ref, wg_ref, gb_ref, o_ref, go_ref, h_ref = refs
    else:
        x_ref, g_ref, mod_ref, w_ref, o_ref, h_ref = refs

    @pl.when(pl.program_id(1) == 0)
    def _():
        x = x_ref[...]
        y = x * lax.rsqrt(jnp.mean(x * x, axis=-1, keepdims=True) + EPS) * g_ref[...]
        hb = (y * (1.0 + mod_ref[1:2, :]) + mod_ref[0:1, :]).astype(BF16)
        h_ref[...] = hb
        if has_gates:
            go_ref[...] = jnp.dot(hb, wg_ref[...], preferred_element_type=F32) + gb_ref[...]

    o_ref[...] = jnp.dot(h_ref[...], w_ref[...], preferred_element_type=F32).astype(o_ref.dtype)


def _in_proj(x, g, mods, w, *, b, t, tn, gates=None):
    ntot, d = x.shape
    n = w.shape[1]
    tm = ROW_TILE
    has_gates = gates is not None
    in_specs = [pl.BlockSpec((tm, d), lambda i, j: (i, 0)),
                pl.BlockSpec((1, d), lambda i, j: (0, 0)),
                _mod_spec(d, t // tm, b * t // tm, b),
                pl.BlockSpec((d, tn), lambda i, j: (0, j))]
    args = [x, g.reshape(1, d), mods, w]
    out_specs = pl.BlockSpec((tm, tn), lambda i, j: (i, j))
    out_shape = jax.ShapeDtypeStruct((ntot, n), BF16)
    if has_gates:
        wg, gb = gates
        in_specs += [pl.BlockSpec((d, LANES), lambda i, j: (0, 0)),
                     pl.BlockSpec((1, LANES), lambda i, j: (0, 0))]
        args += [wg, gb]
        out_specs = [out_specs, pl.BlockSpec((tm, LANES), lambda i, j: (i, 0))]
        out_shape = [out_shape, jax.ShapeDtypeStruct((ntot, LANES), F32)]
    return pl.pallas_call(
        functools.partial(_in_kernel, has_gates=has_gates),
        grid=(ntot // tm, n // tn),
        in_specs=in_specs,
        out_specs=out_specs,
        out_shape=out_shape,
        scratch_shapes=[pltpu.VMEM((tm, d), BF16)],
        compiler_params=_cparams(("arbitrary", "arbitrary")),
    )(*args)


def _conv_kernel(xm_ref, xp_ref, xn_ref, w_ref, b_ref, o_ref, *, ncl, n_lat_chunks, ncol_q, k_scale):
    i = pl.program_id(0)
    j = pl.program_id(1)
    c = i % ncl
    is_lat = i < n_lat_chunks
    left_ok = jnp.logical_and(is_lat, c >= 1)
    right_ok = jnp.logical_and(is_lat, c < ncl - 1)
    xm = xm_ref[...].astype(F32)
    xp = jnp.where(left_ok, xp_ref[...].astype(F32), 0.0)[14:16, :]
    xn = jnp.where(right_ok, xn_ref[...].astype(F32), 0.0)[0:2, :]
    ext = jnp.concatenate([xp, xm, xn], axis=0)
    rows = xm.shape[0]
    w = w_ref[...]
    y = b_ref[...] + w[0:1, :] * ext[0:rows, :]
    for k in range(1, M_CONV_W):
        y = y + w[k:k + 1, :] * ext[k:k + rows, :]
    y = y * jax.nn.sigmoid(y)
    y = y * jnp.where(j >= ncol_q, k_scale, 1.0)
    o_ref[...] = y.astype(o_ref.dtype)


def _mlstm_conv(p, conv_w, conv_b, *, b, t, d):
    ntot = p.shape[0]
    tn = 512
    nrb = ntot // 16
    return pl.pallas_call(
        functools.partial(_conv_kernel, ncl=t // CHUNK, n_lat_chunks=b * t // CHUNK, ncol_q=d // tn,
                          k_scale=(d // M_HEADS) ** -0.5),
        grid=(ntot // CHUNK, 2 * d // tn),
        in_specs=[pl.BlockSpec((CHUNK, tn), lambda i, j: (i, j)),
                  pl.BlockSpec((16, tn), lambda i, j: (jnp.maximum(i * (CHUNK // 16) - 1, 0), j)),
                  pl.BlockSpec((16, tn), lambda i, j: (jnp.minimum((i + 1) * (CHUNK // 16), nrb - 1), j)),
                  pl.BlockSpec((M_CONV_W, tn), lambda i, j: (0, j)),
                  pl.BlockSpec((1, tn), lambda i, j: (0, j))],
        out_specs=pl.BlockSpec((CHUNK, tn), lambda i, j: (i, j)),
        out_shape=jax.ShapeDtypeStruct((ntot, 2 * d), BF16),
        compiler_params=_cparams(("arbitrary", "arbitrary")),
    )(p, p, p, conv_w, conv_b.reshape(1, 2 * d))


def _log_sigmoid(x):
    return jnp.minimum(x, 0.0) - jnp.log(1.0 + jnp.exp(-jnp.abs(x)))


def _mlstm_chunk(q, k, v, gcol, grow, igate, fgate, mask, mask_t, ct_ref, n_ref, m_ref, o_ref):
    i_row = grow[igate:igate + 1, :]
    i_col = gcol[:, igate:igate + 1]
    f_row = _log_sigmoid(grow[fgate:fgate + 1, :])
    f_col = _log_sigmoid(gcol[:, fgate:fgate + 1])
    cum_col = jnp.sum(jnp.where(mask, f_row, 0.0), axis=1, keepdims=True)
    cum_row = jnp.sum(jnp.where(mask_t, f_col, 0.0), axis=0, keepdims=True)
    tot = jnp.sum(f_col, axis=0, keepdims=True)
    m_prev = m_ref[...]
    dmat = jnp.where(mask, cum_col - cum_row + i_row, NEG)
    m_inter = cum_col + m_prev
    m_out = jnp.maximum(m_inter, jnp.max(dmat, axis=1, keepdims=True))
    w_intra = jnp.exp(dmat - m_out)
    w_inter = jnp.exp(m_inter - m_out)
    sc = _nt_dot(q, k) * w_intra
    ct = ct_ref[...]
    n_row = n_ref[...]
    qf = q.astype(F32)
    kf = k.astype(F32)
    vf = v.astype(F32)
    num = (w_inter * jnp.dot(q, ct.astype(BF16), preferred_element_type=F32)
           + jnp.dot(sc.astype(BF16), v, preferred_element_type=F32))
    den = w_inter * jnp.sum(qf * n_row, axis=1, keepdims=True) + jnp.sum(sc, axis=1, keepdims=True)
    o_ref[...] = (num / jnp.maximum(jnp.abs(den), jnp.exp(-m_out))).astype(o_ref.dtype)
    g_col = tot - cum_col + i_col
    m_next = jnp.maximum(tot + m_prev, jnp.max(g_col, axis=0, keepdims=True))
    decay = jnp.exp(tot + m_prev - m_next)
    wg = jnp.exp(g_col - m_next)
    ct_ref[...] = decay * ct + _tn_dot(k, (wg * vf).astype(BF16))
    n_ref[...] = decay * n_row + jnp.sum(wg * kf, axis=0, keepdims=True)
    m_ref[...] = m_next


def _scan_kernel(qf_ref, kf_ref, vf_ref, gcf_ref, grf_ref, qb_ref, kb_ref, vb_ref, gcb_ref, grb_ref,
                 of_ref, ob_ref, ctf_ref, nf_ref, mf_ref, ctb_ref, nb_ref, mb_ref):
    @pl.when(pl.program_id(2) == 0)
    def _():
        for r in (ctf_ref, nf_ref, mf_ref, ctb_ref, nb_ref, mb_ref):
            r[...] = jnp.zeros(r.shape, r.dtype)

    t_i = lax.broadcasted_iota(I32, (CHUNK, CHUNK), 0)
    s_i = lax.broadcasted_iota(I32, (CHUNK, CHUNK), 1)
    lower = s_i <= t_i
    upper = s_i >= t_i
    _mlstm_chunk(qf_ref[...], kf_ref[...], vf_ref[...], gcf_ref[...], grf_ref[...], 0, 2,
                 lower, upper, ctf_ref, nf_ref, mf_ref, of_ref)
    _mlstm_chunk(qb_ref[...], kb_ref[...], vb_ref[...], gcb_ref[...], grb_ref[...], 1, 3,
                 upper, lower, ctb_ref, nb_ref, mb_ref, ob_ref)


def _mlstm_scan(qk, p, gcol, grow, *, b, t, d):
    ntot = qk.shape[0]
    dh = d // M_HEADS
    ncl = t // CHUNK
    ctx0 = b * ncl

    def fwd(bi, s):
        return jnp.where(s == 0, ctx0 + bi, bi * ncl + s - 1)

    def bwd(bi, s):
        return jnp.where(s == 0, ctx0 + bi, bi * ncl + ncl - s)

    def specs(order):
        return [pl.BlockSpec((CHUNK, dh), lambda bi, h, s: (order(bi, s), h)),
                pl.BlockSpec((CHUNK, dh), lambda bi, h, s: (order(bi, s), M_HEADS + h)),
                pl.BlockSpec((CHUNK, dh), lambda bi, h, s: (order(bi, s), 2 * M_HEADS + h)),
                pl.BlockSpec((None, CHUNK, 4), lambda bi, h, s: (h, order(bi, s), 0)),
                pl.BlockSpec((None, 4, CHUNK), lambda bi, h, s: (h, 0, order(bi, s)))]

    def ospec(order):
        return pl.BlockSpec((CHUNK, dh), lambda bi, h, s: (order(bi, s), h))

    state = [pltpu.VMEM((dh, dh), F32), pltpu.VMEM((1, dh), F32), pltpu.VMEM((1, 1), F32)]
    return pl.pallas_call(
        _scan_kernel,
        grid=(b, M_HEADS, ncl + 1),
        in_specs=specs(fwd) + specs(bwd),
        out_specs=[ospec(fwd), ospec(bwd)],
        out_shape=[jax.ShapeDtypeStruct((ntot, d), F32)] * 2,
        scratch_shapes=state + state,
        compiler_params=_cparams(("arbitrary", "arbitrary", "arbitrary")),
    )(qk, qk, p, gcol, grow, qk, qk, p, gcol, grow)


def _qkprep_kernel(p_ref, gq_ref, gk_ref, cos_ref, sin_ref, o_ref, *, n_lat_chunks, nblk_q, q_scale):
    is_ctx = pl.program_id(0) >= n_lat_chunks
    cos = jnp.where(is_ctx, 1.0, cos_ref[...])
    sin = jnp.where(is_ctx, 0.0, sin_ref[...])
    lane = lax.broadcasted_iota(I32, (CHUNK, LANES), 1)
    first_half = (lane % 32) < 16
    r_i = lax.broadcasted_iota(I32, (LANES, LANES), 0)
    c_i = lax.broadcasted_iota(I32, (LANES, LANES), 1)
    group = jnp.where((r_i // D_HEAD_DIM) == (c_i // D_HEAD_DIM), 1.0, 0.0).astype(BF16)
    for blk in range(2 * nblk_q):
        x = p_ref[:, blk * LANES:(blk + 1) * LANES].astype(F32)
        ss = x * x
        ss_hi = ss.astype(BF16)
        ss_lo = (ss - ss_hi.astype(F32)).astype(BF16)
        ssum = (jnp.dot(ss_hi, group, preferred_element_type=F32)
                + jnp.dot(ss_lo, group, preferred_element_type=F32))
        g = gq_ref[...] if blk < nblk_q else gk_ref[...]
        xn = x * lax.rsqrt(ssum * (1.0 / D_HEAD_DIM) + EPS) * g
        partner = jnp.where(first_half, pltpu.roll(xn, LANES - 16, 1), pltpu.roll(xn, 16, 1))
        y = xn * cos + partner * sin
        if blk < nblk_q:
            y = y * q_scale
        o_ref[:, blk * LANES:(blk + 1) * LANES] = y.astype(o_ref.dtype)


def _diff_qkprep(p, q_g, k_g, cos_t, sin_t, *, b, t, d):
    ntot = p.shape[0]
    ncl = t // CHUNK
    g2 = lambda g: jnp.tile(g.astype(F32), 2).reshape(1, LANES)
    return pl.pallas_call(
        functools.partial(_qkprep_kernel, n_lat_chunks=b * ncl, nblk_q=d // LANES,
                          q_scale=D_HEAD_DIM ** -0.5 * math.log2(math.e)),
        grid=(ntot // CHUNK,),
        in_specs=[pl.BlockSpec((CHUNK, 2 * d), lambda i: (i, 0)),
                  pl.BlockSpec((1, LANES), lambda i: (0, 0)),
                  pl.BlockSpec((1, LANES), lambda i: (0, 0)),
                  pl.BlockSpec((CHUNK, LANES), lambda i: (i % ncl, 0)),
                  pl.BlockSpec((CHUNK, LANES), lambda i: (i % ncl, 0))],
        out_specs=pl.BlockSpec((CHUNK, 2 * d), lambda i: (i, 0)),
        out_shape=jax.ShapeDtypeStruct((ntot, 2 * d), BF16),
        compiler_params=_cparams(("arbitrary",)),
    )(p, g2(q_g), g2(k_g), cos_t, sin_t)


def _rope_tables(t):
    rows = t // GRID_W
    row = jnp.repeat(jnp.arange(rows), GRID_W).astype(F32)[:, None]
    col = jnp.tile(jnp.arange(GRID_W), rows).astype(F32)[:, None]
    nf = D_HEAD_DIM // 4
    inv = ROPE_BASE ** (-jnp.arange(nf, dtype=F32) / nf)
    ar, ac = row * inv, col * inv
    cos64 = jnp.concatenate([jnp.cos(ar), jnp.cos(ar), jnp.cos(ac), jnp.cos(ac)], axis=-1)
    sin64 = jnp.concatenate([-jnp.sin(ar), jnp.sin(ar), -jnp.sin(ac), jnp.sin(ac)], axis=-1)
    return jnp.tile(cos64, (1, 2)), jnp.tile(sin64, (1, 2))


def _attn_kernel(*refs, n_lat, out_scale):
    if n_lat:
        lam_ref, q_ref, kc_ref, vtc_ref, kl_ref, vtl_ref, sg_ref, o_ref, m1, a1, m2, a2 = refs[:12]
        bufs = refs[12:]
    else:
        lam_ref, q_ref, kc_ref, vtc_ref, sg_ref, o_ref, m1, a1, m2, a2 = refs[:10]
        bufs = refs[10:]
    sets = [tuple(bufs[i + 2 * k:i + 2 * k + 2] for k in range(4)) for i in range(0, len(bufs), 8)]
    q = q_ref[...]
    lane = lax.broadcasted_iota(I32, q.shape, 1)
    zero = jnp.zeros_like(q)
    qs = (jnp.where(lane < D_HEAD_DIM, q, zero), jnp.where(lane >= D_HEAD_DIM, q, zero))
    maxes = (m1, m2)
    accs = (a1, a2)
    for m_r, a_r in zip(maxes, accs):
        m_r[...] = jnp.full(m_r.shape, NEG, F32)
        a_r[...] = jnp.zeros(a_r.shape, F32)

    def scores(keys, s_bufs, mx_bufs):
        for qm, s_ref, mx_ref in zip(qs, s_bufs, mx_bufs):
            st = _nt_dot(keys, qm)
            s_ref[...] = st
            mx = st[0:8]
            for r in range(8, st.shape[0], 8):
                mx = jnp.maximum(mx, st[r:r + 8])
            mx_ref[...] = mx

    def softmax(s_bufs, mx_bufs, p_bufs, al_bufs):
        for s_ref, mx_ref, p_ref, al_ref, m_r in zip(s_bufs, mx_bufs, p_bufs, al_bufs, maxes):
            m_old = m_r[...]
            m_new = jnp.maximum(m_old, jnp.max(mx_ref[...], axis=0, keepdims=True))
            m_r[...] = m_new
            al_ref[...] = jnp.exp2(m_old - m_new)
            for r0 in range(0, s_ref.shape[0], SOFTMAX_ROWS):
                p = jnp.exp2(s_ref[r0:r0 + SOFTMAX_ROWS, :] - m_new)
                p_ref[r0:r0 + SOFTMAX_ROWS, :] = p.astype(BF16)

    def weigh(vt, p_bufs, al_bufs):
        for p_ref, al_ref, a_r in zip(p_bufs, al_bufs, accs):
            a_r[...] = al_ref[...] * a_r[...] + jnp.dot(vt, p_ref[...], preferred_element_type=F32)

    s_c, mx_c, p_c, al_c = sets[0]
    scores(kc_ref[...], s_c, mx_c)
    softmax(s_c, mx_c, p_c, al_c)
    weigh(vtc_ref[...], p_c, al_c)

    if n_lat:
        def keys(j):
            return kl_ref[pl.ds(pl.multiple_of(j * KV_CHUNK, KV_CHUNK), KV_CHUNK), :]

        groups = (sets[1:3], sets[3:5])
        n_pairs = n_lat // 2

        def qk(g, group):
            for k, (s_x, mx_x, _, _) in enumerate(group):
                scores(keys(2 * g + k), s_x, mx_x)

        def sm(group):
            for s_x, mx_x, p_x, al_x in group:
                softmax(s_x, mx_x, p_x, al_x)

        def pv(g, group):
            for k, (_, _, p_x, al_x) in enumerate(group):
                weigh(vtl_ref[2 * g + k], p_x, al_x)

        qk(0, groups[0])
        sm(groups[0])
        qk(1, groups[1])

        def body(i, carry):
            g = 2 * i + 2
            sm(groups[1])
            qk(g, groups[0])
            pv(g - 2, groups[0])
            sm(groups[0])
            qk(g + 1, groups[1])
            pv(g - 1, groups[1])
            return carry

        lax.fori_loop(0, (n_pairs - 2) // 2, body, 0)
        sm(groups[1])
        pv(n_pairs - 2, groups[0])
        pv(n_pairs - 1, groups[1])
    o = (a1[0:LANES, :] / a1[LANES:LANES + 1, :]
         - lam_ref[0] * (a2[0:LANES, :] / a2[LANES:LANES + 1, :]))
    o = o * lax.rsqrt(jnp.mean(o * o, axis=0, keepdims=True) + EPS) * sg_ref[...] * out_scale
    o_ref[...] = o.T.astype(o_ref.dtype)


def _values_t(v):
    n, tk, d = v.shape
    nh = d // LANES
    vt = v.transpose(0, 2, 1).reshape(n, nh, LANES, tk)
    ones = jnp.ones((n, nh, 1, tk), v.dtype)
    pad = jnp.zeros((n, nh, V_EXT - LANES - 1, tk), v.dtype)
    return jnp.concatenate([vt, ones, pad], axis=2).reshape(n, nh * V_EXT, tk)


def _diff_attention(qk, vt_lat, vt_ctx, lam, sub_g, *, b, t, d, lambda_init, latent):
    nh = d // LANES
    ctx0 = b * t // CHUNK
    n_lat = t // KV_CHUNK if latent else 0
    tq = Q_TILE if latent else CHUNK
    nq = t // tq if latent else 1
    q_row = (lambda bi, c: bi * nq + c) if latent else (lambda bi, c: ctx0 + bi)
    in_specs = [pl.BlockSpec((tq, LANES), lambda bi, h, c, lam_r: (q_row(bi, c), h)),
                pl.BlockSpec((CHUNK, LANES), lambda bi, h, c, lam_r: (ctx0 + bi, nh + h)),
                pl.BlockSpec((None, V_EXT, CHUNK), lambda bi, h, c, lam_r: (bi, h, 0))]
    args = [qk, qk, vt_ctx]
    if latent:
        in_specs += [pl.BlockSpec((t, LANES), lambda bi, h, c, lam_r: (bi, nh + h)),
                     pl.BlockSpec((n_lat, V_EXT, KV_CHUNK), lambda bi, h, c, lam_r: (bi, h, 0))]
        args += [qk, vt_lat]
    in_specs.append(pl.BlockSpec((LANES, 1), lambda bi, h, c, lam_r: (0, 0)))
    args.append(sub_g.reshape(LANES, 1).astype(F32))
    stat = pltpu.VMEM((1, tq), F32)
    acc = pltpu.VMEM((V_EXT, tq), F32)

    def buffer_set(tk):
        return ([pltpu.VMEM((tk, tq), F32)] * 2 + [pltpu.VMEM((8, tq), F32)] * 2
                + [pltpu.VMEM((tk, tq), BF16)] * 2 + [stat] * 2)

    grid_spec = pltpu.PrefetchScalarGridSpec(
        num_scalar_prefetch=1,
        grid=(b, nh, nq),
        in_specs=in_specs,
        out_specs=pl.BlockSpec((tq, LANES), lambda bi, h, c, lam_r: (bi * nq + c, h)),
        scratch_shapes=[stat, acc, stat, acc] + buffer_set(CHUNK) + (buffer_set(KV_CHUNK) * 4 if latent else []))
    return pl.pallas_call(
        functools.partial(_attn_kernel, n_lat=n_lat, out_scale=1.0 - lambda_init),
        grid_spec=grid_spec,
        out_shape=jax.ShapeDtypeStruct((b * nq * tq, d), BF16),
        compiler_params=_cparams(("arbitrary", "arbitrary", "arbitrary")),
    )(lam, *args)


def _out_kernel(*refs, mlstm):
    if mlstm:
        (rf_ref, rb_ref, op_ref, og_ref, w_ref, x_ref, g2_ref, mod_ref, rwh_ref, rwl_ref, rb2_ref,
         xo_ref, h_ref, lg_ref) = refs
        r = rf_ref[...] + rb_ref[...]
        dh = r.shape[1] // M_HEADS
        parts = []
        for hh in range(M_HEADS):
            rr = r[:, hh * dh:(hh + 1) * dh]
            parts.append(rr * lax.rsqrt(jnp.mean(rr * rr, axis=-1, keepdims=True) + EPS))
        rn = jnp.concatenate(parts, axis=1) * og_ref[...]
        u = (jax.nn.sigmoid(op_ref[...].astype(F32)) * rn).astype(BF16)
    else:
        (u_ref, w_ref, x_ref, g2_ref, mod_ref, rwh_ref, rwl_ref, rb2_ref, xo_ref, h_ref, lg_ref) = refs
        u = u_ref[...]
    y = jnp.dot(u, w_ref[...], preferred_element_type=F32)
    x = x_ref[...] + mod_ref[2:3, :] * y
    xo_ref[...] = x
    hn = x * lax.rsqrt(jnp.mean(x * x, axis=-1, keepdims=True) + EPS) * g2_ref[...]
    h = hn * (1.0 + mod_ref[4:5, :]) + mod_ref[3:4, :]
    h_ref[...] = h
    h_hi = h.astype(BF16)
    h_lo = (h - h_hi.astype(F32)).astype(BF16)
    lg_ref[...] = (jnp.dot(h_hi, rwh_ref[...], preferred_element_type=F32)
                   + jnp.dot(h_lo, rwh_ref[...], preferred_element_type=F32)
                   + jnp.dot(h_hi, rwl_ref[...], preferred_element_type=F32) + rb2_ref[...])


def _out_proj(u_args, w_out, x, g2, mods, rw_hi, rw_lo, r_bias, *, b, t, mlstm):
    ntot, d = x.shape
    tm = ROW_TILE
    row = lambda i: (i, 0)
    const = lambda i: (0, 0)
    if mlstm:
        rf, rb, p, out_g = u_args
        u_specs = [pl.BlockSpec((tm, d), row), pl.BlockSpec((tm, d), row),
                   pl.BlockSpec((tm, d), lambda i: (i, 3)), pl.BlockSpec((1, d), const)]
        u_in = [rf, rb, p, out_g.reshape(1, d)]
    else:
        u_specs = [pl.BlockSpec((tm, d), row)]
        u_in = list(u_args)
    in_specs = u_specs + [pl.BlockSpec((d, d), const),
                          pl.BlockSpec((tm, d), row),
                          pl.BlockSpec((1, d), const),
                          _mod_spec(d, t // tm, b * t // tm, b),
                          pl.BlockSpec((d, LANES), const),
                          pl.BlockSpec((d, LANES), const),
                          pl.BlockSpec((1, LANES), const)]
    return pl.pallas_call(
        functools.partial(_out_kernel, mlstm=mlstm),
        grid=(ntot // tm,),
        in_specs=in_specs,
        out_specs=[pl.BlockSpec((tm, d), row), pl.BlockSpec((tm, d), row), pl.BlockSpec((tm, LANES), row)],
        out_shape=[jax.ShapeDtypeStruct((ntot, d), F32), jax.ShapeDtypeStruct((ntot, d), F32),
                   jax.ShapeDtypeStruct((ntot, LANES), F32)],
        compiler_params=_cparams(("arbitrary",)),
    )(*u_in, w_out, x, g2.reshape(1, d), mods, rw_hi, rw_lo, r_bias)


def _lane_pick(lane, pairs):
    out = jnp.zeros(lane.shape, F32)
    for idx, val in pairs:
        out = jnp.where(lane == idx, val, out)
    return out


def _route_kernel(lg_ref, rt_ref, cnt_ref):
    @pl.when(pl.program_id(0) == 0)
    def _():
        cnt_ref[...] = jnp.zeros(cnt_ref.shape, F32)

    lg = lg_ref[...]
    lane = lax.broadcasted_iota(I32, lg.shape, 1)
    lane_f = lane.astype(F32)
    big = 1e9
    is_g = lane < N_GROUPS
    gl = jnp.where(is_g, lg, NEG)
    gmax = jnp.max(gl, axis=1, keepdims=True)
    gidx = jnp.min(jnp.where(gl == gmax, lane_f, big), axis=1, keepdims=True)
    gsum = jnp.sum(jnp.where(is_g, jnp.exp(gl - gmax), 0.0), axis=1, keepdims=True)
    gp_top = 1.0 / gsum
    e_lane = lane_f - float(N_GROUPS)
    lo = gidx * float(EXPERTS_PER_GROUP)
    in_grp = jnp.where(e_lane >= lo, jnp.where(e_lane < lo + float(EXPERTS_PER_GROUP), 1.0, 0.0), 0.0) > 0.5
    el = jnp.where(in_grp, lg, NEG)
    v0 = jnp.max(el, axis=1, keepdims=True)
    i0 = jnp.min(jnp.where(el == v0, lane_f, big), axis=1, keepdims=True)
    el2 = jnp.where(lane_f == i0, NEG, el)
    v1 = jnp.max(el2, axis=1, keepdims=True)
    i1 = jnp.min(jnp.where(el2 == v1, lane_f, big), axis=1, keepdims=True)
    e1 = jnp.exp(v1 - v0)
    g0 = gp_top / (1.0 + e1)
    g1 = gp_top * e1 / (1.0 + e1)
    eid0 = i0 - float(N_GROUPS)
    eid1 = i1 - float(N_GROUPS)
    rt_ref[...] = _lane_pick(lane, ((0, eid0), (1, eid1), (2, g0), (3, g1)))
    hot = jnp.where(lane_f == eid0, 1.0, 0.0) + jnp.where(lane_f == eid1, 1.0, 0.0)
    cnt_ref[0:1, :] += jnp.sum(hot, axis=0, keepdims=True)


def _route(logits):
    ntot = logits.shape[0]
    return pl.pallas_call(
        _route_kernel,
        grid=(ntot // CHUNK,),
        in_specs=[pl.BlockSpec((CHUNK, LANES), lambda i: (i, 0))],
        out_specs=[pl.BlockSpec((CHUNK, LANES), lambda i: (i, 0)), pl.BlockSpec((8, LANES), lambda i: (0, 0))],
        out_shape=[jax.ShapeDtypeStruct((ntot, LANES), F32), jax.ShapeDtypeStruct((8, LANES), F32)],
        compiler_params=_cparams(("arbitrary",)),
    )(logits)


def _slot_kernel(rt_ref, cnt_ref, o_ref, run_ref):
    @pl.when(pl.program_id(0) == 0)
    def _():
        cnt = cnt_ref[0:1, :].astype(I32)
        padded = (((cnt + (MOE_BLOCK - 1)) // MOE_BLOCK) * MOE_BLOCK).astype(F32)
        r_i = lax.broadcasted_iota(I32, (LANES, LANES), 0)
        c_i = lax.broadcasted_iota(I32, (LANES, LANES), 1)
        col = jnp.sum(jnp.where(r_i == c_i, jnp.broadcast_to(padded, (LANES, LANES)), 0.0), axis=1, keepdims=True)
        run_ref[...] = jnp.sum(jnp.where(r_i < c_i, col, 0.0), axis=0, keepdims=True)

    rt = rt_ref[...]
    lane = lax.broadcasted_iota(I32, rt.shape, 1)
    lane_f = lane.astype(F32)
    hot0 = lane_f == rt[:, 0:1]
    hot1 = lane_f == rt[:, 1:2]
    both = jnp.where(hot0, 1.0, 0.0) + jnp.where(hot1, 1.0, 0.0)
    t_i = lax.broadcasted_iota(I32, (CHUNK, CHUNK), 0)
    s_i = lax.broadcasted_iota(I32, (CHUNK, CHUNK), 1)
    before = jnp.where(s_i < t_i, 1.0, 0.0).astype(BF16)
    base = run_ref[...] + jnp.dot(before, both.astype(BF16), preferred_element_type=F32)
    d0 = jnp.sum(jnp.where(hot0, base, 0.0), axis=1, keepdims=True)
    d1 = jnp.sum(jnp.where(hot1, base, 0.0), axis=1, keepdims=True)
    o_ref[...] = _lane_pick(lane, ((0, d0), (1, d1))).astype(I32)
    run_ref[...] += jnp.sum(both, axis=0, keepdims=True)


def _slots(route, counts):
    ntot = route.shape[0]
    return pl.pallas_call(
        _slot_kernel,
        grid=(ntot // CHUNK,),
        in_specs=[pl.BlockSpec((CHUNK, LANES), lambda i: (i, 0)), pl.BlockSpec((8, LANES), lambda i: (0, 0))],
        out_specs=pl.BlockSpec((CHUNK, LANES), lambda i: (i, 0)),
        out_shape=jax.ShapeDtypeStruct((ntot, LANES), I32),
        scratch_shapes=[pltpu.VMEM((1, LANES), F32)],
        compiler_params=_cparams(("arbitrary",)),
    )(route, counts)


def _dispatch_kernel(dest_ref, h_ref, xs_in_ref, xs_ref, sem):
    del xs_in_ref

    def row_copy(r, k):
        return pltpu.make_async_copy(h_ref.at[pl.ds(r, 1), :], xs_ref.at[pl.ds(dest_ref[0, 2 * r + k], 1), :], sem)

    def start(r, carry):
        row_copy(r, 0).start()
        row_copy(r, 1).start()
        return carry

    def wait(r, carry):
        row_copy(r, 0).wait()
        row_copy(r, 1).wait()
        return carry

    lax.fori_loop(0, CHUNK, start, 0, unroll=DMA_UNROLL)
    lax.fori_loop(0, CHUNK, wait, 0, unroll=DMA_UNROLL)


def _dispatch(dest, h, cap):
    ntot, d = h.shape
    return pl.pallas_call(
        _dispatch_kernel,
        grid=(ntot // CHUNK,),
        in_specs=[pl.BlockSpec((None, 1, 2 * CHUNK), lambda i: (i, 0, 0), memory_space=pltpu.SMEM),
                  pl.BlockSpec((CHUNK, d), lambda i: (i, 0)),
                  pl.BlockSpec(memory_space=pl.ANY)],
        out_specs=pl.BlockSpec(memory_space=pl.ANY),
        out_shape=jax.ShapeDtypeStruct((cap, d), F32),
        scratch_shapes=[pltpu.SemaphoreType.DMA],
        input_output_aliases={2: 0},
        compiler_params=_cparams(("arbitrary",)),
    )(dest, h, jnp.zeros((cap, d), F32))


def _expert_kernel(be_ref, xs_ref, wgu_ref, wdn_ref, ys_ref, wgu_bf, wdn_bf):
    j = pl.program_id(0)
    changed = jnp.logical_or(j == 0, be_ref[j] != be_ref[jnp.maximum(j - 1, 0)])

    @pl.when(changed)
    def _():
        wgu_bf[...] = wgu_ref[...].astype(BF16)
        wdn_bf[...] = wdn_ref[...].astype(BF16)

    gu = jnp.dot(xs_ref[...].astype(BF16), wgu_bf[...], preferred_element_type=F32)
    g = gu[:, :D_EXPERT]
    a = (g * jax.nn.sigmoid(g) * gu[:, D_EXPERT:]).astype(BF16)
    ys_ref[...] = jnp.dot(a, wdn_bf[...], preferred_element_type=F32)


def _experts(blk_e, xs, w_gu, w_dn):
    cap, d = xs.shape
    grid_spec = pltpu.PrefetchScalarGridSpec(
        num_scalar_prefetch=1,
        grid=(cap // MOE_BLOCK,),
        in_specs=[pl.BlockSpec((MOE_BLOCK, d), lambda j, be: (j, 0)),
                  pl.BlockSpec((None, d, 2 * D_EXPERT), lambda j, be: (be[j], 0, 0)),
                  pl.BlockSpec((None, D_EXPERT, d), lambda j, be: (be[j], 0, 0))],
        out_specs=pl.BlockSpec((MOE_BLOCK, d), lambda j, be: (j, 0)),
        scratch_shapes=[pltpu.VMEM((d, 2 * D_EXPERT), BF16), pltpu.VMEM((D_EXPERT, d), BF16)])
    return pl.pallas_call(
        _expert_kernel,
        grid_spec=grid_spec,
        out_shape=jax.ShapeDtypeStruct((cap, d), F32),
        compiler_params=_cparams(("arbitrary",)),
    )(blk_e, xs, w_gu, w_dn)


def _combine_kernel(dest_ref, rt_ref, x_ref, mod_ref, ys_ref, xo_ref, buf, sem):
    def row_copy(r, k):
        return pltpu.make_async_copy(ys_ref.at[pl.ds(dest_ref[0, 2 * r + k], 1), :],
                                     buf.at[pl.ds(k * CHUNK + r, 1), :], sem)

    def start(r, carry):
        row_copy(r, 0).start()
        row_copy(r, 1).start()
        return carry

    def wait(r, carry):
        row_copy(r, 0).wait()
        row_copy(r, 1).wait()
        return carry

    lax.fori_loop(0, CHUNK, start, 0, unroll=DMA_UNROLL)
    lax.fori_loop(0, CHUNK, wait, 0, unroll=DMA_UNROLL)
    rt = rt_ref[...]
    y = rt[:, 2:3] * buf[0:CHUNK, :] + rt[:, 3:4] * buf[CHUNK:2 * CHUNK, :]
    xo_ref[...] = x_ref[...] + mod_ref[5:6, :] * y


def _combine(dest, route, x, mods, ys, *, b, t, n_rows):
    d = x.shape[1]
    ncl = t // CHUNK
    return pl.pallas_call(
        _combine_kernel,
        grid=(n_rows // CHUNK,),
        in_specs=[pl.BlockSpec((None, 1, 2 * CHUNK), lambda i: (i, 0, 0), memory_space=pltpu.SMEM),
                  pl.BlockSpec((CHUNK, LANES), lambda i: (i, 0)),
                  pl.BlockSpec((CHUNK, d), lambda i: (i, 0)),
                  _mod_spec(d, ncl, b * ncl, b),
                  pl.BlockSpec(memory_space=pl.ANY)],
        out_specs=pl.BlockSpec((CHUNK, d), lambda i: (i, 0)),
        out_shape=jax.ShapeDtypeStruct((n_rows, d), F32),
        scratch_shapes=[pltpu.VMEM((2 * CHUNK, d), F32), pltpu.SemaphoreType.DMA],
        compiler_params=_cparams(("arbitrary",)),
    )(dest, route, x, mods, ys)


def _moe(x, h, logits, mods, w_gu, w_dn, *, b, t, n_rows):
    ntot = x.shape[0]
    route, counts = _route(logits)
    slot = _slots(route, counts)
    dest = slot[:, :2].reshape(ntot // CHUNK, 1, 2 * CHUNK)
    n_assign = 2 * ntot
    cap = -(-n_assign // MOE_BLOCK) * MOE_BLOCK + N_EXPERTS * MOE_BLOCK
    cnt = counts[0, :N_EXPERTS].astype(I32)
    pad_end = jnp.cumsum((cnt + MOE_BLOCK - 1) // MOE_BLOCK * MOE_BLOCK)
    blk_start = jnp.arange(cap // MOE_BLOCK, dtype=I32) * MOE_BLOCK
    blk_e = jnp.minimum(jnp.sum((pad_end[None, :] <= blk_start[:, None]).astype(I32), axis=1), N_EXPERTS - 1)
    xs = _dispatch(dest, h, cap)
    ys = _experts(blk_e.astype(I32), xs, w_gu, w_dn)
    return _combine(dest, route, x, mods, ys, b=b, t=t, n_rows=n_rows)


def kernel(x, c, ctx, c_ctx, ada_w, ada_b, norm1_g, norm2_g, m_w_in, m_conv_w, m_conv_b, m_gate_b, m_out_g,
           m_w_out, d_w_in, d_q_g, d_k_g, d_lq1, d_lk1, d_lq2, d_lk2, d_sub_g, d_w_out, r_grp_w, r_grp_b,
           r_exp_w, r_exp_b, e_w_gu, e_w_dn):
    b, t, d = x.shape
    cl = ctx.shape[1]
    depth = ada_w.shape[0]
    assert cl == CHUNK and d == 8 * LANES and b + 1 <= 8
    assert t % Q_TILE == 0 and t % (4 * KV_CHUNK) == 0 and t % ROW_TILE == 0 and (b * cl) % ROW_TILE == 0
    n_lat = b * t
    ntot = n_lat + b * cl

    xs = jnp.concatenate([x.reshape(n_lat, d), ctx.reshape(b * cl, d)], axis=0)
    cond = jnp.zeros((8, d), F32).at[:b].set(c).at[b].set(c_ctx)
    mods_all = _modulation(cond, ada_w, ada_b)
    cos_t, sin_t = _rope_tables(t)

    n_route = N_GROUPS + N_EXPERTS
    for i in range(depth):
        jm = i // 2
        mods = mods_all[i]
        rw = jnp.zeros((d, LANES), F32).at[:, :N_GROUPS].set(r_grp_w[i]).at[:, N_GROUPS:n_route].set(r_exp_w[i])
        rw_hi = rw.astype(BF16)
        rw_lo = (rw - rw_hi.astype(F32)).astype(BF16)
        r_bias = jnp.zeros((1, LANES), F32).at[0, :N_GROUPS].set(r_grp_b[i]).at[0, N_GROUPS:n_route].set(r_exp_b[i])
        if i % 2 == 0:
            w_in = m_w_in[jm]
            wg = jnp.zeros((d, LANES), F32).at[:, :4 * M_HEADS].set(w_in[:, 4 * d:]).astype(BF16)
            gb = jnp.zeros((1, LANES), F32).at[0, :4 * M_HEADS].set(m_gate_b[jm])
            p, gates = _in_proj(xs, norm1_g[i], mods, w_in[:, :4 * d].astype(BF16), b=b, t=t, tn=1024,
                                gates=(wg, gb))
            qk = _mlstm_conv(p, m_conv_w[jm], m_conv_b[jm], b=b, t=t, d=d)
            g4 = gates[:, :4 * M_HEADS].reshape(ntot, 4, M_HEADS)
            rf, rb = _mlstm_scan(qk, p, g4.transpose(2, 0, 1), g4.transpose(2, 1, 0), b=b, t=t, d=d)
            xs, h, logits = _out_proj((rf, rb, p, m_out_g[jm]), m_w_out[jm].astype(BF16), xs, norm2_g[i], mods,
                                      rw_hi, rw_lo, r_bias, b=b, t=t, mlstm=True)
        else:
            lambda_init = 0.8 - 0.6 * math.exp(-0.3 * i)
            p = _in_proj(xs, norm1_g[i], mods, d_w_in[jm].astype(BF16), b=b, t=t, tn=1024)
            qk = _diff_qkprep(p, d_q_g[jm], d_k_g[jm], cos_t, sin_t, b=b, t=t, d=d)
            lam = (jnp.exp(jnp.sum(d_lq1[jm] * d_lk1[jm])) - jnp.exp(jnp.sum(d_lq2[jm] * d_lk2[jm]))
                   + lambda_init).reshape(1).astype(F32)
            v = p[:, 2 * d:]
            vt_lat = _values_t(v[:n_lat].reshape(n_lat // KV_CHUNK, KV_CHUNK, d))
            vt_ctx = _values_t(v[n_lat:].reshape(b, cl, d))
            attn = functools.partial(_diff_attention, qk, vt_lat, vt_ctx, lam, d_sub_g[jm], b=b, t=t, d=d,
                                     lambda_init=lambda_init)
            o = jnp.concatenate([attn(latent=True), attn(latent=False)], axis=0)
            xs, h, logits = _out_proj((o,), d_w_out[jm].astype(BF16), xs, norm2_g[i], mods,
                                      rw_hi, rw_lo, r_bias, b=b, t=t, mlstm=False)
        last = i == depth - 1
        xs = _moe(xs, h, logits, mods, e_w_gu[i], e_w_dn[i], b=b, t=t, n_rows=n_lat if last else ntot)
    return xs.reshape(b, t, d)
```

```python
import functools
import math

import jax
import jax.numpy as jnp
from jax import lax
from jax.experimental import pallas as pl
from jax.experimental.pallas import tpu as pltpu

F32 = jnp.float32
BF16 = jnp.bfloat16
I32 = jnp.int32
HIGHEST = lax.Precision.HIGHEST

EPS = 1e-6
GRID_W = 64
ROPE_BASE = 10000.0
M_HEADS = 4
M_CONV_W = 5
D_HEADS = 8
D_HEAD_DIM = 64
N_GROUPS = 4
EXPERTS_PER_GROUP = 8
N_EXPERTS = 32
D_EXPERT = 512

LANES = 128
CHUNK = 256
ROW_TILE = 512
Q_TILE = 512
KV_CHUNK = 512
SOFTMAX_ROWS = 64
V_EXT = LANES + 16
MOE_BLOCK = 256
DMA_UNROLL = 8
NEG = -1e30
VMEM_LIMIT = 56 * 1024 * 1024


def _cparams(sem):
    return pltpu.CompilerParams(dimension_semantics=sem, vmem_limit_bytes=VMEM_LIMIT)


def _nt_dot(a, b):
    return lax.dot_general(a, b, (((1,), (1,)), ((), ())), preferred_element_type=F32)


def _tn_dot(a, b):
    return lax.dot_general(a, b, (((0,), (0,)), ((), ())), preferred_element_type=F32)


def _mod_spec(d, tiles_per_batch, n_lat_tiles, b):
    return pl.BlockSpec((None, 6, d), lambda i, *_: (jnp.where(i < n_lat_tiles, i // tiles_per_batch, b), 0, 0))


def _mod_kernel(s_ref, w_ref, b_ref, o_ref):
    s = s_ref[...]
    s = s * jax.nn.sigmoid(s)
    o_ref[...] = jnp.dot(s, w_ref[...], precision=HIGHEST, preferred_element_type=F32) + b_ref[...]


def _modulation(cond, ada_w, ada_b):
    depth, d, n = ada_w.shape
    tn = n // 4
    out = pl.pallas_call(
        _mod_kernel,
        grid=(depth, n // tn),
        in_specs=[pl.BlockSpec((8, d), lambda l, j: (0, 0)),
                  pl.BlockSpec((None, d, tn), lambda l, j: (l, 0, j)),
                  pl.BlockSpec((None, 1, tn), lambda l, j: (l, 0, j))],
        out_specs=pl.BlockSpec((None, 8, tn), lambda l, j: (l, 0, j)),
        out_shape=jax.ShapeDtypeStruct((depth, 8, n), F32),
        compiler_params=_cparams(("arbitrary", "arbitrary")),
    )(cond, ada_w, ada_b.reshape(depth, 1, n))
    return out.reshape(depth, 8, 6, d)


def _in_kernel(*refs, has_gates):
    if has_gates:
        x_ref, g_ref, mod_ref, w_ref, wg_ref, gb_ref, o_ref, go_ref, h_ref = refs
    else:
        x_ref, g_ref, mod_ref, w_ref, o_ref, h_ref = refs

    @pl.when(pl.program_id(1) == 0)
    def _():
        x = x_ref[...]
        y = x * lax.rsqrt(jnp.mean(x * x, axis=-1, keepdims=True) + EPS) * g_ref[...]
        hb = (y * (1.0 + mod_ref[1:2, :]) + mod_ref[0:1, :]).astype(BF16)
        h_ref[...] = hb
        if has_gates:
            go_ref[...] = jnp.dot(hb, wg_ref[...], preferred_element_type=F32) + gb_ref[...]

    o_ref[...] = jnp.dot(h_ref[...], w_ref[...], preferred_element_type=F32).astype(o_ref.dtype)


def _in_proj(x, g, mods, w, *, b, t, tn, gates=None):
    ntot, d = x.shape
    n = w.shape[1]
    tm = ROW_TILE
    has_gates = gates is not None
    in_specs = [pl.BlockSpec((tm, d), lambda i, j: (i, 0)),
                pl.BlockSpec((1, d), lambda i, j: (0, 0)),
                _mod_spec(d, t // tm, b * t // tm, b),
                pl.BlockSpec((d, tn), lambda i, j: (0, j))]
    args = [x, g.reshape(1, d), mods, w]
    out_specs = pl.BlockSpec((tm, tn), lambda i, j: (i, j))
    out_shape = jax.ShapeDtypeStruct((ntot, n), BF16)
    if has_gates:
        wg, gb = gates
        in_specs += [pl.BlockSpec((d, LANES), lambda i, j: (0, 0)),
                     pl.BlockSpec((1, LANES), lambda i, j: (0, 0))]
        args += [wg, gb]
        out_specs = [out_specs, pl.BlockSpec((tm, LANES), lambda i, j: (i, 0))]
        out_shape = [out_shape, jax.ShapeDtypeStruct((ntot, LANES), F32)]
    return pl.pallas_call(
        functools.partial(_in_kernel, has_gates=has_gates),
        grid=(ntot // tm, n // tn),
        in_specs=in_specs,
        out_specs=out_specs,
        out_shape=out_shape,
        scratch_shapes=[pltpu.VMEM((tm, d), BF16)],
        compiler_params=_cparams(("arbitrary", "arbitrary")),
    )(*args)


def _conv_kernel(xm_ref, xp_ref, xn_ref, w_ref, b_ref, o_ref, *, ncl, n_lat_chunks, ncol_q, k_scale):
    i = pl.program_id(0)
    j = pl.program_id(1)
    c = i % ncl
    is_lat = i < n_lat_chunks
    left_ok = jnp.logical_and(is_lat, c >= 1)
    right_ok = jnp.logical_and(is_lat, c < ncl - 1)
    xm = xm_ref[...].astype(F32)
    xp = jnp.where(left_ok, xp_ref[...].astype(F32), 0.0)[14:16, :]
    xn = jnp.where(right_ok, xn_ref[...].astype(F32), 0.0)[0:2, :]
    ext = jnp.concatenate([xp, xm, xn], axis=0)
    rows = xm.shape[0]
    w = w_ref[...]
    y = b_ref[...] + w[0:1, :] * ext[0:rows, :]
    for k in range(1, M_CONV_W):
        y = y + w[k:k + 1, :] * ext[k:k + rows, :]
    y = y * jax.nn.sigmoid(y)
    y = y * jnp.where(j >= ncol_q, k_scale, 1.0)
    o_ref[...] = y.astype(o_ref.dtype)


def _mlstm_conv(p, conv_w, conv_b, *, b, t, d):
    ntot = p.shape[0]
    tn = 512
    nrb = ntot // 16
    return pl.pallas_call(
        functools.partial(_conv_kernel, ncl=t // CHUNK, n_lat_chunks=b * t // CHUNK, ncol_q=d // tn,
                          k_scale=(d // M_HEADS) ** -0.5),
        grid=(ntot // CHUNK, 2 * d // tn),
        in_specs=[pl.BlockSpec((CHUNK, tn), lambda i, j: (i, j)),
                  pl.BlockSpec((16, tn), lambda i, j: (jnp.maximum(i * (CHUNK // 16) - 1, 0), j)),
                  pl.BlockSpec((16, tn), lambda i, j: (jnp.minimum((i + 1) * (CHUNK // 16), nrb - 1), j)),
                  pl.BlockSpec((M_CONV_W, tn), lambda i, j: (0, j)),
                  pl.BlockSpec((1, tn), lambda i, j: (0, j))],
        out_specs=pl.BlockSpec((CHUNK, tn), lambda i, j: (i, j)),
        out_shape=jax.ShapeDtypeStruct((ntot, 2 * d), BF16),
        compiler_params=_cparams(("arbitrary", "arbitrary")),
    )(p, p, p, conv_w, conv_b.reshape(1, 2 * d))


def _log_sigmoid(x):
    return jnp.minimum(x, 0.0) - jnp.log(1.0 + jnp.exp(-jnp.abs(x)))


def _mlstm_chunk(q, k, v, gcol, grow, igate, fgate, mask, mask_t, ct_ref, n_ref, m_ref, o_ref):
    i_row = grow[igate:igate + 1, :]
    i_col = gcol[:, igate:igate + 1]
    f_row = _log_sigmoid(grow[fgate:fgate + 1, :])
    f_col = _log_sigmoid(gcol[:, fgate:fgate + 1])
    cum_col = jnp.sum(jnp.where(mask, f_row, 0.0), axis=1, keepdims=True)
    cum_row = jnp.sum(jnp.where(mask_t, f_col, 0.0), axis=0, keepdims=True)
    tot = jnp.sum(f_col, axis=0, keepdims=True)
    m_prev = m_ref[...]
    dmat = jnp.where(mask, cum_col - cum_row + i_row, NEG)
    m_inter = cum_col + m_prev
    m_out = jnp.maximum(m_inter, jnp.max(dmat, axis=1, keepdims=True))
    w_intra = jnp.exp(dmat - m_out)
    w_inter = jnp.exp(m_inter - m_out)
    sc = _nt_dot(q, k) * w_intra
    ct = ct_ref[...]
    n_row = n_ref[...]
    qf = q.astype(F32)
    kf = k.astype(F32)
    vf = v.astype(F32)
    num = (w_inter * jnp.dot(q, ct.astype(BF16), preferred_element_type=F32)
           + jnp.dot(sc.astype(BF16), v, preferred_element_type=F32))
    den = w_inter * jnp.sum(qf * n_row, axis=1, keepdims=True) + jnp.sum(sc, axis=1, keepdims=True)
    o_ref[...] = (num / jnp.maximum(jnp.abs(den), jnp.exp(-m_out))).astype(o_ref.dtype)
    g_col = tot - cum_col + i_col
    m_next = jnp.maximum(tot + m_prev, jnp.max(g_col, axis=0, keepdims=True))
    decay = jnp.exp(tot + m_prev - m_next)
    wg = jnp.exp(g_col - m_next)
    ct_ref[...] = decay * ct + _tn_dot(k, (wg * vf).astype(BF16))
    n_ref[...] = decay * n_row + jnp.sum(wg * kf, axis=0, keepdims=True)
    m_ref[...] = m_next


def _scan_kernel(qf_ref, kf_ref, vf_ref, gcf_ref, grf_ref, qb_ref, kb_ref, vb_ref, gcb_ref, grb_ref,
                 of_ref, ob_ref, ctf_ref, nf_ref, mf_ref, ctb_ref, nb_ref, mb_ref):
    @pl.when(pl.program_id(2) == 0)
    def _():
        for r in (ctf_ref, nf_ref, mf_ref, ctb_ref, nb_ref, mb_ref):
            r[...] = jnp.zeros(r.shape, r.dtype)

    t_i = lax.broadcasted_iota(I32, (CHUNK, CHUNK), 0)
    s_i = lax.broadcasted_iota(I32, (CHUNK, CHUNK), 1)
    lower = s_i <= t_i
    upper = s_i >= t_i
    _mlstm_chunk(qf_ref[...], kf_ref[...], vf_ref[...], gcf_ref[...], grf_ref[...], 0, 2,
                 lower, upper, ctf_ref, nf_ref, mf_ref, of_ref)
    _mlstm_chunk(qb_ref[...], kb_ref[...], vb_ref[...], gcb_ref[...], grb_ref[...], 1, 3,
                 upper, lower, ctb_ref, nb_ref, mb_ref, ob_ref)


def _mlstm_scan(qk, p, gcol, grow, *, b, t, d):
    ntot = qk.shape[0]
    dh = d // M_HEADS
    ncl = t // CHUNK
    ctx0 = b * ncl

    def fwd(bi, s):
        return jnp.where(s == 0, ctx0 + bi, bi * ncl + s - 1)

    def bwd(bi, s):
        return jnp.where(s == 0, ctx0 + bi, bi * ncl + ncl - s)

    def specs(order):
        return [pl.BlockSpec((CHUNK, dh), lambda bi, h, s: (order(bi, s), h)),
                pl.BlockSpec((CHUNK, dh), lambda bi, h, s: (order(bi, s), M_HEADS + h)),
                pl.BlockSpec((CHUNK, dh), lambda bi, h, s: (order(bi, s), 2 * M_HEADS + h)),
                pl.BlockSpec((None, CHUNK, 4), lambda bi, h, s: (h, order(bi, s), 0)),
                pl.BlockSpec((None, 4, CHUNK), lambda bi, h, s: (h, 0, order(bi, s)))]

    def ospec(order):
        return pl.BlockSpec((CHUNK, dh), lambda bi, h, s: (order(bi, s), h))

    state = [pltpu.VMEM((dh, dh), F32), pltpu.VMEM((1, dh), F32), pltpu.VMEM((1, 1), F32)]
    return pl.pallas_call(
        _scan_kernel,
        grid=(b, M_HEADS, ncl + 1),
        in_specs=specs(fwd) + specs(bwd),
        out_specs=[ospec(fwd), ospec(bwd)],
        out_shape=[jax.ShapeDtypeStruct((ntot, d), F32)] * 2,
        scratch_shapes=state + state,
        compiler_params=_cparams(("arbitrary", "arbitrary", "arbitrary")),
    )(qk, qk, p, gcol, grow, qk, qk, p, gcol, grow)


def _qkprep_kernel(p_ref, gq_ref, gk_ref, cos_ref, sin_ref, o_ref, *, n_lat_chunks, nblk_q, q_scale):
    is_ctx = pl.program_id(0) >= n_lat_chunks
    cos = jnp.where(is_ctx, 1.0, cos_ref[...])
    sin = jnp.where(is_ctx, 0.0, sin_ref[...])
    lane = lax.broadcasted_iota(I32, (CHUNK, LANES), 1)
    first_half = (lane % 32) < 16
    r_i = lax.broadcasted_iota(I32, (LANES, LANES), 0)
    c_i = lax.broadcasted_iota(I32, (LANES, LANES), 1)
    group = jnp.where((r_i // D_HEAD_DIM) == (c_i // D_HEAD_DIM), 1.0, 0.0).astype(BF16)
    for blk in range(2 * nblk_q):
        x = p_ref[:, blk * LANES:(blk + 1) * LANES].astype(F32)
        ss = x * x
        ss_hi = ss.astype(BF16)
        ss_lo = (ss - ss_hi.astype(F32)).astype(BF16)
        ssum = (jnp.dot(ss_hi, group, preferred_element_type=F32)
                + jnp.dot(ss_lo, group, preferred_element_type=F32))
        g = gq_ref[...] if blk < nblk_q else gk_ref[...]
        xn = x * lax.rsqrt(ssum * (1.0 / D_HEAD_DIM) + EPS) * g
        partner = jnp.where(first_half, pltpu.roll(xn, LANES - 16, 1), pltpu.roll(xn, 16, 1))
        y = xn * cos + partner * sin
        if blk < nblk_q:
            y = y * q_scale
        o_ref[:, blk * LANES:(blk + 1) * LANES] = y.astype(o_ref.dtype)


def _diff_qkprep(p, q_g, k_g, cos_t, sin_t, *, b, t, d):
    ntot = p.shape[0]
    ncl = t // CHUNK
    g2 = lambda g: jnp.tile(g.astype(F32), 2).reshape(1, LANES)
    return pl.pallas_call(
        functools.partial(_qkprep_kernel, n_lat_chunks=b * ncl, nblk_q=d // LANES,
                          q_scale=D_HEAD_DIM ** -0.5 * math.log2(math.e)),
        grid=(ntot // CHUNK,),
        in_specs=[pl.BlockSpec((CHUNK, 2 * d), lambda i: (i, 0)),
                  pl.BlockSpec((1, LANES), lambda i: (0, 0)),
                  pl.BlockSpec((1, LANES), lambda i: (0, 0)),
                  pl.BlockSpec((CHUNK, LANES), lambda i: (i % ncl, 0)),
                  pl.BlockSpec((CHUNK, LANES), lambda i: (i % ncl, 0))],
        out_specs=pl.BlockSpec((CHUNK, 2 * d), lambda i: (i, 0)),
        out_shape=jax.ShapeDtypeStruct((ntot, 2 * d), BF16),
        compiler_params=_cparams(("arbitrary",)),
    )(p, g2(q_g), g2(k_g), cos_t, sin_t)


def _rope_tables(t):
    rows = t // GRID_W
    row = jnp.repeat(jnp.arange(rows), GRID_W).astype(F32)[:, None]
    col = jnp.tile(jnp.arange(GRID_W), rows).astype(F32)[:, None]
    nf = D_HEAD_DIM // 4
    inv = ROPE_BASE ** (-jnp.arange(nf, dtype=F32) / nf)
    ar, ac = row * inv, col * inv
    cos64 = jnp.concatenate([jnp.cos(ar), jnp.cos(ar), jnp.cos(ac), jnp.cos(ac)], axis=-1)
    sin64 = jnp.concatenate([-jnp.sin(ar), jnp.sin(ar), -jnp.sin(ac), jnp.sin(ac)], axis=-1)
    return jnp.tile(cos64, (1, 2)), jnp.tile(sin64, (1, 2))


def _attn_kernel(*refs, n_lat, out_scale):
    if n_lat:
        lam_ref, q_ref, kc_ref, vtc_ref, kl_ref, vtl_ref, sg_ref, o_ref, m1, a1, m2, a2 = refs[:12]
        bufs = refs[12:]
    else:
        lam_ref, q_ref, kc_ref, vtc_ref, sg_ref, o_ref, m1, a1, m2, a2 = refs[:10]
        bufs = refs[10:]
    sets = [tuple(bufs[i + 2 * k:i + 2 * k + 2] for k in range(4)) for i in range(0, len(bufs), 8)]
    q = q_ref[...]
    lane = lax.broadcasted_iota(I32, q.shape, 1)
    zero = jnp.zeros_like(q)
    qs = (jnp.where(lane < D_HEAD_DIM, q, zero), jnp.where(lane >= D_HEAD_DIM, q, zero))
    maxes = (m1, m2)
    accs = (a1, a2)
    for m_r, a_r in zip(maxes, accs):
        m_r[...] = jnp.full(m_r.shape, NEG, F32)
        a_r[...] = jnp.zeros(a_r.shape, F32)

    def scores(keys, s_bufs, mx_bufs):
        for qm, s_ref, mx_ref in zip(qs, s_bufs, mx_bufs):
            st = _nt_dot(keys, qm)
            s_ref[...] = st
            mx = st[0:8]
            for r in range(8, st.shape[0], 8):
                mx = jnp.maximum(mx, st[r:r + 8])
            mx_ref[...] = mx

    def softmax(s_bufs, mx_bufs, p_bufs, al_bufs):
        for s_ref, mx_ref, p_ref, al_ref, m_r in zip(s_bufs, mx_bufs, p_bufs, al_bufs, maxes):
            m_old = m_r[...]
            m_new = jnp.maximum(m_old, jnp.max(mx_ref[...], axis=0, keepdims=True))
            m_r[...] = m_new
            al_ref[...] = jnp.exp2(m_old - m_new)
            for r0 in range(0, s_ref.shape[0], SOFTMAX_ROWS):
                p = jnp.exp2(s_ref[r0:r0 + SOFTMAX_ROWS, :] - m_new)
                p_ref[r0:r0 + SOFTMAX_ROWS, :] = p.astype(BF16)

    def weigh(vt, p_bufs, al_bufs):
        for p_ref, al_ref, a_r in zip(p_bufs, al_bufs, accs):
            a_r[...] = al_ref[...] * a_r[...] + jnp.dot(vt, p_ref[...], preferred_element_type=F32)

    s_c, mx_c, p_c, al_c = sets[0]
    scores(kc_ref[...], s_c, mx_c)
    softmax(s_c, mx_c, p_c, al_c)
    weigh(vtc_ref[...], p_c, al_c)

    if n_lat:
        def keys(j):
            return kl_ref[pl.ds(pl.multiple_of(j * KV_CHUNK, KV_CHUNK), KV_CHUNK), :]

        groups = (sets[1:3], sets[3:5])
        n_pairs = n_lat // 2

        def qk(g, group):
            for k, (s_x, mx_x, _, _) in enumerate(group):
                scores(keys(2 * g + k), s_x, mx_x)

        def sm(group):
            for s_x, mx_x, p_x, al_x in group:
                softmax(s_x, mx_x, p_x, al_x)

        def pv(g, group):
            for k, (_, _, p_x, al_x) in enumerate(group):
                weigh(vtl_ref[2 * g + k], p_x, al_x)

        qk(0, groups[0])
        sm(groups[0])
        qk(1, groups[1])

        def body(i, carry):
            g = 2 * i + 2
            sm(groups[1])
            qk(g, groups[0])
            pv(g - 2, groups[0])
            sm(groups[0])
            qk(g + 1, groups[1])
            pv(g - 1, groups[1])
            return carry

        lax.fori_loop(0, (n_pairs - 2) // 2, body, 0)
        sm(groups[1])
        pv(n_pairs - 2, groups[0])
        pv(n_pairs - 1, groups[1])
    o = (a1[0:LANES, :] / a1[LANES:LANES + 1, :]
         - lam_ref[0] * (a2[0:LANES, :] / a2[LANES:LANES + 1, :]))
    o = o * lax.rsqrt(jnp.mean(o * o, axis=0, keepdims=True) + EPS) * sg_ref[...] * out_scale
    o_ref[...] = o.T.astype(o_ref.dtype)


def _values_t(v):
    n, tk, d = v.shape
    nh = d // LANES
    vt = v.transpose(0, 2, 1).reshape(n, nh, LANES, tk)
    ones = jnp.ones((n, nh, 1, tk), v.dtype)
    pad = jnp.zeros((n, nh, V_EXT - LANES - 1, tk), v.dtype)
    return jnp.concatenate([vt, ones, pad], axis=2).reshape(n, nh * V_EXT, tk)


def _diff_attention(qk, vt_lat, vt_ctx, lam, sub_g, *, b, t, d, lambda_init, latent):
    nh = d // LANES
    ctx0 = b * t // CHUNK
    n_lat = t // KV_CHUNK if latent else 0
    tq = Q_TILE if latent else CHUNK
    nq = t // tq if latent else 1
    q_row = (lambda bi, c: bi * nq + c) if latent else (lambda bi, c: ctx0 + bi)
    in_specs = [pl.BlockSpec((tq, LANES), lambda bi, h, c, lam_r: (q_row(bi, c), h)),
                pl.BlockSpec((CHUNK, LANES), lambda bi, h, c, lam_r: (ctx0 + bi, nh + h)),
                pl.BlockSpec((None, V_EXT, CHUNK), lambda bi, h, c, lam_r: (bi, h, 0))]
    args = [qk, qk, vt_ctx]
    if latent:
        in_specs += [pl.BlockSpec((t, LANES), lambda bi, h, c, lam_r: (bi, nh + h)),
                     pl.BlockSpec((n_lat, V_EXT, KV_CHUNK), lambda bi, h, c, lam_r: (bi, h, 0))]
        args += [qk, vt_lat]
    in_specs.append(pl.BlockSpec((LANES, 1), lambda bi, h, c, lam_r: (0, 0)))
    args.append(sub_g.reshape(LANES, 1).astype(F32))
    stat = pltpu.VMEM((1, tq), F32)
    acc = pltpu.VMEM((V_EXT, tq), F32)

    def buffer_set(tk):
        return ([pltpu.VMEM((tk, tq), F32)] * 2 + [pltpu.VMEM((8, tq), F32)] * 2
                + [pltpu.VMEM((tk, tq), BF16)] * 2 + [stat] * 2)

    grid_spec = pltpu.PrefetchScalarGridSpec(
        num_scalar_prefetch=1,
        grid=(b, nh, nq),
        in_specs=in_specs,
        out_specs=pl.BlockSpec((tq, LANES), lambda bi, h, c, lam_r: (bi * nq + c, h)),
        scratch_shapes=[stat, acc, stat, acc] + buffer_set(CHUNK) + (buffer_set(KV_CHUNK) * 4 if latent else []))
    return pl.pallas_call(
        functools.partial(_attn_kernel, n_lat=n_lat, out_scale=1.0 - lambda_init),
        grid_spec=grid_spec,
        out_shape=jax.ShapeDtypeStruct((b * nq * tq, d), BF16),
        compiler_params=_cparams(("arbitrary", "arbitrary", "arbitrary")),
    )(lam, *args)


def _out_kernel(*refs, mlstm):
    if mlstm:
        (rf_ref, rb_ref, op_ref, og_ref, w_ref, x_ref, g2_ref, mod_ref, rwh_ref, rwl_ref, rb2_ref,
         xo_ref, h_ref, lg_ref) = refs
        r = rf_ref[...] + rb_ref[...]
        dh = r.shape[1] // M_HEADS
        parts = []
        for hh in range(M_HEADS):
            rr = r[:, hh * dh:(hh + 1) * dh]
            parts.append(rr * lax.rsqrt(jnp.mean(rr * rr, axis=-1, keepdims=True) + EPS))
        rn = jnp.concatenate(parts, axis=1) * og_ref[...]
        u = (jax.nn.sigmoid(op_ref[...].astype(F32)) * rn).astype(BF16)
    else:
        (u_ref, w_ref, x_ref, g2_ref, mod_ref, rwh_ref, rwl_ref, rb2_ref, xo_ref, h_ref, lg_ref) = refs
        u = u_ref[...]
    y = jnp.dot(u, w_ref[...], preferred_element_type=F32)
    x = x_ref[...] + mod_ref[2:3, :] * y
    xo_ref[...] = x
    hn = x * lax.rsqrt(jnp.mean(x * x, axis=-1, keepdims=True) + EPS) * g2_ref[...]
    h = hn * (1.0 + mod_ref[4:5, :]) + mod_ref[3:4, :]
    h_ref[...] = h
    h_hi = h.astype(BF16)
    h_lo = (h - h_hi.astype(F32)).astype(BF16)
    lg_ref[...] = (jnp.dot(h_hi, rwh_ref[...], preferred_element_type=F32)
                   + jnp.dot(h_lo, rwh_ref[...], preferred_element_type=F32)
                   + jnp.dot(h_hi, rwl_ref[...], preferred_element_type=F32) + rb2_ref[...])


def _out_proj(u_args, w_out, x, g2, mods, rw_hi, rw_lo, r_bias, *, b, t, mlstm):
    ntot, d = x.shape
    tm = ROW_TILE
    row = lambda i: (i, 0)
    const = lambda i: (0, 0)
    if mlstm:
        rf, rb, p, out_g = u_args
        u_specs = [pl.BlockSpec((tm, d), row), pl.BlockSpec((tm, d), row),
                   pl.BlockSpec((tm, d), lambda i: (i, 3)), pl.BlockSpec((1, d), const)]
        u_in = [rf, rb, p, out_g.reshape(1, d)]
    else:
        u_specs = [pl.BlockSpec((tm, d), row)]
        u_in = list(u_args)
    in_specs = u_specs + [pl.BlockSpec((d, d), const),
                          pl.BlockSpec((tm, d), row),
                          pl.BlockSpec((1, d), const),
                          _mod_spec(d, t // tm, b * t // tm, b),
                          pl.BlockSpec((d, LANES), const),
                          pl.BlockSpec((d, LANES), const),
                          pl.BlockSpec((1, LANES), const)]
    return pl.pallas_call(
        functools.partial(_out_kernel, mlstm=mlstm),
        grid=(ntot // tm,),
        in_specs=in_specs,
        out_specs=[pl.BlockSpec((tm, d), row), pl.BlockSpec((tm, d), row), pl.BlockSpec((tm, LANES), row)],
        out_shape=[jax.ShapeDtypeStruct((ntot, d), F32), jax.ShapeDtypeStruct((ntot, d), F32),
                   jax.ShapeDtypeStruct((ntot, LANES), F32)],
        compiler_params=_cparams(("arbitrary",)),
    )(*u_in, w_out, x, g2.reshape(1, d), mods, rw_hi, rw_lo, r_bias)


def _lane_pick(lane, pairs):
    out = jnp.zeros(lane.shape, F32)
    for idx, val in pairs:
        out = jnp.where(lane == idx, val, out)
    return out


def _route_kernel(lg_ref, rt_ref, cnt_ref):
    @pl.when(pl.program_id(0) == 0)
    def _():
        cnt_ref[...] = jnp.zeros(cnt_ref.shape, F32)

    lg = lg_ref[...]
    lane = lax.broadcasted_iota(I32, lg.shape, 1)
    lane_f = lane.astype(F32)
    big = 1e9
    is_g = lane < N_GROUPS
    gl = jnp.where(is_g, lg, NEG)
    gmax = jnp.max(gl, axis=1, keepdims=True)
    gidx = jnp.min(jnp.where(gl == gmax, lane_f, big), axis=1, keepdims=True)
    gsum = jnp.sum(jnp.where(is_g, jnp.exp(gl - gmax), 0.0), axis=1, keepdims=True)
    gp_top = 1.0 / gsum
    e_lane = lane_f - float(N_GROUPS)
    lo = gidx * float(EXPERTS_PER_GROUP)
    in_grp = jnp.where(e_lane >= lo, jnp.where(e_lane < lo + float(EXPERTS_PER_GROUP), 1.0, 0.0), 0.0) > 0.5
    el = jnp.where(in_grp, lg, NEG)
    v0 = jnp.max(el, axis=1, keepdims=True)
    i0 = jnp.min(jnp.where(el == v0, lane_f, big), axis=1, keepdims=True)
    el2 = jnp.where(lane_f == i0, NEG, el)
    v1 = jnp.max(el2, axis=1, keepdims=True)
    i1 = jnp.min(jnp.where(el2 == v1, lane_f, big), axis=1, keepdims=True)
    e1 = jnp.exp(v1 - v0)
    g0 = gp_top / (1.0 + e1)
    g1 = gp_top * e1 / (1.0 + e1)
    eid0 = i0 - float(N_GROUPS)
    eid1 = i1 - float(N_GROUPS)
    rt_ref[...] = _lane_pick(lane, ((0, eid0), (1, eid1), (2, g0), (3, g1)))
    hot = jnp.where(lane_f == eid0, 1.0, 0.0) + jnp.where(lane_f == eid1, 1.0, 0.0)
    cnt_ref[0:1, :] += jnp.sum(hot, axis=0, keepdims=True)


def _route(logits):
    ntot = logits.shape[0]
    return pl.pallas_call(
        _route_kernel,
        grid=(ntot // CHUNK,),
        in_specs=[pl.BlockSpec((CHUNK, LANES), lambda i: (i, 0))],
        out_specs=[pl.BlockSpec((CHUNK, LANES), lambda i: (i, 0)), pl.BlockSpec((8, LANES), lambda i: (0, 0))],
        out_shape=[jax.ShapeDtypeStruct((ntot, LANES), F32), jax.ShapeDtypeStruct((8, LANES), F32)],
        compiler_params=_cparams(("arbitrary",)),
    )(logits)


def _slot_kernel(rt_ref, cnt_ref, o_ref, run_ref):
    @pl.when(pl.program_id(0) == 0)
    def _():
        cnt = cnt_ref[0:1, :].astype(I32)
        padded = (((cnt + (MOE_BLOCK - 1)) // MOE_BLOCK) * MOE_BLOCK).astype(F32)
        r_i = lax.broadcasted_iota(I32, (LANES, LANES), 0)
        c_i = lax.broadcasted_iota(I32, (LANES, LANES), 1)
        col = jnp.sum(jnp.where(r_i == c_i, jnp.broadcast_to(padded, (LANES, LANES)), 0.0), axis=1, keepdims=True)
        run_ref[...] = jnp.sum(jnp.where(r_i < c_i, col, 0.0), axis=0, keepdims=True)

    rt = rt_ref[...]
    lane = lax.broadcasted_iota(I32, rt.shape, 1)
    lane_f = lane.astype(F32)
    hot0 = lane_f == rt[:, 0:1]
    hot1 = lane_f == rt[:, 1:2]
    both = jnp.where(hot0, 1.0, 0.0) + jnp.where(hot1, 1.0, 0.0)
    t_i = lax.broadcasted_iota(I32, (CHUNK, CHUNK), 0)
    s_i = lax.broadcasted_iota(I32, (CHUNK, CHUNK), 1)
    before = jnp.where(s_i < t_i, 1.0, 0.0).astype(BF16)
    base = run_ref[...] + jnp.dot(before, both.astype(BF16), preferred_element_type=F32)
    d0 = jnp.sum(jnp.where(hot0, base, 0.0), axis=1, keepdims=True)
    d1 = jnp.sum(jnp.where(hot1, base, 0.0), axis=1, keepdims=True)
    o_ref[...] = _lane_pick(lane, ((0, d0), (1, d1))).astype(I32)
    run_ref[...] += jnp.sum(both, axis=0, keepdims=True)


def _slots(route, counts):
    ntot = route.shape[0]
    return pl.pallas_call(
        _slot_kernel,
        grid=(ntot // CHUNK,),
        in_specs=[pl.BlockSpec((CHUNK, LANES), lambda i: (i, 0)), pl.BlockSpec((8, LANES), lambda i: (0, 0))],
        out_specs=pl.BlockSpec((CHUNK, LANES), lambda i: (i, 0)),
        out_shape=jax.ShapeDtypeStruct((ntot, LANES), I32),
        scratch_shapes=[pltpu.VMEM((1, LANES), F32)],
        compiler_params=_cparams(("arbitrary",)),
    )(route, counts)


def _dispatch_kernel(dest_ref, h_ref, xs_in_ref, xs_ref, sem):
    del xs_in_ref

    def row_copy(r, k):
        return pltpu.make_async_copy(h_ref.at[pl.ds(r, 1), :], xs_ref.at[pl.ds(dest_ref[0, 2 * r + k], 1), :], sem)

    def start(r, carry):
        row_copy(r, 0).start(priority=0)
        row_copy(r, 1).start(priority=1)
        return carry

    def wait(r, carry):
        row_copy(r, 0).wait()
        row_copy(r, 1).wait()
        return carry

    lax.fori_loop(0, CHUNK, start, 0, unroll=DMA_UNROLL)
    lax.fori_loop(0, CHUNK, wait, 0, unroll=DMA_UNROLL)


def _dispatch(dest, h, buf):
    ntot, d = h.shape
    cap = buf.shape[0]
    return pl.pallas_call(
        _dispatch_kernel,
        grid=(ntot // CHUNK,),
        in_specs=[pl.BlockSpec((None, 1, 2 * CHUNK), lambda i: (i, 0, 0), memory_space=pltpu.SMEM),
                  pl.BlockSpec((CHUNK, d), lambda i: (i, 0)),
                  pl.BlockSpec(memory_space=pl.ANY)],
        out_specs=pl.BlockSpec(memory_space=pl.ANY),
        out_shape=jax.ShapeDtypeStruct((cap, d), F32),
        scratch_shapes=[pltpu.SemaphoreType.DMA],
        input_output_aliases={2: 0},
        compiler_params=_cparams(("arbitrary",)),
    )(dest, h, buf)


def _expert_kernel(be_ref, nu_ref, xs_ref, wgu_ref, wdn_ref, ys_ref, wgu_bf, wdn_bf):
    j = pl.program_id(0)
    changed = jnp.logical_or(j == 0, be_ref[j] != be_ref[jnp.maximum(j - 1, 0)])

    @pl.when(changed)
    def _():
        wgu_bf[...] = wgu_ref[...].astype(BF16)
        wdn_bf[...] = wdn_ref[...].astype(BF16)

    @pl.when(j < nu_ref[0])
    def _():
        gu = jnp.dot(xs_ref[...].astype(BF16), wgu_bf[...], preferred_element_type=F32)
        g = gu[:, :D_EXPERT]
        a = (g * jax.nn.sigmoid(g) * gu[:, D_EXPERT:]).astype(BF16)
        ys_ref[...] = jnp.dot(a, wdn_bf[...], preferred_element_type=F32)

    @pl.when(j >= nu_ref[0])
    def _():
        ys_ref[...] = jnp.zeros(ys_ref.shape, ys_ref.dtype)


def _experts(blk_e, n_used, xs, w_gu, w_dn, layer):
    cap, d = xs.shape
    row = lambda j, be, nu: (jnp.minimum(j, nu[0] - 1), 0)
    grid_spec = pltpu.PrefetchScalarGridSpec(
        num_scalar_prefetch=2,
        grid=(cap // MOE_BLOCK,),
        in_specs=[pl.BlockSpec((MOE_BLOCK, d), row),
                  pl.BlockSpec((None, None, d, 2 * D_EXPERT), lambda j, be, nu: (layer, be[j], 0, 0)),
                  pl.BlockSpec((None, None, D_EXPERT, d), lambda j, be, nu: (layer, be[j], 0, 0))],
        out_specs=pl.BlockSpec((MOE_BLOCK, d), lambda j, be, nu: (j, 0)),
        scratch_shapes=[pltpu.VMEM((d, 2 * D_EXPERT), BF16), pltpu.VMEM((D_EXPERT, d), BF16)])
    return pl.pallas_call(
        _expert_kernel,
        grid_spec=grid_spec,
        out_shape=jax.ShapeDtypeStruct((cap, d), F32),
        compiler_params=_cparams(("arbitrary",)),
    )(blk_e, n_used, xs, w_gu, w_dn)


def _combine_kernel(dest_ref, rt_ref, x_ref, mod_ref, ys_ref, xo_ref, buf, sem):
    def row_copy(r, k):
        return pltpu.make_async_copy(ys_ref.at[pl.ds(dest_ref[0, 2 * r + k], 1), :],
                                     buf.at[pl.ds(k * CHUNK + r, 1), :], sem)

    def start(r, carry):
        row_copy(r, 0).start(priority=0)
        row_copy(r, 1).start(priority=1)
        return carry

    def wait(r, carry):
        row_copy(r, 0).wait()
        row_copy(r, 1).wait()
        return carry

    lax.fori_loop(0, CHUNK, start, 0, unroll=DMA_UNROLL)
    lax.fori_loop(0, CHUNK, wait, 0, unroll=DMA_UNROLL)
    rt = rt_ref[...]
    y = rt[:, 2:3] * buf[0:CHUNK, :] + rt[:, 3:4] * buf[CHUNK:2 * CHUNK, :]
    xo_ref[...] = x_ref[...] + mod_ref[5:6, :] * y


def _combine(dest, route, x, mods, ys, *, b, t, n_rows):
    d = x.shape[1]
    ncl = t // CHUNK
    return pl.pallas_call(
        _combine_kernel,
        grid=(n_rows // CHUNK,),
        in_specs=[pl.BlockSpec((None, 1, 2 * CHUNK), lambda i: (i, 0, 0), memory_space=pltpu.SMEM),
                  pl.BlockSpec((CHUNK, LANES), lambda i: (i, 0)),
                  pl.BlockSpec((CHUNK, d), lambda i: (i, 0)),
                  _mod_spec(d, ncl, b * ncl, b),
                  pl.BlockSpec(memory_space=pl.ANY)],
        out_specs=pl.BlockSpec((CHUNK, d), lambda i: (i, 0)),
        out_shape=jax.ShapeDtypeStruct((n_rows, d), F32),
        scratch_shapes=[pltpu.VMEM((2 * CHUNK, d), F32), pltpu.SemaphoreType.DMA],
        compiler_params=_cparams(("arbitrary",)),
    )(dest, route, x, mods, ys)


def _moe_capacity(ntot):
    return -(-2 * ntot // MOE_BLOCK) * MOE_BLOCK + N_EXPERTS * MOE_BLOCK


def _moe(x, h, logits, mods, w_gu, w_dn, layer, slot_buf, *, b, t, n_rows):
    ntot = x.shape[0]
    route, counts = _route(logits)
    slot = _slots(route, counts)
    dest = slot[:, :2].reshape(ntot // CHUNK, 1, 2 * CHUNK)
    nblk = slot_buf.shape[0] // MOE_BLOCK
    cnt = counts[0, :N_EXPERTS].astype(I32)
    pad_end = jnp.cumsum((cnt + MOE_BLOCK - 1) // MOE_BLOCK * MOE_BLOCK)
    n_used = pad_end[N_EXPERTS - 1:] // MOE_BLOCK
    blk_start = jnp.minimum(jnp.arange(nblk, dtype=I32), n_used - 1) * MOE_BLOCK
    blk_e = jnp.minimum(jnp.sum((pad_end[None, :] <= blk_start[:, None]).astype(I32), axis=1), N_EXPERTS - 1)
    xs = _dispatch(dest, h, slot_buf)
    ys = _experts(blk_e.astype(I32), n_used.astype(I32), xs, w_gu, w_dn, layer)
    return _combine(dest, route, x, mods, ys, b=b, t=t, n_rows=n_rows), xs


def kernel(x, c, ctx, c_ctx, ada_w, ada_b, norm1_g, norm2_g, m_w_in, m_conv_w, m_conv_b, m_gate_b, m_out_g,
           m_w_out, d_w_in, d_q_g, d_k_g, d_lq1, d_lk1, d_lq2, d_lk2, d_sub_g, d_w_out, r_grp_w, r_grp_b,
           r_exp_w, r_exp_b, e_w_gu, e_w_dn):
    b, t, d = x.shape
    cl = ctx.shape[1]
    depth = ada_w.shape[0]
    assert cl == CHUNK and d == 8 * LANES and b + 1 <= 8
    assert t % Q_TILE == 0 and t % (4 * KV_CHUNK) == 0 and t % ROW_TILE == 0 and (b * cl) % ROW_TILE == 0
    n_lat = b * t
    ntot = n_lat + b * cl

    xs = jnp.concatenate([x.reshape(n_lat, d), ctx.reshape(b * cl, d)], axis=0)
    cond = jnp.zeros((8, d), F32).at[:b].set(c).at[b].set(c_ctx)
    mods_all = _modulation(cond, ada_w, ada_b)
    cos_t, sin_t = _rope_tables(t)
    slot_buf = jnp.zeros((_moe_capacity(ntot), d), F32)

    n_route = N_GROUPS + N_EXPERTS
    for i in range(depth):
        jm = i // 2
        mods = mods_all[i]
        rw = jnp.zeros((d, LANES), F32).at[:, :N_GROUPS].set(r_grp_w[i]).at[:, N_GROUPS:n_route].set(r_exp_w[i])
        rw_hi = rw.astype(BF16)
        rw_lo = (rw - rw_hi.astype(F32)).astype(BF16)
        r_bias = jnp.zeros((1, LANES), F32).at[0, :N_GROUPS].set(r_grp_b[i]).at[0, N_GROUPS:n_route].set(r_exp_b[i])
        if i % 2 == 0:
            w_in = m_w_in[jm]
            wg = jnp.zeros((d, LANES), F32).at[:, :4 * M_HEADS].set(w_in[:, 4 * d:]).astype(BF16)
            gb = jnp.zeros((1, LANES), F32).at[0, :4 * M_HEADS].set(m_gate_b[jm])
            p, gates = _in_proj(xs, norm1_g[i], mods, w_in[:, :4 * d].astype(BF16), b=b, t=t, tn=1024,
                                gates=(wg, gb))
            qk = _mlstm_conv(p, m_conv_w[jm], m_conv_b[jm], b=b, t=t, d=d)
            g4 = gates[:, :4 * M_HEADS].reshape(ntot, 4, M_HEADS)
            rf, rb = _mlstm_scan(qk, p, g4.transpose(2, 0, 1), g4.transpose(2, 1, 0), b=b, t=t, d=d)
            xs, h, logits = _out_proj((rf, rb, p, m_out_g[jm]), m_w_out[jm].astype(BF16), xs, norm2_g[i], mods,
                                      rw_hi, rw_lo, r_bias, b=b, t=t, mlstm=True)
        else:
            lambda_init = 0.8 - 0.6 * math.exp(-0.3 * i)
            p = _in_proj(xs, norm1_g[i], mods, d_w_in[jm].astype(BF16), b=b, t=t, tn=1024)
            qk = _diff_qkprep(p, d_q_g[jm], d_k_g[jm], cos_t, sin_t, b=b, t=t, d=d)
            lam = (jnp.exp(jnp.sum(d_lq1[jm] * d_lk1[jm])) - jnp.exp(jnp.sum(d_lq2[jm] * d_lk2[jm]))
                   + lambda_init).reshape(1).astype(F32)
            v = p[:, 2 * d:]
            vt_lat = _values_t(v[:n_lat].reshape(n_lat // KV_CHUNK, KV_CHUNK, d))
            vt_ctx = _values_t(v[n_lat:].reshape(b, cl, d))
            attn = functools.partial(_diff_attention, qk, vt_lat, vt_ctx, lam, d_sub_g[jm], b=b, t=t, d=d,
                                     lambda_init=lambda_init)
            o = jnp.concatenate([attn(latent=True), attn(latent=False)], axis=0)
            xs, h, logits = _out_proj((o,), d_w_out[jm].astype(BF16), xs, norm2_g[i], mods,
                                      rw_hi, rw_lo, r_bias, b=b, t=t, mlstm=False)
        last = i == depth - 1
        xs, slot_buf = _moe(xs, h, logits, mods, e_w_gu, e_w_dn, i, slot_buf, b=b, t=t,
                            n_rows=n_lat if last else ntot)
    return xs.reshape(b, t, d)
```

```python
import functools
import math

import jax
import jax.numpy as jnp
from jax import lax
from jax.experimental import pallas as pl
from jax.experimental.pallas import tpu as pltpu

F32 = jnp.float32
BF16 = jnp.bfloat16
I32 = jnp.int32
HIGHEST = lax.Precision.HIGHEST

EPS = 1e-6
GRID_W = 64
ROPE_BASE = 10000.0
M_HEADS = 4
M_CONV_W = 5
D_HEADS = 8
D_HEAD_DIM = 64
N_GROUPS = 4
EXPERTS_PER_GROUP = 8
N_EXPERTS = 32
D_EXPERT = 512

LANES = 128
CHUNK = 256
ROW_TILE = 512
Q_TILE = 512
KV_CHUNK = 512
SOFTMAX_ROWS = 64
MOE_BLOCK = 256
DMA_UNROLL = 8
NEG = -1e30
VMEM_LIMIT = 56 * 1024 * 1024


def _cparams(sem):
    return pltpu.CompilerParams(dimension_semantics=sem, vmem_limit_bytes=VMEM_LIMIT)


def _nt_dot(a, b):
    return lax.dot_general(a, b, (((1,), (1,)), ((), ())), preferred_element_type=F32)


def _tn_dot(a, b):
    return lax.dot_general(a, b, (((0,), (0,)), ((), ())), preferred_element_type=F32)


def _mod_spec(d, tiles_per_batch, n_lat_tiles, b):
    return pl.BlockSpec((None, 6, d), lambda i, *_: (jnp.where(i < n_lat_tiles, i // tiles_per_batch, b), 0, 0))


def _mod_kernel(s_ref, w_ref, b_ref, o_ref):
    s = s_ref[...]
    s = s * jax.nn.sigmoid(s)
    o_ref[...] = jnp.dot(s, w_ref[...], precision=HIGHEST, preferred_element_type=F32) + b_ref[...]


def _modulation(cond, ada_w, ada_b):
    depth, d, n = ada_w.shape
    tn = n // 4
    out = pl.pallas_call(
        _mod_kernel,
        grid=(depth, n // tn),
        in_specs=[pl.BlockSpec((8, d), lambda l, j: (0, 0)),
                  pl.BlockSpec((None, d, tn), lambda l, j: (l, 0, j)),
                  pl.BlockSpec((None, 1, tn), lambda l, j: (l, 0, j))],
        out_specs=pl.BlockSpec((None, 8, tn), lambda l, j: (l, 0, j)),
        out_shape=jax.ShapeDtypeStruct((depth, 8, n), F32),
        compiler_params=_cparams(("arbitrary", "arbitrary")),
    )(cond, ada_w, ada_b.reshape(depth, 1, n))
    return out.reshape(depth, 8, 6, d)


def _in_kernel(*refs, has_gates):
    if has_gates:
        x_ref, g_ref, mod_ref, w_ref, wg_ref, gb_ref, o_ref, go_ref, h_ref = refs
    else:
        x_ref, g_ref, mod_ref, w_ref, o_ref, h_ref = refs

    @pl.when(pl.program_id(1) == 0)
    def _():
        x = x_ref[...]
        y = x * lax.rsqrt(jnp.mean(x * x, axis=-1, keepdims=True) + EPS) * g_ref[...]
        hb = (y * (1.0 + mod_ref[1:2, :]) + mod_ref[0:1, :]).astype(BF16)
        h_ref[...] = hb
        if has_gates:
            go_ref[...] = jnp.dot(hb, wg_ref[...], preferred_element_type=F32) + gb_ref[...]

    o_ref[...] = jnp.dot(h_ref[...], w_ref[...], preferred_element_type=F32).astype(o_ref.dtype)


def _in_proj(x, g, mods, w, *, b, t, tn, gates=None):
    ntot, d = x.shape
    n = w.shape[1]
    tm = ROW_TILE
    has_gates = gates is not None
    in_specs = [pl.BlockSpec((tm, d), lambda i, j: (i, 0)),
                pl.BlockSpec((1, d), lambda i, j: (0, 0)),
                _mod_spec(d, t // tm, b * t // tm, b),
                pl.BlockSpec((d, tn), lambda i, j: (0, j))]
    args = [x, g.reshape(1, d), mods, w]
    out_specs = pl.BlockSpec((tm, tn), lambda i, j: (i, j))
    out_shape = jax.ShapeDtypeStruct((ntot, n), BF16)
    if has_gates:
        wg, gb = gates
        in_specs += [pl.BlockSpec((d, LANES), lambda i, j: (0, 0)),
                     pl.BlockSpec((1, LANES), lambda i, j: (0, 0))]
        args += [wg, gb]
        out_specs = [out_specs, pl.BlockSpec((tm, LANES), lambda i, j: (i, 0))]
        out_shape = [out_shape, jax.ShapeDtypeStruct((ntot, LANES), F32)]
    return pl.pallas_call(
        functools.partial(_in_kernel, has_gates=has_gates),
        grid=(ntot // tm, n // tn),
        in_specs=in_specs,
        out_specs=out_specs,
        out_shape=out_shape,
        scratch_shapes=[pltpu.VMEM((tm, d), BF16)],
        compiler_params=_cparams(("arbitrary", "arbitrary")),
    )(*args)


def _conv_kernel(xm_ref, xp_ref, xn_ref, w_ref, b_ref, o_ref, *, ncl, n_lat_chunks, ncol_q, k_scale):
    i = pl.program_id(0)
    j = pl.program_id(1)
    c = i % ncl
    is_lat = i < n_lat_chunks
    left_ok = jnp.logical_and(is_lat, c >= 1)
    right_ok = jnp.logical_and(is_lat, c < ncl - 1)
    xm = xm_ref[...].astype(F32)
    xp = jnp.where(left_ok, xp_ref[...].astype(F32), 0.0)[14:16, :]
    xn = jnp.where(right_ok, xn_ref[...].astype(F32), 0.0)[0:2, :]
    ext = jnp.concatenate([xp, xm, xn], axis=0)
    rows = xm.shape[0]
    w = w_ref[...]
    y = b_ref[...] + w[0:1, :] * ext[0:rows, :]
    for k in range(1, M_CONV_W):
        y = y + w[k:k + 1, :] * ext[k:k + rows, :]
    y = y * jax.nn.sigmoid(y)
    y = y * jnp.where(j >= ncol_q, k_scale, 1.0)
    o_ref[...] = y.astype(o_ref.dtype)


def _mlstm_conv(p, conv_w, conv_b, *, b, t, d):
    ntot = p.shape[0]
    tn = 512
    nrb = ntot // 16
    return pl.pallas_call(
        functools.partial(_conv_kernel, ncl=t // CHUNK, n_lat_chunks=b * t // CHUNK, ncol_q=d // tn,
                          k_scale=(d // M_HEADS) ** -0.5),
        grid=(ntot // CHUNK, 2 * d // tn),
        in_specs=[pl.BlockSpec((CHUNK, tn), lambda i, j: (i, j)),
                  pl.BlockSpec((16, tn), lambda i, j: (jnp.maximum(i * (CHUNK // 16) - 1, 0), j)),
                  pl.BlockSpec((16, tn), lambda i, j: (jnp.minimum((i + 1) * (CHUNK // 16), nrb - 1), j)),
                  pl.BlockSpec((M_CONV_W, tn), lambda i, j: (0, j)),
                  pl.BlockSpec((1, tn), lambda i, j: (0, j))],
        out_specs=pl.BlockSpec((CHUNK, tn), lambda i, j: (i, j)),
        out_shape=jax.ShapeDtypeStruct((ntot, 2 * d), BF16),
        compiler_params=_cparams(("arbitrary", "arbitrary")),
    )(p, p, p, conv_w, conv_b.reshape(1, 2 * d))


def _log_sigmoid(x):
    return jnp.minimum(x, 0.0) - jnp.log(1.0 + jnp.exp(-jnp.abs(x)))


def _mlstm_chunk(q, k, v, gcol, grow, igate, fgate, mask, mask_t, ct_ref, n_ref, m_ref, o_ref):
    i_row = grow[igate:igate + 1, :]
    i_col = gcol[:, igate:igate + 1]
    f_row = _log_sigmoid(grow[fgate:fgate + 1, :])
    f_col = _log_sigmoid(gcol[:, fgate:fgate + 1])
    cum_col = jnp.sum(jnp.where(mask, f_row, 0.0), axis=1, keepdims=True)
    cum_row = jnp.sum(jnp.where(mask_t, f_col, 0.0), axis=0, keepdims=True)
    tot = jnp.sum(f_col, axis=0, keepdims=True)
    m_prev = m_ref[...]
    dmat = jnp.where(mask, cum_col - cum_row + i_row, NEG)
    m_inter = cum_col + m_prev
    m_out = jnp.maximum(m_inter, jnp.max(dmat, axis=1, keepdims=True))
    w_intra = jnp.exp(dmat - m_out)
    w_inter = jnp.exp(m_inter - m_out)
    sc = _nt_dot(q, k) * w_intra
    ct = ct_ref[...]
    n_row = n_ref[...]
    qf = q.astype(F32)
    kf = k.astype(F32)
    vf = v.astype(F32)
    num = (w_inter * jnp.dot(q, ct.astype(BF16), preferred_element_type=F32)
           + jnp.dot(sc.astype(BF16), v, preferred_element_type=F32))
    den = w_inter * jnp.sum(qf * n_row, axis=1, keepdims=True) + jnp.sum(sc, axis=1, keepdims=True)
    o_ref[...] = (num / jnp.maximum(jnp.abs(den), jnp.exp(-m_out))).astype(o_ref.dtype)
    g_col = tot - cum_col + i_col
    m_next = jnp.maximum(tot + m_prev, jnp.max(g_col, axis=0, keepdims=True))
    decay = jnp.exp(tot + m_prev - m_next)
    wg = jnp.exp(g_col - m_next)
    ct_ref[...] = decay * ct + _tn_dot(k, (wg * vf).astype(BF16))
    n_ref[...] = decay * n_row + jnp.sum(wg * kf, axis=0, keepdims=True)
    m_ref[...] = m_next


def _scan_kernel(qf_ref, kf_ref, vf_ref, gcf_ref, grf_ref, qb_ref, kb_ref, vb_ref, gcb_ref, grb_ref,
                 of_ref, ob_ref, ctf_ref, nf_ref, mf_ref, ctb_ref, nb_ref, mb_ref):
    @pl.when(pl.program_id(2) == 0)
    def _():
        for r in (ctf_ref, nf_ref, mf_ref, ctb_ref, nb_ref, mb_ref):
            r[...] = jnp.zeros(r.shape, r.dtype)

    t_i = lax.broadcasted_iota(I32, (CHUNK, CHUNK), 0)
    s_i = lax.broadcasted_iota(I32, (CHUNK, CHUNK), 1)
    lower = s_i <= t_i
    upper = s_i >= t_i
    _mlstm_chunk(qf_ref[...], kf_ref[...], vf_ref[...], gcf_ref[...], grf_ref[...], 0, 2,
                 lower, upper, ctf_ref, nf_ref, mf_ref, of_ref)
    _mlstm_chunk(qb_ref[...], kb_ref[...], vb_ref[...], gcb_ref[...], grb_ref[...], 1, 3,
                 upper, lower, ctb_ref, nb_ref, mb_ref, ob_ref)


def _mlstm_scan(qk, p, gcol, grow, *, b, t, d):
    ntot = qk.shape[0]
    dh = d // M_HEADS
    ncl = t // CHUNK
    ctx0 = b * ncl

    def fwd(bi, s):
        return jnp.where(s == 0, ctx0 + bi, bi * ncl + s - 1)

    def bwd(bi, s):
        return jnp.where(s == 0, ctx0 + bi, bi * ncl + ncl - s)

    def specs(order):
        return [pl.BlockSpec((CHUNK, dh), lambda bi, h, s: (order(bi, s), h)),
                pl.BlockSpec((CHUNK, dh), lambda bi, h, s: (order(bi, s), M_HEADS + h)),
                pl.BlockSpec((CHUNK, dh), lambda bi, h, s: (order(bi, s), 2 * M_HEADS + h)),
                pl.BlockSpec((None, CHUNK, 4), lambda bi, h, s: (h, order(bi, s), 0)),
                pl.BlockSpec((None, 4, CHUNK), lambda bi, h, s: (h, 0, order(bi, s)))]

    def ospec(order):
        return pl.BlockSpec((CHUNK, dh), lambda bi, h, s: (order(bi, s), h))

    state = [pltpu.VMEM((dh, dh), F32), pltpu.VMEM((1, dh), F32), pltpu.VMEM((1, 1), F32)]
    return pl.pallas_call(
        _scan_kernel,
        grid=(b, M_HEADS, ncl + 1),
        in_specs=specs(fwd) + specs(bwd),
        out_specs=[ospec(fwd), ospec(bwd)],
        out_shape=[jax.ShapeDtypeStruct((ntot, d), F32)] * 2,
        scratch_shapes=state + state,
        compiler_params=_cparams(("arbitrary", "arbitrary", "arbitrary")),
    )(qk, qk, p, gcol, grow, qk, qk, p, gcol, grow)


def _qkprep_kernel(p_ref, gq_ref, gk_ref, cos_ref, sin_ref, o_ref, *, n_lat_chunks, nblk_q, q_scale):
    is_ctx = pl.program_id(0) >= n_lat_chunks
    cos = jnp.where(is_ctx, 1.0, cos_ref[...])
    sin = jnp.where(is_ctx, 0.0, sin_ref[...])
    lane = lax.broadcasted_iota(I32, (CHUNK, LANES), 1)
    first_half = (lane % 32) < 16
    r_i = lax.broadcasted_iota(I32, (LANES, LANES), 0)
    c_i = lax.broadcasted_iota(I32, (LANES, LANES), 1)
    group = jnp.where((r_i // D_HEAD_DIM) == (c_i // D_HEAD_DIM), 1.0, 0.0).astype(BF16)
    for blk in range(2 * nblk_q):
        x = p_ref[:, blk * LANES:(blk + 1) * LANES].astype(F32)
        ss = x * x
        ss_hi = ss.astype(BF16)
        ss_lo = (ss - ss_hi.astype(F32)).astype(BF16)
        ssum = (jnp.dot(ss_hi, group, preferred_element_type=F32)
                + jnp.dot(ss_lo, group, preferred_element_type=F32))
        g = gq_ref[...] if blk < nblk_q else gk_ref[...]
        xn = x * lax.rsqrt(ssum * (1.0 / D_HEAD_DIM) + EPS) * g
        partner = jnp.where(first_half, pltpu.roll(xn, LANES - 16, 1), pltpu.roll(xn, 16, 1))
        y = xn * cos + partner * sin
        if blk < nblk_q:
            y = y * q_scale
        o_ref[:, blk * LANES:(blk + 1) * LANES] = y.astype(o_ref.dtype)


def _diff_qkprep(p, q_g, k_g, cos_t, sin_t, *, b, t, d):
    ntot = p.shape[0]
    ncl = t // CHUNK
    g2 = lambda g: jnp.tile(g.astype(F32), 2).reshape(1, LANES)
    return pl.pallas_call(
        functools.partial(_qkprep_kernel, n_lat_chunks=b * ncl, nblk_q=d // LANES,
                          q_scale=D_HEAD_DIM ** -0.5 * math.log2(math.e)),
        grid=(ntot // CHUNK,),
        in_specs=[pl.BlockSpec((CHUNK, 2 * d), lambda i: (i, 0)),
                  pl.BlockSpec((1, LANES), lambda i: (0, 0)),
                  pl.BlockSpec((1, LANES), lambda i: (0, 0)),
                  pl.BlockSpec((CHUNK, LANES), lambda i: (i % ncl, 0)),
                  pl.BlockSpec((CHUNK, LANES), lambda i: (i % ncl, 0))],
        out_specs=pl.BlockSpec((CHUNK, 2 * d), lambda i: (i, 0)),
        out_shape=jax.ShapeDtypeStruct((ntot, 2 * d), BF16),
        compiler_params=_cparams(("arbitrary",)),
    )(p, g2(q_g), g2(k_g), cos_t, sin_t)


def _rope_tables(t):
    rows = t // GRID_W
    row = jnp.repeat(jnp.arange(rows), GRID_W).astype(F32)[:, None]
    col = jnp.tile(jnp.arange(GRID_W), rows).astype(F32)[:, None]
    nf = D_HEAD_DIM // 4
    inv = ROPE_BASE ** (-jnp.arange(nf, dtype=F32) / nf)
    ar, ac = row * inv, col * inv
    cos64 = jnp.concatenate([jnp.cos(ar), jnp.cos(ar), jnp.cos(ac), jnp.cos(ac)], axis=-1)
    sin64 = jnp.concatenate([-jnp.sin(ar), jnp.sin(ar), -jnp.sin(ac), jnp.sin(ac)], axis=-1)
    return jnp.tile(cos64, (1, 2)), jnp.tile(sin64, (1, 2))


def _attn_kernel(*refs, n_lat, out_scale):
    if n_lat:
        lam_ref, q_ref, kc_ref, vtc_ref, kl_ref, vtl_ref, sg_ref, o_ref, m1, l1, a1, m2, l2, a2 = refs[:14]
        bufs = refs[14:]
    else:
        lam_ref, q_ref, kc_ref, vtc_ref, sg_ref, o_ref, m1, l1, a1, m2, l2, a2 = refs[:12]
        bufs = refs[12:]
    sets = [tuple(bufs[i + 2 * k:i + 2 * k + 2] for k in range(4)) for i in range(0, len(bufs), 8)]
    q = q_ref[...]
    lane = lax.broadcasted_iota(I32, q.shape, 1)
    zero = jnp.zeros_like(q)
    qs = (jnp.where(lane < D_HEAD_DIM, q, zero), jnp.where(lane >= D_HEAD_DIM, q, zero))
    stats = ((m1, l1), (m2, l2))
    accs = (a1, a2)
    for (m_r, l_r), a_r in zip(stats, accs):
        m_r[...] = jnp.full(m_r.shape, NEG, F32)
        l_r[...] = jnp.zeros(l_r.shape, F32)
        a_r[...] = jnp.zeros(a_r.shape, F32)

    def scores(keys, s_bufs, mx_bufs):
        for qm, s_ref, mx_ref in zip(qs, s_bufs, mx_bufs):
            st = _nt_dot(keys, qm)
            s_ref[...] = st
            mx = st[0:8]
            for r in range(8, st.shape[0], 8):
                mx = jnp.maximum(mx, st[r:r + 8])
            mx_ref[...] = mx

    def softmax(s_bufs, mx_bufs, p_bufs, al_bufs):
        for s_ref, mx_ref, p_ref, al_ref, (m_r, l_r) in zip(s_bufs, mx_bufs, p_bufs, al_bufs, stats):
            m_old = m_r[...]
            m_new = jnp.maximum(m_old, jnp.max(mx_ref[...], axis=0, keepdims=True))
            alpha = jnp.exp2(m_old - m_new)
            m_r[...] = m_new
            al_ref[...] = alpha
            part = jnp.zeros(mx_ref.shape, F32)
            for r0 in range(0, s_ref.shape[0], SOFTMAX_ROWS):
                p = jnp.exp2(s_ref[r0:r0 + SOFTMAX_ROWS, :] - m_new)
                for r in range(0, SOFTMAX_ROWS, 8):
                    part = part + p[r:r + 8]
                p_ref[r0:r0 + SOFTMAX_ROWS, :] = p.astype(BF16)
            l_r[...] = alpha * l_r[...] + jnp.sum(part, axis=0, keepdims=True)

    def weigh(vt, p_bufs, al_bufs):
        for p_ref, al_ref, a_r in zip(p_bufs, al_bufs, accs):
            a_r[...] = al_ref[...] * a_r[...] + jnp.dot(vt, p_ref[...], preferred_element_type=F32)

    s_c, mx_c, p_c, al_c = sets[0]
    scores(kc_ref[...], s_c, mx_c)
    softmax(s_c, mx_c, p_c, al_c)
    weigh(vtc_ref[...], p_c, al_c)

    if n_lat:
        def keys(j):
            return kl_ref[pl.ds(pl.multiple_of(j * KV_CHUNK, KV_CHUNK), KV_CHUNK), :]

        (s_a, mx_a, p_a, al_a), (s_b, mx_b, p_b, al_b) = sets[1:3]
        scores(keys(0), s_a, mx_a)
        softmax(s_a, mx_a, p_a, al_a)
        scores(keys(1), s_b, mx_b)

        def body(jj, carry):
            j = 2 * jj
            softmax(s_b, mx_b, p_b, al_b)
            scores(keys(j + 2), s_a, mx_a)
            weigh(vtl_ref[j], p_a, al_a)
            softmax(s_a, mx_a, p_a, al_a)
            scores(keys(j + 3), s_b, mx_b)
            weigh(vtl_ref[j + 1], p_b, al_b)
            return carry

        lax.fori_loop(0, n_lat // 2 - 1, body, 0)
        softmax(s_b, mx_b, p_b, al_b)
        weigh(vtl_ref[n_lat - 2], p_a, al_a)
        weigh(vtl_ref[n_lat - 1], p_b, al_b)
    o = a1[...] / l1[...] - lam_ref[0] * (a2[...] / l2[...])
    o = o * lax.rsqrt(jnp.mean(o * o, axis=0, keepdims=True) + EPS) * sg_ref[...] * out_scale
    o_ref[...] = o.T.astype(o_ref.dtype)


def _diff_attention(qk, vt_lat, vt_ctx, lam, sub_g, *, b, t, d, lambda_init, latent):
    nh = d // LANES
    ctx0 = b * t // CHUNK
    n_lat = t // KV_CHUNK if latent else 0
    tq = Q_TILE if latent else CHUNK
    nq = t // tq if latent else 1
    q_row = (lambda bi, c: bi * nq + c) if latent else (lambda bi, c: ctx0 + bi)
    in_specs = [pl.BlockSpec((tq, LANES), lambda bi, h, c, lam_r: (q_row(bi, c), h)),
                pl.BlockSpec((CHUNK, LANES), lambda bi, h, c, lam_r: (ctx0 + bi, nh + h)),
                pl.BlockSpec((None, LANES, CHUNK), lambda bi, h, c, lam_r: (bi, h, 0))]
    args = [qk, qk, vt_ctx]
    if latent:
        in_specs += [pl.BlockSpec((t, LANES), lambda bi, h, c, lam_r: (bi, nh + h)),
                     pl.BlockSpec((n_lat, LANES, KV_CHUNK), lambda bi, h, c, lam_r: (bi, h, 0))]
        args += [qk, vt_lat]
    in_specs.append(pl.BlockSpec((LANES, 1), lambda bi, h, c, lam_r: (0, 0)))
    args.append(sub_g.reshape(LANES, 1).astype(F32))
    stat = pltpu.VMEM((1, tq), F32)
    acc = pltpu.VMEM((LANES, tq), F32)

    def buffer_set(tk):
        return ([pltpu.VMEM((tk, tq), F32)] * 2 + [pltpu.VMEM((8, tq), F32)] * 2
                + [pltpu.VMEM((tk, tq), BF16)] * 2 + [stat] * 2)

    grid_spec = pltpu.PrefetchScalarGridSpec(
        num_scalar_prefetch=1,
        grid=(b, nh, nq),
        in_specs=in_specs,
        out_specs=pl.BlockSpec((tq, LANES), lambda bi, h, c, lam_r: (bi * nq + c, h)),
        scratch_shapes=[stat, stat, acc, stat, stat, acc] + buffer_set(CHUNK) + (buffer_set(KV_CHUNK) * 4 if latent else []))
    return pl.pallas_call(
        functools.partial(_attn_kernel, n_lat=n_lat, out_scale=1.0 - lambda_init),
        grid_spec=grid_spec,
        out_shape=jax.ShapeDtypeStruct((b * nq * tq, d), BF16),
        compiler_params=_cparams(("arbitrary", "arbitrary", "arbitrary")),
    )(lam, *args)


def _out_kernel(*refs, mlstm):
    if mlstm:
        (rf_ref, rb_ref, op_ref, og_ref, w_ref, x_ref, g2_ref, mod_ref, rwh_ref, rwl_ref, rb2_ref,
         xo_ref, h_ref, lg_ref) = refs
        r = rf_ref[...] + rb_ref[...]
        dh = r.shape[1] // M_HEADS
        parts = []
        for hh in range(M_HEADS):
            rr = r[:, hh * dh:(hh + 1) * dh]
            parts.append(rr * lax.rsqrt(jnp.mean(rr * rr, axis=-1, keepdims=True) + EPS))
        rn = jnp.concatenate(parts, axis=1) * og_ref[...]
        u = (jax.nn.sigmoid(op_ref[...].astype(F32)) * rn).astype(BF16)
    else:
        (u_ref, w_ref, x_ref, g2_ref, mod_ref, rwh_ref, rwl_ref, rb2_ref, xo_ref, h_ref, lg_ref) = refs
        u = u_ref[...]
    y = jnp.dot(u, w_ref[...], preferred_element_type=F32)
    x = x_ref[...] + mod_ref[2:3, :] * y
    xo_ref[...] = x
    hn = x * lax.rsqrt(jnp.mean(x * x, axis=-1, keepdims=True) + EPS) * g2_ref[...]
    h = hn * (1.0 + mod_ref[4:5, :]) + mod_ref[3:4, :]
    h_ref[...] = h
    h_hi = h.astype(BF16)
    h_lo = (h - h_hi.astype(F32)).astype(BF16)
    lg_ref[...] = (jnp.dot(h_hi, rwh_ref[...], preferred_element_type=F32)
                   + jnp.dot(h_lo, rwh_ref[...], preferred_element_type=F32)
                   + jnp.dot(h_hi, rwl_ref[...], preferred_element_type=F32) + rb2_ref[...])


def _out_proj(u_args, w_out, x, g2, mods, rw_hi, rw_lo, r_bias, *, b, t, mlstm):
    ntot, d = x.shape
    tm = ROW_TILE
    row = lambda i: (i, 0)
    const = lambda i: (0, 0)
    if mlstm:
        rf, rb, p, out_g = u_args
        u_specs = [pl.BlockSpec((tm, d), row), pl.BlockSpec((tm, d), row),
                   pl.BlockSpec((tm, d), lambda i: (i, 3)), pl.BlockSpec((1, d), const)]
        u_in = [rf, rb, p, out_g.reshape(1, d)]
    else:
        u_specs = [pl.BlockSpec((tm, d), row)]
        u_in = list(u_args)
    in_specs = u_specs + [pl.BlockSpec((d, d), const),
                          pl.BlockSpec((tm, d), row),
                          pl.BlockSpec((1, d), const),
                          _mod_spec(d, t // tm, b * t // tm, b),
                          pl.BlockSpec((d, LANES), const),
                          pl.BlockSpec((d, LANES), const),
                          pl.BlockSpec((1, LANES), const)]
    return pl.pallas_call(
        functools.partial(_out_kernel, mlstm=mlstm),
        grid=(ntot // tm,),
        in_specs=in_specs,
        out_specs=[pl.BlockSpec((tm, d), row), pl.BlockSpec((tm, d), row), pl.BlockSpec((tm, LANES), row)],
        out_shape=[jax.ShapeDtypeStruct((ntot, d), F32), jax.ShapeDtypeStruct((ntot, d), F32),
                   jax.ShapeDtypeStruct((ntot, LANES), F32)],
        compiler_params=_cparams(("arbitrary",)),
    )(*u_in, w_out, x, g2.reshape(1, d), mods, rw_hi, rw_lo, r_bias)


def _lane_pick(lane, pairs):
    out = jnp.zeros(lane.shape, F32)
    for idx, val in pairs:
        out = jnp.where(lane == idx, val, out)
    return out


def _route_kernel(lg_ref, rt_ref, cnt_ref):
    @pl.when(pl.program_id(0) == 0)
    def _():
        cnt_ref[...] = jnp.zeros(cnt_ref.shape, F32)

    lg = lg_ref[...]
    lane = lax.broadcasted_iota(I32, lg.shape, 1)
    lane_f = lane.astype(F32)
    big = 1e9
    is_g = lane < N_GROUPS
    gl = jnp.where(is_g, lg, NEG)
    gmax = jnp.max(gl, axis=1, keepdims=True)
    gidx = jnp.min(jnp.where(gl == gmax, lane_f, big), axis=1, keepdims=True)
    gsum = jnp.sum(jnp.where(is_g, jnp.exp(gl - gmax), 0.0), axis=1, keepdims=True)
    gp_top = 1.0 / gsum
    e_lane = lane_f - float(N_GROUPS)
    lo = gidx * float(EXPERTS_PER_GROUP)
    in_grp = jnp.where(e_lane >= lo, jnp.where(e_lane < lo + float(EXPERTS_PER_GROUP), 1.0, 0.0), 0.0) > 0.5
    el = jnp.where(in_grp, lg, NEG)
    v0 = jnp.max(el, axis=1, keepdims=True)
    i0 = jnp.min(jnp.where(el == v0, lane_f, big), axis=1, keepdims=True)
    el2 = jnp.where(lane_f == i0, NEG, el)
    v1 = jnp.max(el2, axis=1, keepdims=True)
    i1 = jnp.min(jnp.where(el2 == v1, lane_f, big), axis=1, keepdims=True)
    e1 = jnp.exp(v1 - v0)
    g0 = gp_top / (1.0 + e1)
    g1 = gp_top * e1 / (1.0 + e1)
    eid0 = i0 - float(N_GROUPS)
    eid1 = i1 - float(N_GROUPS)
    rt_ref[...] = _lane_pick(lane, ((0, eid0), (1, eid1), (2, g0), (3, g1)))
    hot = jnp.where(lane_f == eid0, 1.0, 0.0) + jnp.where(lane_f == eid1, 1.0, 0.0)
    cnt_ref[0:1, :] += jnp.sum(hot, axis=0, keepdims=True)


def _route(logits):
    ntot = logits.shape[0]
    return pl.pallas_call(
        _route_kernel,
        grid=(ntot // CHUNK,),
        in_specs=[pl.BlockSpec((CHUNK, LANES), lambda i: (i, 0))],
        out_specs=[pl.BlockSpec((CHUNK, LANES), lambda i: (i, 0)), pl.BlockSpec((8, LANES), lambda i: (0, 0))],
        out_shape=[jax.ShapeDtypeStruct((ntot, LANES), F32), jax.ShapeDtypeStruct((8, LANES), F32)],
        compiler_params=_cparams(("arbitrary",)),
    )(logits)


def _slot_kernel(rt_ref, cnt_ref, o_ref, run_ref):
    @pl.when(pl.program_id(0) == 0)
    def _():
        cnt = cnt_ref[0:1, :].astype(I32)
        padded = (((cnt + (MOE_BLOCK - 1)) // MOE_BLOCK) * MOE_BLOCK).astype(F32)
        r_i = lax.broadcasted_iota(I32, (LANES, LANES), 0)
        c_i = lax.broadcasted_iota(I32, (LANES, LANES), 1)
        col = jnp.sum(jnp.where(r_i == c_i, jnp.broadcast_to(padded, (LANES, LANES)), 0.0), axis=1, keepdims=True)
        run_ref[...] = jnp.sum(jnp.where(r_i < c_i, col, 0.0), axis=0, keepdims=True)

    rt = rt_ref[...]
    lane = lax.broadcasted_iota(I32, rt.shape, 1)
    lane_f = lane.astype(F32)
    hot0 = lane_f == rt[:, 0:1]
    hot1 = lane_f == rt[:, 1:2]
    both = jnp.where(hot0, 1.0, 0.0) + jnp.where(hot1, 1.0, 0.0)
    t_i = lax.broadcasted_iota(I32, (CHUNK, CHUNK), 0)
    s_i = lax.broadcasted_iota(I32, (CHUNK, CHUNK), 1)
    before = jnp.where(s_i < t_i, 1.0, 0.0).astype(BF16)
    base = run_ref[...] + jnp.dot(before, both.astype(BF16), preferred_element_type=F32)
    d0 = jnp.sum(jnp.where(hot0, base, 0.0), axis=1, keepdims=True)
    d1 = jnp.sum(jnp.where(hot1, base, 0.0), axis=1, keepdims=True)
    o_ref[...] = _lane_pick(lane, ((0, d0), (1, d1))).astype(I32)
    run_ref[...] += jnp.sum(both, axis=0, keepdims=True)


def _slots(route, counts):
    ntot = route.shape[0]
    return pl.pallas_call(
        _slot_kernel,
        grid=(ntot // CHUNK,),
        in_specs=[pl.BlockSpec((CHUNK, LANES), lambda i: (i, 0)), pl.BlockSpec((8, LANES), lambda i: (0, 0))],
        out_specs=pl.BlockSpec((CHUNK, LANES), lambda i: (i, 0)),
        out_shape=jax.ShapeDtypeStruct((ntot, LANES), I32),
        scratch_shapes=[pltpu.VMEM((1, LANES), F32)],
        compiler_params=_cparams(("arbitrary",)),
    )(route, counts)


def _dispatch_kernel(dest_ref, h_ref, xs_in_ref, xs_ref, sem):
    del xs_in_ref

    def row_copy(r, k):
        return pltpu.make_async_copy(h_ref.at[pl.ds(r, 1), :], xs_ref.at[pl.ds(dest_ref[0, 2 * r + k], 1), :], sem)

    def start(r, carry):
        row_copy(r, 0).start(priority=0)
        row_copy(r, 1).start(priority=1)
        return carry

    def wait(r, carry):
        row_copy(r, 0).wait()
        row_copy(r, 1).wait()
        return carry

    lax.fori_loop(0, CHUNK, start, 0, unroll=DMA_UNROLL)
    lax.fori_loop(0, CHUNK, wait, 0, unroll=DMA_UNROLL)


def _dispatch(dest, h, buf):
    ntot, d = h.shape
    cap = buf.shape[0]
    return pl.pallas_call(
        _dispatch_kernel,
        grid=(ntot // CHUNK,),
        in_specs=[pl.BlockSpec((None, 1, 2 * CHUNK), lambda i: (i, 0, 0), memory_space=pltpu.SMEM),
                  pl.BlockSpec((CHUNK, d), lambda i: (i, 0)),
                  pl.BlockSpec(memory_space=pl.ANY)],
        out_specs=pl.BlockSpec(memory_space=pl.ANY),
        out_shape=jax.ShapeDtypeStruct((cap, d), F32),
        scratch_shapes=[pltpu.SemaphoreType.DMA],
        input_output_aliases={2: 0},
        compiler_params=_cparams(("arbitrary",)),
    )(dest, h, buf)


def _expert_kernel(be_ref, nu_ref, xs_ref, wgu_ref, wdn_ref, ys_ref, wgu_bf, wdn_bf):
    j = pl.program_id(0)
    changed = jnp.logical_or(j == 0, be_ref[j] != be_ref[jnp.maximum(j - 1, 0)])

    @pl.when(changed)
    def _():
        wgu_bf[...] = wgu_ref[...].astype(BF16)
        wdn_bf[...] = wdn_ref[...].astype(BF16)

    @pl.when(j < nu_ref[0])
    def _():
        gu = jnp.dot(xs_ref[...].astype(BF16), wgu_bf[...], preferred_element_type=F32)
        g = gu[:, :D_EXPERT]
        a = (g * jax.nn.sigmoid(g) * gu[:, D_EXPERT:]).astype(BF16)
        ys_ref[...] = jnp.dot(a, wdn_bf[...], preferred_element_type=F32)

    @pl.when(j >= nu_ref[0])
    def _():
        ys_ref[...] = jnp.zeros(ys_ref.shape, ys_ref.dtype)


def _experts(blk_e, n_used, xs, w_gu, w_dn, layer):
    cap, d = xs.shape
    row = lambda j, be, nu: (jnp.minimum(j, nu[0] - 1), 0)
    grid_spec = pltpu.PrefetchScalarGridSpec(
        num_scalar_prefetch=2,
        grid=(cap // MOE_BLOCK,),
        in_specs=[pl.BlockSpec((MOE_BLOCK, d), row),
                  pl.BlockSpec((None, None, d, 2 * D_EXPERT), lambda j, be, nu: (layer, be[j], 0, 0)),
                  pl.BlockSpec((None, None, D_EXPERT, d), lambda j, be, nu: (layer, be[j], 0, 0))],
        out_specs=pl.BlockSpec((MOE_BLOCK, d), lambda j, be, nu: (j, 0)),
        scratch_shapes=[pltpu.VMEM((d, 2 * D_EXPERT), BF16), pltpu.VMEM((D_EXPERT, d), BF16)])
    return pl.pallas_call(
        _expert_kernel,
        grid_spec=grid_spec,
        out_shape=jax.ShapeDtypeStruct((cap, d), F32),
        compiler_params=_cparams(("arbitrary",)),
    )(blk_e, n_used, xs, w_gu, w_dn)


def _combine_kernel(dest_ref, rt_ref, x_ref, mod_ref, ys_ref, xo_ref, buf, sem):
    def row_copy(r, k):
        return pltpu.make_async_copy(ys_ref.at[pl.ds(dest_ref[0, 2 * r + k], 1), :],
                                     buf.at[pl.ds(k * CHUNK + r, 1), :], sem)

    def start(r, carry):
        row_copy(r, 0).start(priority=0)
        row_copy(r, 1).start(priority=1)
        return carry

    def wait(r, carry):
        row_copy(r, 0).wait()
        row_copy(r, 1).wait()
        return carry

    lax.fori_loop(0, CHUNK, start, 0, unroll=DMA_UNROLL)
    lax.fori_loop(0, CHUNK, wait, 0, unroll=DMA_UNROLL)
    rt = rt_ref[...]
    y = rt[:, 2:3] * buf[0:CHUNK, :] + rt[:, 3:4] * buf[CHUNK:2 * CHUNK, :]
    xo_ref[...] = x_ref[...] + mod_ref[5:6, :] * y


def _combine(dest, route, x, mods, ys, *, b, t, n_rows):
    d = x.shape[1]
    ncl = t // CHUNK
    return pl.pallas_call(
        _combine_kernel,
        grid=(n_rows // CHUNK,),
        in_specs=[pl.BlockSpec((None, 1, 2 * CHUNK), lambda i: (i, 0, 0), memory_space=pltpu.SMEM),
                  pl.BlockSpec((CHUNK, LANES), lambda i: (i, 0)),
                  pl.BlockSpec((CHUNK, d), lambda i: (i, 0)),
                  _mod_spec(d, ncl, b * ncl, b),
                  pl.BlockSpec(memory_space=pl.ANY)],
        out_specs=pl.BlockSpec((CHUNK, d), lambda i: (i, 0)),
        out_shape=jax.ShapeDtypeStruct((n_rows, d), F32),
        scratch_shapes=[pltpu.VMEM((2 * CHUNK, d), F32), pltpu.SemaphoreType.DMA],
        compiler_params=_cparams(("arbitrary",)),
    )(dest, route, x, mods, ys)


def _moe_capacity(ntot):
    return -(-2 * ntot // MOE_BLOCK) * MOE_BLOCK + N_EXPERTS * MOE_BLOCK


def _moe(x, h, logits, mods, w_gu, w_dn, layer, slot_buf, *, b, t, n_rows):
    ntot = x.shape[0]
    route, counts = _route(logits)
    slot = _slots(route, counts)
    dest = slot[:, :2].reshape(ntot // CHUNK, 1, 2 * CHUNK)
    nblk = slot_buf.shape[0] // MOE_BLOCK
    cnt = counts[0, :N_EXPERTS].astype(I32)
    pad_end = jnp.cumsum((cnt + MOE_BLOCK - 1) // MOE_BLOCK * MOE_BLOCK)
    n_used = pad_end[N_EXPERTS - 1:] // MOE_BLOCK
    blk_start = jnp.minimum(jnp.arange(nblk, dtype=I32), n_used - 1) * MOE_BLOCK
    blk_e = jnp.minimum(jnp.sum((pad_end[None, :] <= blk_start[:, None]).astype(I32), axis=1), N_EXPERTS - 1)
    xs = _dispatch(dest, h, slot_buf)
    ys = _experts(blk_e.astype(I32), n_used.astype(I32), xs, w_gu, w_dn, layer)
    return _combine(dest, route, x, mods, ys, b=b, t=t, n_rows=n_rows), xs


def kernel(x, c, ctx, c_ctx, ada_w, ada_b, norm1_g, norm2_g, m_w_in, m_conv_w, m_conv_b, m_gate_b, m_out_g,
           m_w_out, d_w_in, d_q_g, d_k_g, d_lq1, d_lk1, d_lq2, d_lk2, d_sub_g, d_w_out, r_grp_w, r_grp_b,
           r_exp_w, r_exp_b, e_w_gu, e_w_dn):
    b, t, d = x.shape
    cl = ctx.shape[1]
    depth = ada_w.shape[0]
    assert cl == CHUNK and d == 8 * LANES and b + 1 <= 8
    assert t % Q_TILE == 0 and t % (4 * KV_CHUNK) == 0 and t % ROW_TILE == 0 and (b * cl) % ROW_TILE == 0
    n_lat = b * t
    ntot = n_lat + b * cl

    xs = jnp.concatenate([x.reshape(n_lat, d), ctx.reshape(b * cl, d)], axis=0)
    cond = jnp.zeros((8, d), F32).at[:b].set(c).at[b].set(c_ctx)
    mods_all = _modulation(cond, ada_w, ada_b)
    cos_t, sin_t = _rope_tables(t)
    slot_buf = jnp.zeros((_moe_capacity(ntot), d), F32)

    n_route = N_GROUPS + N_EXPERTS
    for i in range(depth):
        jm = i // 2
        mods = mods_all[i]
        rw = jnp.zeros((d, LANES), F32).at[:, :N_GROUPS].set(r_grp_w[i]).at[:, N_GROUPS:n_route].set(r_exp_w[i])
        rw_hi = rw.astype(BF16)
        rw_lo = (rw - rw_hi.astype(F32)).astype(BF16)
        r_bias = jnp.zeros((1, LANES), F32).at[0, :N_GROUPS].set(r_grp_b[i]).at[0, N_GROUPS:n_route].set(r_exp_b[i])
        if i % 2 == 0:
            w_in = m_w_in[jm]
            wg = jnp.zeros((d, LANES), F32).at[:, :4 * M_HEADS].set(w_in[:, 4 * d:]).astype(BF16)
            gb = jnp.zeros((1, LANES), F32).at[0, :4 * M_HEADS].set(m_gate_b[jm])
            p, gates = _in_proj(xs, norm1_g[i], mods, w_in[:, :4 * d].astype(BF16), b=b, t=t, tn=1024,
                                gates=(wg, gb))
            qk = _mlstm_conv(p, m_conv_w[jm], m_conv_b[jm], b=b, t=t, d=d)
            g4 = gates[:, :4 * M_HEADS].reshape(ntot, 4, M_HEADS)
            rf, rb = _mlstm_scan(qk, p, g4.transpose(2, 0, 1), g4.transpose(2, 1, 0), b=b, t=t, d=d)
            xs, h, logits = _out_proj((rf, rb, p, m_out_g[jm]), m_w_out[jm].astype(BF16), xs, norm2_g[i], mods,
                                      rw_hi, rw_lo, r_bias, b=b, t=t, mlstm=True)
        else:
            lambda_init = 0.8 - 0.6 * math.exp(-0.3 * i)
            p = _in_proj(xs, norm1_g[i], mods, d_w_in[jm].astype(BF16), b=b, t=t, tn=1024)
            qk = _diff_qkprep(p, d_q_g[jm], d_k_g[jm], cos_t, sin_t, b=b, t=t, d=d)
            lam = (jnp.exp(jnp.sum(d_lq1[jm] * d_lk1[jm])) - jnp.exp(jnp.sum(d_lq2[jm] * d_lk2[jm]))
                   + lambda_init).reshape(1).astype(F32)
            v = p[:, 2 * d:]
            vt_lat = v[:n_lat].reshape(n_lat // KV_CHUNK, KV_CHUNK, d).transpose(0, 2, 1)
            vt_ctx = v[n_lat:].reshape(b, cl, d).transpose(0, 2, 1)
            attn = functools.partial(_diff_attention, qk, vt_lat, vt_ctx, lam, d_sub_g[jm], b=b, t=t, d=d,
                                     lambda_init=lambda_init)
            o = jnp.concatenate([attn(latent=True), attn(latent=False)], axis=0)
            xs, h, logits = _out_proj((o,), d_w_out[jm].astype(BF16), xs, norm2_g[i], mods,
                                      rw_hi, rw_lo, r_bias, b=b, t=t, mlstm=False)
        last = i == depth - 1
        xs, slot_buf = _moe(xs, h, logits, mods, e_w_gu, e_w_dn, i, slot_buf, b=b, t=t,
                            n_rows=n_lat if last else ntot)
    return xs.reshape(b, t, d)
```

```python
import functools
import math

import jax
import jax.numpy as jnp
from jax import lax
from jax.experimental import pallas as pl
from jax.experimental.pallas import tpu as pltpu

F32 = jnp.float32
BF16 = jnp.bfloat16
I32 = jnp.int32
HIGHEST = lax.Precision.HIGHEST

EPS = 1e-6
GRID_W = 64
ROPE_BASE = 10000.0
M_HEADS = 4
M_CONV_W = 5
D_HEADS = 8
D_HEAD_DIM = 64
N_GROUPS = 4
EXPERTS_PER_GROUP = 8
N_EXPERTS = 32
D_EXPERT = 512

LANES = 128
CHUNK = 256
ROW_TILE = 512
Q_TILE = 512
KV_CHUNK = 1024
SOFTMAX_ROWS = 64
MOE_BLOCK = 256
DMA_UNROLL = 8
NEG = -1e30
VMEM_LIMIT = 56 * 1024 * 1024


def _cparams(sem):
    return pltpu.CompilerParams(dimension_semantics=sem, vmem_limit_bytes=VMEM_LIMIT)


def _nt_dot(a, b):
    return lax.dot_general(a, b, (((1,), (1,)), ((), ())), preferred_element_type=F32)


def _tn_dot(a, b):
    return lax.dot_general(a, b, (((0,), (0,)), ((), ())), preferred_element_type=F32)


def _mod_spec(d, tiles_per_batch, n_lat_tiles, b):
    return pl.BlockSpec((None, 6, d), lambda i, *_: (jnp.where(i < n_lat_tiles, i // tiles_per_batch, b), 0, 0))


def _mod_kernel(s_ref, w_ref, b_ref, o_ref):
    s = s_ref[...]
    s = s * jax.nn.sigmoid(s)
    o_ref[...] = jnp.dot(s, w_ref[...], precision=HIGHEST, preferred_element_type=F32) + b_ref[...]


def _modulation(cond, ada_w, ada_b):
    depth, d, n = ada_w.shape
    tn = n // 4
    out = pl.pallas_call(
        _mod_kernel,
        grid=(depth, n // tn),
        in_specs=[pl.BlockSpec((8, d), lambda l, j: (0, 0)),
                  pl.BlockSpec((None, d, tn), lambda l, j: (l, 0, j)),
                  pl.BlockSpec((None, 1, tn), lambda l, j: (l, 0, j))],
        out_specs=pl.BlockSpec((None, 8, tn), lambda l, j: (l, 0, j)),
        out_shape=jax.ShapeDtypeStruct((depth, 8, n), F32),
        compiler_params=_cparams(("arbitrary", "arbitrary")),
    )(cond, ada_w, ada_b.reshape(depth, 1, n))
    return out.reshape(depth, 8, 6, d)


def _in_kernel(*refs, has_gates):
    if has_gates:
        x_ref, g_ref, mod_ref, w_ref, wg_ref, gb_ref, o_ref, go_ref, h_ref = refs
    else:
        x_ref, g_ref, mod_ref, w_ref, o_ref, h_ref = refs

    @pl.when(pl.program_id(1) == 0)
    def _():
        x = x_ref[...]
        y = x * lax.rsqrt(jnp.mean(x * x, axis=-1, keepdims=True) + EPS) * g_ref[...]
        hb = (y * (1.0 + mod_ref[1:2, :]) + mod_ref[0:1, :]).astype(BF16)
        h_ref[...] = hb
        if has_gates:
            go_ref[...] = jnp.dot(hb, wg_ref[...], preferred_element_type=F32) + gb_ref[...]

    o_ref[...] = jnp.dot(h_ref[...], w_ref[...], preferred_element_type=F32).astype(o_ref.dtype)


def _in_proj(x, g, mods, w, *, b, t, tn, gates=None):
    ntot, d = x.shape
    n = w.shape[1]
    tm = ROW_TILE
    has_gates = gates is not None
    in_specs = [pl.BlockSpec((tm, d), lambda i, j: (i, 0)),
                pl.BlockSpec((1, d), lambda i, j: (0, 0)),
                _mod_spec(d, t // tm, b * t // tm, b),
                pl.BlockSpec((d, tn), lambda i, j: (0, j))]
    args = [x, g.reshape(1, d), mods, w]
    out_specs = pl.BlockSpec((tm, tn), lambda i, j: (i, j))
    out_shape = jax.ShapeDtypeStruct((ntot, n), BF16)
    if has_gates:
        wg, gb = gates
        in_specs += [pl.BlockSpec((d, LANES), lambda i, j: (0, 0)),
                     pl.BlockSpec((1, LANES), lambda i, j: (0, 0))]
        args += [wg, gb]
        out_specs = [out_specs, pl.BlockSpec((tm, LANES), lambda i, j: (i, 0))]
        out_shape = [out_shape, jax.ShapeDtypeStruct((ntot, LANES), F32)]
    return pl.pallas_call(
        functools.partial(_in_kernel, has_gates=has_gates),
        grid=(ntot // tm, n // tn),
        in_specs=in_specs,
        out_specs=out_specs,
        out_shape=out_shape,
        scratch_shapes=[pltpu.VMEM((tm, d), BF16)],
        compiler_params=_cparams(("arbitrary", "arbitrary")),
    )(*args)


def _conv_kernel(xm_ref, xp_ref, xn_ref, w_ref, b_ref, o_ref, *, ncl, n_lat_chunks, ncol_q, k_scale):
    i = pl.program_id(0)
    j = pl.program_id(1)
    c = i % ncl
    is_lat = i < n_lat_chunks
    left_ok = jnp.logical_and(is_lat, c >= 1)
    right_ok = jnp.logical_and(is_lat, c < ncl - 1)
    xm = xm_ref[...].astype(F32)
    xp = jnp.where(left_ok, xp_ref[...].astype(F32), 0.0)[14:16, :]
    xn = jnp.where(right_ok, xn_ref[...].astype(F32), 0.0)[0:2, :]
    ext = jnp.concatenate([xp, xm, xn], axis=0)
    rows = xm.shape[0]
    w = w_ref[...]
    y = b_ref[...] + w[0:1, :] * ext[0:rows, :]
    for k in range(1, M_CONV_W):
        y = y + w[k:k + 1, :] * ext[k:k + rows, :]
    y = y * jax.nn.sigmoid(y)
    y = y * jnp.where(j >= ncol_q, k_scale, 1.0)
    o_ref[...] = y.astype(o_ref.dtype)


def _mlstm_conv(p, conv_w, conv_b, *, b, t, d):
    ntot = p.shape[0]
    tn = 512
    nrb = ntot // 16
    return pl.pallas_call(
        functools.partial(_conv_kernel, ncl=t // CHUNK, n_lat_chunks=b * t // CHUNK, ncol_q=d // tn,
                          k_scale=(d // M_HEADS) ** -0.5),
        grid=(ntot // CHUNK, 2 * d // tn),
        in_specs=[pl.BlockSpec((CHUNK, tn), lambda i, j: (i, j)),
                  pl.BlockSpec((16, tn), lambda i, j: (jnp.maximum(i * (CHUNK // 16) - 1, 0), j)),
                  pl.BlockSpec((16, tn), lambda i, j: (jnp.minimum((i + 1) * (CHUNK // 16), nrb - 1), j)),
                  pl.BlockSpec((M_CONV_W, tn), lambda i, j: (0, j)),
                  pl.BlockSpec((1, tn), lambda i, j: (0, j))],
        out_specs=pl.BlockSpec((CHUNK, tn), lambda i, j: (i, j)),
        out_shape=jax.ShapeDtypeStruct((ntot, 2 * d), BF16),
        compiler_params=_cparams(("arbitrary", "arbitrary")),
    )(p, p, p, conv_w, conv_b.reshape(1, 2 * d))


def _log_sigmoid(x):
    return jnp.minimum(x, 0.0) - jnp.log(1.0 + jnp.exp(-jnp.abs(x)))


def _mlstm_chunk(q, k, v, gcol, grow, igate, fgate, mask, mask_t, ct_ref, n_ref, m_ref, o_ref):
    i_row = grow[igate:igate + 1, :]
    i_col = gcol[:, igate:igate + 1]
    f_row = _log_sigmoid(grow[fgate:fgate + 1, :])
    f_col = _log_sigmoid(gcol[:, fgate:fgate + 1])
    cum_col = jnp.sum(jnp.where(mask, f_row, 0.0), axis=1, keepdims=True)
    cum_row = jnp.sum(jnp.where(mask_t, f_col, 0.0), axis=0, keepdims=True)
    tot = jnp.sum(f_col, axis=0, keepdims=True)
    m_prev = m_ref[...]
    dmat = jnp.where(mask, cum_col - cum_row + i_row, NEG)
    m_inter = cum_col + m_prev
    m_out = jnp.maximum(m_inter, jnp.max(dmat, axis=1, keepdims=True))
    w_intra = jnp.exp(dmat - m_out)
    w_inter = jnp.exp(m_inter - m_out)
    sc = _nt_dot(q, k) * w_intra
    ct = ct_ref[...]
    n_row = n_ref[...]
    qf = q.astype(F32)
    kf = k.astype(F32)
    vf = v.astype(F32)
    num = (w_inter * jnp.dot(q, ct.astype(BF16), preferred_element_type=F32)
           + jnp.dot(sc.astype(BF16), v, preferred_element_type=F32))
    den = w_inter * jnp.sum(qf * n_row, axis=1, keepdims=True) + jnp.sum(sc, axis=1, keepdims=True)
    o_ref[...] = (num / jnp.maximum(jnp.abs(den), jnp.exp(-m_out))).astype(o_ref.dtype)
    g_col = tot - cum_col + i_col
    m_next = jnp.maximum(tot + m_prev, jnp.max(g_col, axis=0, keepdims=True))
    decay = jnp.exp(tot + m_prev - m_next)
    wg = jnp.exp(g_col - m_next)
    ct_ref[...] = decay * ct + _tn_dot(k, (wg * vf).astype(BF16))
    n_ref[...] = decay * n_row + jnp.sum(wg * kf, axis=0, keepdims=True)
    m_ref[...] = m_next


def _scan_kernel(qf_ref, kf_ref, vf_ref, gcf_ref, grf_ref, qb_ref, kb_ref, vb_ref, gcb_ref, grb_ref,
                 of_ref, ob_ref, ctf_ref, nf_ref, mf_ref, ctb_ref, nb_ref, mb_ref):
    @pl.when(pl.program_id(2) == 0)
    def _():
        for r in (ctf_ref, nf_ref, mf_ref, ctb_ref, nb_ref, mb_ref):
            r[...] = jnp.zeros(r.shape, r.dtype)

    t_i = lax.broadcasted_iota(I32, (CHUNK, CHUNK), 0)
    s_i = lax.broadcasted_iota(I32, (CHUNK, CHUNK), 1)
    lower = s_i <= t_i
    upper = s_i >= t_i
    _mlstm_chunk(qf_ref[...], kf_ref[...], vf_ref[...], gcf_ref[...], grf_ref[...], 0, 2,
                 lower, upper, ctf_ref, nf_ref, mf_ref, of_ref)
    _mlstm_chunk(qb_ref[...], kb_ref[...], vb_ref[...], gcb_ref[...], grb_ref[...], 1, 3,
                 upper, lower, ctb_ref, nb_ref, mb_ref, ob_ref)


def _mlstm_scan(qk, p, gcol, grow, *, b, t, d):
    ntot = qk.shape[0]
    dh = d // M_HEADS
    ncl = t // CHUNK
    ctx0 = b * ncl

    def fwd(bi, s):
        return jnp.where(s == 0, ctx0 + bi, bi * ncl + s - 1)

    def bwd(bi, s):
        return jnp.where(s == 0, ctx0 + bi, bi * ncl + ncl - s)

    def specs(order):
        return [pl.BlockSpec((CHUNK, dh), lambda bi, h, s: (order(bi, s), h)),
                pl.BlockSpec((CHUNK, dh), lambda bi, h, s: (order(bi, s), M_HEADS + h)),
                pl.BlockSpec((CHUNK, dh), lambda bi, h, s: (order(bi, s), 2 * M_HEADS + h)),
                pl.BlockSpec((None, CHUNK, 4), lambda bi, h, s: (h, order(bi, s), 0)),
                pl.BlockSpec((None, 4, CHUNK), lambda bi, h, s: (h, 0, order(bi, s)))]

    def ospec(order):
        return pl.BlockSpec((CHUNK, dh), lambda bi, h, s: (order(bi, s), h))

    state = [pltpu.VMEM((dh, dh), F32), pltpu.VMEM((1, dh), F32), pltpu.VMEM((1, 1), F32)]
    return pl.pallas_call(
        _scan_kernel,
        grid=(b, M_HEADS, ncl + 1),
        in_specs=specs(fwd) + specs(bwd),
        out_specs=[ospec(fwd), ospec(bwd)],
        out_shape=[jax.ShapeDtypeStruct((ntot, d), F32)] * 2,
        scratch_shapes=state + state,
        compiler_params=_cparams(("arbitrary", "arbitrary", "arbitrary")),
    )(qk, qk, p, gcol, grow, qk, qk, p, gcol, grow)


def _qkprep_kernel(p_ref, gq_ref, gk_ref, cos_ref, sin_ref, o_ref, *, n_lat_chunks, nblk_q, q_scale):
    is_ctx = pl.program_id(0) >= n_lat_chunks
    cos = jnp.where(is_ctx, 1.0, cos_ref[...])
    sin = jnp.where(is_ctx, 0.0, sin_ref[...])
    lane = lax.broadcasted_iota(I32, (CHUNK, LANES), 1)
    first_half = (lane % 32) < 16
    r_i = lax.broadcasted_iota(I32, (LANES, LANES), 0)
    c_i = lax.broadcasted_iota(I32, (LANES, LANES), 1)
    group = jnp.where((r_i // D_HEAD_DIM) == (c_i // D_HEAD_DIM), 1.0, 0.0).astype(BF16)
    for blk in range(2 * nblk_q):
        x = p_ref[:, blk * LANES:(blk + 1) * LANES].astype(F32)
        ss = x * x
        ss_hi = ss.astype(BF16)
        ss_lo = (ss - ss_hi.astype(F32)).astype(BF16)
        ssum = (jnp.dot(ss_hi, group, preferred_element_type=F32)
                + jnp.dot(ss_lo, group, preferred_element_type=F32))
        g = gq_ref[...] if blk < nblk_q else gk_ref[...]
        xn = x * lax.rsqrt(ssum * (1.0 / D_HEAD_DIM) + EPS) * g
        partner = jnp.where(first_half, pltpu.roll(xn, LANES - 16, 1), pltpu.roll(xn, 16, 1))
        y = xn * cos + partner * sin
        if blk < nblk_q:
            y = y * q_scale
        o_ref[:, blk * LANES:(blk + 1) * LANES] = y.astype(o_ref.dtype)


def _diff_qkprep(p, q_g, k_g, cos_t, sin_t, *, b, t, d):
    ntot = p.shape[0]
    ncl = t // CHUNK
    g2 = lambda g: jnp.tile(g.astype(F32), 2).reshape(1, LANES)
    return pl.pallas_call(
        functools.partial(_qkprep_kernel, n_lat_chunks=b * ncl, nblk_q=d // LANES,
                          q_scale=D_HEAD_DIM ** -0.5 * math.log2(math.e)),
        grid=(ntot // CHUNK,),
        in_specs=[pl.BlockSpec((CHUNK, 2 * d), lambda i: (i, 0)),
                  pl.BlockSpec((1, LANES), lambda i: (0, 0)),
                  pl.BlockSpec((1, LANES), lambda i: (0, 0)),
                  pl.BlockSpec((CHUNK, LANES), lambda i: (i % ncl, 0)),
                  pl.BlockSpec((CHUNK, LANES), lambda i: (i % ncl, 0))],
        out_specs=pl.BlockSpec((CHUNK, 2 * d), lambda i: (i, 0)),
        out_shape=jax.ShapeDtypeStruct((ntot, 2 * d), BF16),
        compiler_params=_cparams(("arbitrary",)),
    )(p, g2(q_g), g2(k_g), cos_t, sin_t)


def _rope_tables(t):
    rows = t // GRID_W
    row = jnp.repeat(jnp.arange(rows), GRID_W).astype(F32)[:, None]
    col = jnp.tile(jnp.arange(GRID_W), rows).astype(F32)[:, None]
    nf = D_HEAD_DIM // 4
    inv = ROPE_BASE ** (-jnp.arange(nf, dtype=F32) / nf)
    ar, ac = row * inv, col * inv
    cos64 = jnp.concatenate([jnp.cos(ar), jnp.cos(ar), jnp.cos(ac), jnp.cos(ac)], axis=-1)
    sin64 = jnp.concatenate([-jnp.sin(ar), jnp.sin(ar), -jnp.sin(ac), jnp.sin(ac)], axis=-1)
    return jnp.tile(cos64, (1, 2)), jnp.tile(sin64, (1, 2))


def _attn_kernel(*refs, n_lat, out_scale):
    if n_lat:
        lam_ref, q_ref, kc_ref, vtc_ref, kl_ref, vtl_ref, sg_ref, o_ref, m1, l1, a1, m2, l2, a2 = refs[:14]
        bufs = refs[14:]
    else:
        lam_ref, q_ref, kc_ref, vtc_ref, sg_ref, o_ref, m1, l1, a1, m2, l2, a2 = refs[:12]
        bufs = refs[12:]
    sets = [tuple(bufs[i + 2 * k:i + 2 * k + 2] for k in range(4)) for i in range(0, len(bufs), 8)]
    q = q_ref[...]
    lane = lax.broadcasted_iota(I32, q.shape, 1)
    zero = jnp.zeros_like(q)
    qs = (jnp.where(lane < D_HEAD_DIM, q, zero), jnp.where(lane >= D_HEAD_DIM, q, zero))
    stats = ((m1, l1), (m2, l2))
    accs = (a1, a2)
    for (m_r, l_r), a_r in zip(stats, accs):
        m_r[...] = jnp.full(m_r.shape, NEG, F32)
        l_r[...] = jnp.zeros(l_r.shape, F32)
        a_r[...] = jnp.zeros(a_r.shape, F32)

    def scores(keys, s_bufs, mx_bufs):
        for qm, s_ref, mx_ref in zip(qs, s_bufs, mx_bufs):
            st = _nt_dot(keys, qm)
            s_ref[...] = st
            mx = st[0:8]
            for r in range(8, st.shape[0], 8):
                mx = jnp.maximum(mx, st[r:r + 8])
            mx_ref[...] = mx

    def softmax(s_bufs, mx_bufs, p_bufs, al_bufs):
        for s_ref, mx_ref, p_ref, al_ref, (m_r, l_r) in zip(s_bufs, mx_bufs, p_bufs, al_bufs, stats):
            m_old = m_r[...]
            m_new = jnp.maximum(m_old, jnp.max(mx_ref[...], axis=0, keepdims=True))
            alpha = jnp.exp2(m_old - m_new)
            m_r[...] = m_new
            al_ref[...] = alpha
            part = jnp.zeros(mx_ref.shape, F32)
            for r0 in range(0, s_ref.shape[0], SOFTMAX_ROWS):
                p = jnp.exp2(s_ref[r0:r0 + SOFTMAX_ROWS, :] - m_new)
                for r in range(0, SOFTMAX_ROWS, 8):
                    part = part + p[r:r + 8]
                p_ref[r0:r0 + SOFTMAX_ROWS, :] = p.astype(BF16)
            l_r[...] = alpha * l_r[...] + jnp.sum(part, axis=0, keepdims=True)

    def weigh(vt, p_bufs, al_bufs):
        for p_ref, al_ref, a_r in zip(p_bufs, al_bufs, accs):
            a_r[...] = al_ref[...] * a_r[...] + jnp.dot(vt, p_ref[...], preferred_element_type=F32)

    s_c, mx_c, p_c, al_c = sets[0]
    scores(kc_ref[...], s_c, mx_c)
    softmax(s_c, mx_c, p_c, al_c)
    weigh(vtc_ref[...], p_c, al_c)

    if n_lat:
        def keys(j):
            return kl_ref[pl.ds(pl.multiple_of(j * KV_CHUNK, KV_CHUNK), KV_CHUNK), :]

        (s_a, mx_a, p_a, al_a), (s_b, mx_b, p_b, al_b) = sets[1:3]
        scores(keys(0), s_a, mx_a)
        softmax(s_a, mx_a, p_a, al_a)
        scores(keys(1), s_b, mx_b)

        def body(jj, carry):
            j = 2 * jj
            softmax(s_b, mx_b, p_b, al_b)
            scores(keys(j + 2), s_a, mx_a)
            weigh(vtl_ref[j], p_a, al_a)
            softmax(s_a, mx_a, p_a, al_a)
            scores(keys(j + 3), s_b, mx_b)
            weigh(vtl_ref[j + 1], p_b, al_b)
            return carry

        lax.fori_loop(0, n_lat // 2 - 1, body, 0)
        softmax(s_b, mx_b, p_b, al_b)
        weigh(vtl_ref[n_lat - 2], p_a, al_a)
        weigh(vtl_ref[n_lat - 1], p_b, al_b)
    o = a1[...] / l1[...] - lam_ref[0] * (a2[...] / l2[...])
    o = o * lax.rsqrt(jnp.mean(o * o, axis=0, keepdims=True) + EPS) * sg_ref[...] * out_scale
    o_ref[...] = o.T.astype(o_ref.dtype)


def _diff_attention(qk, vt_lat, vt_ctx, lam, sub_g, *, b, t, d, lambda_init, latent):
    nh = d // LANES
    ctx0 = b * t // CHUNK
    n_lat = t // KV_CHUNK if latent else 0
    tq = Q_TILE if latent else CHUNK
    nq = t // tq if latent else 1
    q_row = (lambda bi, c: bi * nq + c) if latent else (lambda bi, c: ctx0 + bi)
    in_specs = [pl.BlockSpec((tq, LANES), lambda bi, h, c, lam_r: (q_row(bi, c), h)),
                pl.BlockSpec((CHUNK, LANES), lambda bi, h, c, lam_r: (ctx0 + bi, nh + h)),
                pl.BlockSpec((None, LANES, CHUNK), lambda bi, h, c, lam_r: (bi, h, 0))]
    args = [qk, qk, vt_ctx]
    if latent:
        in_specs += [pl.BlockSpec((t, LANES), lambda bi, h, c, lam_r: (bi, nh + h)),
                     pl.BlockSpec((n_lat, LANES, KV_CHUNK), lambda bi, h, c, lam_r: (bi, h, 0))]
        args += [qk, vt_lat]
    in_specs.append(pl.BlockSpec((LANES, 1), lambda bi, h, c, lam_r: (0, 0)))
    args.append(sub_g.reshape(LANES, 1).astype(F32))
    stat = pltpu.VMEM((1, tq), F32)
    acc = pltpu.VMEM((LANES, tq), F32)

    def buffer_set(tk):
        return ([pltpu.VMEM((tk, tq), F32)] * 2 + [pltpu.VMEM((8, tq), F32)] * 2
                + [pltpu.VMEM((tk, tq), BF16)] * 2 + [stat] * 2)

    grid_spec = pltpu.PrefetchScalarGridSpec(
        num_scalar_prefetch=1,
        grid=(b, nh, nq),
        in_specs=in_specs,
        out_specs=pl.BlockSpec((tq, LANES), lambda bi, h, c, lam_r: (bi * nq + c, h)),
        scratch_shapes=[stat, stat, acc, stat, stat, acc] + buffer_set(CHUNK) + (buffer_set(KV_CHUNK) * 2 if latent else []))
    return pl.pallas_call(
        functools.partial(_attn_kernel, n_lat=n_lat, out_scale=1.0 - lambda_init),
        grid_spec=grid_spec,
        out_shape=jax.ShapeDtypeStruct((b * nq * tq, d), BF16),
        compiler_params=_cparams(("arbitrary", "arbitrary", "arbitrary")),
    )(lam, *args)


def _out_kernel(*refs, mlstm):
    if mlstm:
        (rf_ref, rb_ref, op_ref, og_ref, w_ref, x_ref, g2_ref, mod_ref, rwh_ref, rwl_ref, rb2_ref,
         xo_ref, h_ref, lg_ref) = refs
        r = rf_ref[...] + rb_ref[...]
        dh = r.shape[1] // M_HEADS
        parts = []
        for hh in range(M_HEADS):
            rr = r[:, hh * dh:(hh + 1) * dh]
            parts.append(rr * lax.rsqrt(jnp.mean(rr * rr, axis=-1, keepdims=True) + EPS))
        rn = jnp.concatenate(parts, axis=1) * og_ref[...]
        u = (jax.nn.sigmoid(op_ref[...].astype(F32)) * rn).astype(BF16)
    else:
        (u_ref, w_ref, x_ref, g2_ref, mod_ref, rwh_ref, rwl_ref, rb2_ref, xo_ref, h_ref, lg_ref) = refs
        u = u_ref[...]
    y = jnp.dot(u, w_ref[...], preferred_element_type=F32)
    x = x_ref[...] + mod_ref[2:3, :] * y
    xo_ref[...] = x
    hn = x * lax.rsqrt(jnp.mean(x * x, axis=-1, keepdims=True) + EPS) * g2_ref[...]
    h = hn * (1.0 + mod_ref[4:5, :]) + mod_ref[3:4, :]
    h_ref[...] = h
    h_hi = h.astype(BF16)
    h_lo = (h - h_hi.astype(F32)).astype(BF16)
    lg_ref[...] = (jnp.dot(h_hi, rwh_ref[...], preferred_element_type=F32)
                   + jnp.dot(h_lo, rwh_ref[...], preferred_element_type=F32)
                   + jnp.dot(h_hi, rwl_ref[...], preferred_element_type=F32) + rb2_ref[...])


def _out_proj(u_args, w_out, x, g2, mods, rw_hi, rw_lo, r_bias, *, b, t, mlstm):
    ntot, d = x.shape
    tm = ROW_TILE
    row = lambda i: (i, 0)
    const = lambda i: (0, 0)
    if mlstm:
        rf, rb, p, out_g = u_args
        u_specs = [pl.BlockSpec((tm, d), row), pl.BlockSpec((tm, d), row),
                   pl.BlockSpec((tm, d), lambda i: (i, 3)), pl.BlockSpec((1, d), const)]
        u_in = [rf, rb, p, out_g.reshape(1, d)]
    else:
        u_specs = [pl.BlockSpec((tm, d), row)]
        u_in = list(u_args)
    in_specs = u_specs + [pl.BlockSpec((d, d), const),
                          pl.BlockSpec((tm, d), row),
                          pl.BlockSpec((1, d), const),
                          _mod_spec(d, t // tm, b * t // tm, b),
                          pl.BlockSpec((d, LANES), const),
                          pl.BlockSpec((d, LANES), const),
                          pl.BlockSpec((1, LANES), const)]
    return pl.pallas_call(
        functools.partial(_out_kernel, mlstm=mlstm),
        grid=(ntot // tm,),
        in_specs=in_specs,
        out_specs=[pl.BlockSpec((tm, d), row), pl.BlockSpec((tm, d), row), pl.BlockSpec((tm, LANES), row)],
        out_shape=[jax.ShapeDtypeStruct((ntot, d), F32), jax.ShapeDtypeStruct((ntot, d), F32),
                   jax.ShapeDtypeStruct((ntot, LANES), F32)],
        compiler_params=_cparams(("arbitrary",)),
    )(*u_in, w_out, x, g2.reshape(1, d), mods, rw_hi, rw_lo, r_bias)


def _lane_pick(lane, pairs):
    out = jnp.zeros(lane.shape, F32)
    for idx, val in pairs:
        out = jnp.where(lane == idx, val, out)
    return out


def _route_kernel(lg_ref, rt_ref, cnt_ref):
    @pl.when(pl.program_id(0) == 0)
    def _():
        cnt_ref[...] = jnp.zeros(cnt_ref.shape, F32)

    lg = lg_ref[...]
    lane = lax.broadcasted_iota(I32, lg.shape, 1)
    lane_f = lane.astype(F32)
    big = 1e9
    is_g = lane < N_GROUPS
    gl = jnp.where(is_g, lg, NEG)
    gmax = jnp.max(gl, axis=1, keepdims=True)
    gidx = jnp.min(jnp.where(gl == gmax, lane_f, big), axis=1, keepdims=True)
    gsum = jnp.sum(jnp.where(is_g, jnp.exp(gl - gmax), 0.0), axis=1, keepdims=True)
    gp_top = 1.0 / gsum
    e_lane = lane_f - float(N_GROUPS)
    lo = gidx * float(EXPERTS_PER_GROUP)
    in_grp = jnp.where(e_lane >= lo, jnp.where(e_lane < lo + float(EXPERTS_PER_GROUP), 1.0, 0.0), 0.0) > 0.5
    el = jnp.where(in_grp, lg, NEG)
    v0 = jnp.max(el, axis=1, keepdims=True)
    i0 = jnp.min(jnp.where(el == v0, lane_f, big), axis=1, keepdims=True)
    el2 = jnp.where(lane_f == i0, NEG, el)
    v1 = jnp.max(el2, axis=1, keepdims=True)
    i1 = jnp.min(jnp.where(el2 == v1, lane_f, big), axis=1, keepdims=True)
    e1 = jnp.exp(v1 - v0)
    g0 = gp_top / (1.0 + e1)
    g1 = gp_top * e1 / (1.0 + e1)
    eid0 = i0 - float(N_GROUPS)
    eid1 = i1 - float(N_GROUPS)
    rt_ref[...] = _lane_pick(lane, ((0, eid0), (1, eid1), (2, g0), (3, g1)))
    hot = jnp.where(lane_f == eid0, 1.0, 0.0) + jnp.where(lane_f == eid1, 1.0, 0.0)
    cnt_ref[0:1, :] += jnp.sum(hot, axis=0, keepdims=True)


def _route(logits):
    ntot = logits.shape[0]
    return pl.pallas_call(
        _route_kernel,
        grid=(ntot // CHUNK,),
        in_specs=[pl.BlockSpec((CHUNK, LANES), lambda i: (i, 0))],
        out_specs=[pl.BlockSpec((CHUNK, LANES), lambda i: (i, 0)), pl.BlockSpec((8, LANES), lambda i: (0, 0))],
        out_shape=[jax.ShapeDtypeStruct((ntot, LANES), F32), jax.ShapeDtypeStruct((8, LANES), F32)],
        compiler_params=_cparams(("arbitrary",)),
    )(logits)


def _slot_kernel(rt_ref, cnt_ref, o_ref, run_ref):
    @pl.when(pl.program_id(0) == 0)
    def _():
        cnt = cnt_ref[0:1, :].astype(I32)
        padded = (((cnt + (MOE_BLOCK - 1)) // MOE_BLOCK) * MOE_BLOCK).astype(F32)
        r_i = lax.broadcasted_iota(I32, (LANES, LANES), 0)
        c_i = lax.broadcasted_iota(I32, (LANES, LANES), 1)
        col = jnp.sum(jnp.where(r_i == c_i, jnp.broadcast_to(padded, (LANES, LANES)), 0.0), axis=1, keepdims=True)
        run_ref[...] = jnp.sum(jnp.where(r_i < c_i, col, 0.0), axis=0, keepdims=True)

    rt = rt_ref[...]
    lane = lax.broadcasted_iota(I32, rt.shape, 1)
    lane_f = lane.astype(F32)
    hot0 = lane_f == rt[:, 0:1]
    hot1 = lane_f == rt[:, 1:2]
    both = jnp.where(hot0, 1.0, 0.0) + jnp.where(hot1, 1.0, 0.0)
    t_i = lax.broadcasted_iota(I32, (CHUNK, CHUNK), 0)
    s_i = lax.broadcasted_iota(I32, (CHUNK, CHUNK), 1)
    before = jnp.where(s_i < t_i, 1.0, 0.0).astype(BF16)
    base = run_ref[...] + jnp.dot(before, both.astype(BF16), preferred_element_type=F32)
    d0 = jnp.sum(jnp.where(hot0, base, 0.0), axis=1, keepdims=True)
    d1 = jnp.sum(jnp.where(hot1, base, 0.0), axis=1, keepdims=True)
    o_ref[...] = _lane_pick(lane, ((0, d0), (1, d1))).astype(I32)
    run_ref[...] += jnp.sum(both, axis=0, keepdims=True)


def _slots(route, counts):
    ntot = route.shape[0]
    return pl.pallas_call(
        _slot_kernel,
        grid=(ntot // CHUNK,),
        in_specs=[pl.BlockSpec((CHUNK, LANES), lambda i: (i, 0)), pl.BlockSpec((8, LANES), lambda i: (0, 0))],
        out_specs=pl.BlockSpec((CHUNK, LANES), lambda i: (i, 0)),
        out_shape=jax.ShapeDtypeStruct((ntot, LANES), I32),
        scratch_shapes=[pltpu.VMEM((1, LANES), F32)],
        compiler_params=_cparams(("arbitrary",)),
    )(route, counts)


def _dispatch_kernel(dest_ref, h_ref, xs_in_ref, xs_ref, sem):
    del xs_in_ref

    def row_copy(r, k):
        return pltpu.make_async_copy(h_ref.at[pl.ds(r, 1), :], xs_ref.at[pl.ds(dest_ref[0, 2 * r + k], 1), :], sem)

    def start(r, carry):
        row_copy(r, 0).start(priority=0)
        row_copy(r, 1).start(priority=1)
        return carry

    def wait(r, carry):
        row_copy(r, 0).wait()
        row_copy(r, 1).wait()
        return carry

    lax.fori_loop(0, CHUNK, start, 0, unroll=DMA_UNROLL)
    lax.fori_loop(0, CHUNK, wait, 0, unroll=DMA_UNROLL)


def _dispatch(dest, h, buf):
    ntot, d = h.shape
    cap = buf.shape[0]
    return pl.pallas_call(
        _dispatch_kernel,
        grid=(ntot // CHUNK,),
        in_specs=[pl.BlockSpec((None, 1, 2 * CHUNK), lambda i: (i, 0, 0), memory_space=pltpu.SMEM),
                  pl.BlockSpec((CHUNK, d), lambda i: (i, 0)),
                  pl.BlockSpec(memory_space=pl.ANY)],
        out_specs=pl.BlockSpec(memory_space=pl.ANY),
        out_shape=jax.ShapeDtypeStruct((cap, d), F32),
        scratch_shapes=[pltpu.SemaphoreType.DMA],
        input_output_aliases={2: 0},
        compiler_params=_cparams(("arbitrary",)),
    )(dest, h, buf)


def _expert_kernel(be_ref, nu_ref, xs_ref, wgu_ref, wdn_ref, ys_ref, wgu_bf, wdn_bf):
    j = pl.program_id(0)
    changed = jnp.logical_or(j == 0, be_ref[j] != be_ref[jnp.maximum(j - 1, 0)])

    @pl.when(changed)
    def _():
        wgu_bf[...] = wgu_ref[...].astype(BF16)
        wdn_bf[...] = wdn_ref[...].astype(BF16)

    @pl.when(j < nu_ref[0])
    def _():
        gu = jnp.dot(xs_ref[...].astype(BF16), wgu_bf[...], preferred_element_type=F32)
        g = gu[:, :D_EXPERT]
        a = (g * jax.nn.sigmoid(g) * gu[:, D_EXPERT:]).astype(BF16)
        ys_ref[...] = jnp.dot(a, wdn_bf[...], preferred_element_type=F32)

    @pl.when(j >= nu_ref[0])
    def _():
        ys_ref[...] = jnp.zeros(ys_ref.shape, ys_ref.dtype)


def _experts(blk_e, n_used, xs, w_gu, w_dn, layer):
    cap, d = xs.shape
    row = lambda j, be, nu: (jnp.minimum(j, nu[0] - 1), 0)
    grid_spec = pltpu.PrefetchScalarGridSpec(
        num_scalar_prefetch=2,
        grid=(cap // MOE_BLOCK,),
        in_specs=[pl.BlockSpec((MOE_BLOCK, d), row),
                  pl.BlockSpec((None, None, d, 2 * D_EXPERT), lambda j, be, nu: (layer, be[j], 0, 0)),
                  pl.BlockSpec((None, None, D_EXPERT, d), lambda j, be, nu: (layer, be[j], 0, 0))],
        out_specs=pl.BlockSpec((MOE_BLOCK, d), lambda j, be, nu: (j, 0)),
        scratch_shapes=[pltpu.VMEM((d, 2 * D_EXPERT), BF16), pltpu.VMEM((D_EXPERT, d), BF16)])
    return pl.pallas_call(
        _expert_kernel,
        grid_spec=grid_spec,
        out_shape=jax.ShapeDtypeStruct((cap, d), F32),
        compiler_params=_cparams(("arbitrary",)),
    )(blk_e, n_used, xs, w_gu, w_dn)


def _combine_kernel(dest_ref, rt_ref, x_ref, mod_ref, ys_ref, xo_ref, buf, sem):
    def row_copy(r, k):
        return pltpu.make_async_copy(ys_ref.at[pl.ds(dest_ref[0, 2 * r + k], 1), :],
                                     buf.at[pl.ds(k * CHUNK + r, 1), :], sem)

    def start(r, carry):
        row_copy(r, 0).start(priority=0)
        row_copy(r, 1).start(priority=1)
        return carry

    def wait(r, carry):
        row_copy(r, 0).wait()
        row_copy(r, 1).wait()
        return carry

    lax.fori_loop(0, CHUNK, start, 0, unroll=DMA_UNROLL)
    lax.fori_loop(0, CHUNK, wait, 0, unroll=DMA_UNROLL)
    rt = rt_ref[...]
    y = rt[:, 2:3] * buf[0:CHUNK, :] + rt[:, 3:4] * buf[CHUNK:2 * CHUNK, :]
    xo_ref[...] = x_ref[...] + mod_ref[5:6, :] * y


def _combine(dest, route, x, mods, ys, *, b, t, n_rows):
    d = x.shape[1]
    ncl = t // CHUNK
    return pl.pallas_call(
        _combine_kernel,
        grid=(n_rows // CHUNK,),
        in_specs=[pl.BlockSpec((None, 1, 2 * CHUNK), lambda i: (i, 0, 0), memory_space=pltpu.SMEM),
                  pl.BlockSpec((CHUNK, LANES), lambda i: (i, 0)),
                  pl.BlockSpec((CHUNK, d), lambda i: (i, 0)),
                  _mod_spec(d, ncl, b * ncl, b),
                  pl.BlockSpec(memory_space=pl.ANY)],
        out_specs=pl.BlockSpec((CHUNK, d), lambda i: (i, 0)),
        out_shape=jax.ShapeDtypeStruct((n_rows, d), F32),
        scratch_shapes=[pltpu.VMEM((2 * CHUNK, d), F32), pltpu.SemaphoreType.DMA],
        compiler_params=_cparams(("arbitrary",)),
    )(dest, route, x, mods, ys)


def _moe_capacity(ntot):
    return -(-2 * ntot // MOE_BLOCK) * MOE_BLOCK + N_EXPERTS * MOE_BLOCK


def _moe(x, h, logits, mods, w_gu, w_dn, layer, slot_buf, *, b, t, n_rows):
    ntot = x.shape[0]
    route, counts = _route(logits)
    slot = _slots(route, counts)
    dest = slot[:, :2].reshape(ntot // CHUNK, 1, 2 * CHUNK)
    nblk = slot_buf.shape[0] // MOE_BLOCK
    cnt = counts[0, :N_EXPERTS].astype(I32)
    pad_end = jnp.cumsum((cnt + MOE_BLOCK - 1) // MOE_BLOCK * MOE_BLOCK)
    n_used = pad_end[N_EXPERTS - 1:] // MOE_BLOCK
    blk_start = jnp.minimum(jnp.arange(nblk, dtype=I32), n_used - 1) * MOE_BLOCK
    blk_e = jnp.minimum(jnp.sum((pad_end[None, :] <= blk_start[:, None]).astype(I32), axis=1), N_EXPERTS - 1)
    xs = _dispatch(dest, h, slot_buf)
    ys = _experts(blk_e.astype(I32), n_used.astype(I32), xs, w_gu, w_dn, layer)
    return _combine(dest, route, x, mods, ys, b=b, t=t, n_rows=n_rows), xs


def kernel(x, c, ctx, c_ctx, ada_w, ada_b, norm1_g, norm2_g, m_w_in, m_conv_w, m_conv_b, m_gate_b, m_out_g,
           m_w_out, d_w_in, d_q_g, d_k_g, d_lq1, d_lk1, d_lq2, d_lk2, d_sub_g, d_w_out, r_grp_w, r_grp_b,
           r_exp_w, r_exp_b, e_w_gu, e_w_dn):
    b, t, d = x.shape
    cl = ctx.shape[1]
    depth = ada_w.shape[0]
    assert cl == CHUNK and d == 8 * LANES and b + 1 <= 8
    assert t % Q_TILE == 0 and t % (4 * KV_CHUNK) == 0 and t % ROW_TILE == 0 and (b * cl) % ROW_TILE == 0
    n_lat = b * t
    ntot = n_lat + b * cl

    xs = jnp.concatenate([x.reshape(n_lat, d), ctx.reshape(b * cl, d)], axis=0)
    cond = jnp.zeros((8, d), F32).at[:b].set(c).at[b].set(c_ctx)
    mods_all = _modulation(cond, ada_w, ada_b)
    cos_t, sin_t = _rope_tables(t)
    slot_buf = jnp.zeros((_moe_capacity(ntot), d), F32)

    n_route = N_GROUPS + N_EXPERTS
    for i in range(depth):
        jm = i // 2
        mods = mods_all[i]
        rw = jnp.zeros((d, LANES), F32).at[:, :N_GROUPS].set(r_grp_w[i]).at[:, N_GROUPS:n_route].set(r_exp_w[i])
        rw_hi = rw.astype(BF16)
        rw_lo = (rw - rw_hi.astype(F32)).astype(BF16)
        r_bias = jnp.zeros((1, LANES), F32).at[0, :N_GROUPS].set(r_grp_b[i]).at[0, N_GROUPS:n_route].set(r_exp_b[i])
        if i % 2 == 0:
            w_in = m_w_in[jm]
            wg = jnp.zeros((d, LANES), F32).at[:, :4 * M_HEADS].set(w_in[:, 4 * d:]).astype(BF16)
            gb = jnp.zeros((1, LANES), F32).at[0, :4 * M_HEADS].set(m_gate_b[jm])
            p, gates = _in_proj(xs, norm1_g[i], mods, w_in[:, :4 * d].astype(BF16), b=b, t=t, tn=1024,
                                gates=(wg, gb))
            qk = _mlstm_conv(p, m_conv_w[jm], m_conv_b[jm], b=b, t=t, d=d)
            g4 = gates[:, :4 * M_HEADS].reshape(ntot, 4, M_HEADS)
            rf, rb = _mlstm_scan(qk, p, g4.transpose(2, 0, 1), g4.transpose(2, 1, 0), b=b, t=t, d=d)
            xs, h, logits = _out_proj((rf, rb, p, m_out_g[jm]), m_w_out[jm].astype(BF16), xs, norm2_g[i], mods,
                                      rw_hi, rw_lo, r_bias, b=b, t=t, mlstm=True)
        else:
            lambda_init = 0.8 - 0.6 * math.exp(-0.3 * i)
            p = _in_proj(xs, norm1_g[i], mods, d_w_in[jm].astype(BF16), b=b, t=t, tn=1024)
            qk = _diff_qkprep(p, d_q_g[jm], d_k_g[jm], cos_t, sin_t, b=b, t=t, d=d)
            lam = (jnp.exp(jnp.sum(d_lq1[jm] * d_lk1[jm])) - jnp.exp(jnp.sum(d_lq2[jm] * d_lk2[jm]))
                   + lambda_init).reshape(1).astype(F32)
            v = p[:, 2 * d:]
            vt_lat = v[:n_lat].reshape(n_lat // KV_CHUNK, KV_CHUNK, d).transpose(0, 2, 1)
            vt_ctx = v[n_lat:].reshape(b, cl, d).transpose(0, 2, 1)
            attn = functools.partial(_diff_attention, qk, vt_lat, vt_ctx, lam, d_sub_g[jm], b=b, t=t, d=d,
                                     lambda_init=lambda_init)
            o = jnp.concatenate([attn(latent=True), attn(latent=False)], axis=0)
            xs, h, logits = _out_proj((o,), d_w_out[jm].astype(BF16), xs, norm2_g[i], mods,
                                      rw_hi, rw_lo, r_bias, b=b, t=t, mlstm=False)
        last = i == depth - 1
        xs, slot_buf = _moe(xs, h, logits, mods, e_w_gu, e_w_dn, i, slot_buf, b=b, t=t,
                            n_rows=n_lat if last else ntot)
    return xs.reshape(b, t, d)
```

```python
import functools
import math

import jax
import jax.numpy as jnp
from jax import lax
from jax.experimental import pallas as pl
from jax.experimental.pallas import tpu as pltpu

F32 = jnp.float32
BF16 = jnp.bfloat16
I32 = jnp.int32
HIGHEST = lax.Precision.HIGHEST

EPS = 1e-6
GRID_W = 64
ROPE_BASE = 10000.0
M_HEADS = 4
M_CONV_W = 5
D_HEADS = 8
D_HEAD_DIM = 64
N_GROUPS = 4
EXPERTS_PER_GROUP = 8
N_EXPERTS = 32
D_EXPERT = 512

LANES = 128
CHUNK = 256
ROW_TILE = 512
Q_TILE = 512
KV_CHUNK = 1024
SOFTMAX_ROWS = 64
V_EXT = LANES + 16
MOE_BLOCK = 256
DMA_UNROLL = 8
NEG = -1e30
VMEM_LIMIT = 56 * 1024 * 1024


def _cparams(sem):
    return pltpu.CompilerParams(dimension_semantics=sem, vmem_limit_bytes=VMEM_LIMIT)


def _nt_dot(a, b):
    return lax.dot_general(a, b, (((1,), (1,)), ((), ())), preferred_element_type=F32)


def _tn_dot(a, b):
    return lax.dot_general(a, b, (((0,), (0,)), ((), ())), preferred_element_type=F32)


def _mod_spec(d, tiles_per_batch, n_lat_tiles, b):
    return pl.BlockSpec((None, 6, d), lambda i, *_: (jnp.where(i < n_lat_tiles, i // tiles_per_batch, b), 0, 0))


def _mod_kernel(s_ref, w_ref, b_ref, o_ref):
    s = s_ref[...]
    s = s * jax.nn.sigmoid(s)
    o_ref[...] = jnp.dot(s, w_ref[...], precision=HIGHEST, preferred_element_type=F32) + b_ref[...]


def _modulation(cond, ada_w, ada_b):
    depth, d, n = ada_w.shape
    tn = n // 4
    out = pl.pallas_call(
        _mod_kernel,
        grid=(depth, n // tn),
        in_specs=[pl.BlockSpec((8, d), lambda l, j: (0, 0)),
                  pl.BlockSpec((None, d, tn), lambda l, j: (l, 0, j)),
                  pl.BlockSpec((None, 1, tn), lambda l, j: (l, 0, j))],
        out_specs=pl.BlockSpec((None, 8, tn), lambda l, j: (l, 0, j)),
        out_shape=jax.ShapeDtypeStruct((depth, 8, n), F32),
        compiler_params=_cparams(("arbitrary", "arbitrary")),
    )(cond, ada_w, ada_b.reshape(depth, 1, n))
    return out.reshape(depth, 8, 6, d)


def _in_kernel(*refs, has_gates):
    if has_gates:
        x_ref, g_ref, mod_ref, w_ref, wg_ref, gb_ref, o_ref, go_ref, h_ref = refs
    else:
        x_ref, g_ref, mod_ref, w_ref, o_ref, h_ref = refs

    @pl.when(pl.program_id(1) == 0)
    def _():
        x = x_ref[...]
        y = x * lax.rsqrt(jnp.mean(x * x, axis=-1, keepdims=True) + EPS) * g_ref[...]
        hb = (y * (1.0 + mod_ref[1:2, :]) + mod_ref[0:1, :]).astype(BF16)
        h_ref[...] = hb
        if has_gates:
            go_ref[...] = jnp.dot(hb, wg_ref[...], preferred_element_type=F32) + gb_ref[...]

    o_ref[...] = jnp.dot(h_ref[...], w_ref[...], preferred_element_type=F32).astype(o_ref.dtype)


def _in_proj(x, g, mods, w, *, b, t, tn, gates=None):
    ntot, d = x.shape
    n = w.shape[1]
    tm = ROW_TILE
    has_gates = gates is not None
    in_specs = [pl.BlockSpec((tm, d), lambda i, j: (i, 0)),
                pl.BlockSpec((1, d), lambda i, j: (0, 0)),
                _mod_spec(d, t // tm, b * t // tm, b),
                pl.BlockSpec((d, tn), lambda i, j: (0, j))]
    args = [x, g.reshape(1, d), mods, w]
    out_specs = pl.BlockSpec((tm, tn), lambda i, j: (i, j))
    out_shape = jax.ShapeDtypeStruct((ntot, n), BF16)
    if has_gates:
        wg, gb = gates
        in_specs += [pl.BlockSpec((d, LANES), lambda i, j: (0, 0)),
                     pl.BlockSpec((1, LANES), lambda i, j: (0, 0))]
        args += [wg, gb]
        out_specs = [out_specs, pl.BlockSpec((tm, LANES), lambda i, j: (i, 0))]
        out_shape = [out_shape, jax.ShapeDtypeStruct((ntot, LANES), F32)]
    return pl.pallas_call(
        functools.partial(_in_kernel, has_gates=has_gates),
        grid=(ntot // tm, n // tn),
        in_specs=in_specs,
        out_specs=out_specs,
        out_shape=out_shape,
        scratch_shapes=[pltpu.VMEM((tm, d), BF16)],
        compiler_params=_cparams(("arbitrary", "arbitrary")),
    )(*args)


def _conv_kernel(xm_ref, xp_ref, xn_ref, w_ref, b_ref, o_ref, *, ncl, n_lat_chunks, ncol_q, k_scale):
    i = pl.program_id(0)
    j = pl.program_id(1)
    c = i % ncl
    is_lat = i < n_lat_chunks
    left_ok = jnp.logical_and(is_lat, c >= 1)
    right_ok = jnp.logical_and(is_lat, c < ncl - 1)
    xm = xm_ref[...].astype(F32)
    xp = jnp.where(left_ok, xp_ref[...].astype(F32), 0.0)[14:16, :]
    xn = jnp.where(right_ok, xn_ref[...].astype(F32), 0.0)[0:2, :]
    ext = jnp.concatenate([xp, xm, xn], axis=0)
    rows = xm.shape[0]
    w = w_ref[...]
    y = b_ref[...] + w[0:1, :] * ext[0:rows, :]
    for k in range(1, M_CONV_W):
        y = y + w[k:k + 1, :] * ext[k:k + rows, :]
    y = y * jax.nn.sigmoid(y)
    y = y * jnp.where(j >= ncol_q, k_scale, 1.0)
    o_ref[...] = y.astype(o_ref.dtype)


def _mlstm_conv(p, conv_w, conv_b, *, b, t, d):
    ntot = p.shape[0]
    tn = 512
    nrb = ntot // 16
    return pl.pallas_call(
        functools.partial(_conv_kernel, ncl=t // CHUNK, n_lat_chunks=b * t // CHUNK, ncol_q=d // tn,
                          k_scale=(d // M_HEADS) ** -0.5),
        grid=(ntot // CHUNK, 2 * d // tn),
        in_specs=[pl.BlockSpec((CHUNK, tn), lambda i, j: (i, j)),
                  pl.BlockSpec((16, tn), lambda i, j: (jnp.maximum(i * (CHUNK // 16) - 1, 0), j)),
                  pl.BlockSpec((16, tn), lambda i, j: (jnp.minimum((i + 1) * (CHUNK // 16), nrb - 1), j)),
                  pl.BlockSpec((M_CONV_W, tn), lambda i, j: (0, j)),
                  pl.BlockSpec((1, tn), lambda i, j: (0, j))],
        out_specs=pl.BlockSpec((CHUNK, tn), lambda i, j: (i, j)),
        out_shape=jax.ShapeDtypeStruct((ntot, 2 * d), BF16),
        compiler_params=_cparams(("arbitrary", "arbitrary")),
    )(p, p, p, conv_w, conv_b.reshape(1, 2 * d))


def _log_sigmoid(x):
    return jnp.minimum(x, 0.0) - jnp.log(1.0 + jnp.exp(-jnp.abs(x)))


def _mlstm_chunk(q, k, v, gcol, grow, igate, fgate, mask, mask_t, ct_ref, n_ref, m_ref, o_ref):
    i_row = grow[igate:igate + 1, :]
    i_col = gcol[:, igate:igate + 1]
    f_row = _log_sigmoid(grow[fgate:fgate + 1, :])
    f_col = _log_sigmoid(gcol[:, fgate:fgate + 1])
    cum_col = jnp.sum(jnp.where(mask, f_row, 0.0), axis=1, keepdims=True)
    cum_row = jnp.sum(jnp.where(mask_t, f_col, 0.0), axis=0, keepdims=True)
    tot = jnp.sum(f_col, axis=0, keepdims=True)
    m_prev = m_ref[...]
    dmat = jnp.where(mask, cum_col - cum_row + i_row, NEG)
    m_inter = cum_col + m_prev
    m_out = jnp.maximum(m_inter, jnp.max(dmat, axis=1, keepdims=True))
    w_intra = jnp.exp(dmat - m_out)
    w_inter = jnp.exp(m_inter - m_out)
    sc = _nt_dot(q, k) * w_intra
    ct = ct_ref[...]
    n_row = n_ref[...]
    qf = q.astype(F32)
    kf = k.astype(F32)
    vf = v.astype(F32)
    num = (w_inter * jnp.dot(q, ct.astype(BF16), preferred_element_type=F32)
           + jnp.dot(sc.astype(BF16), v, preferred_element_type=F32))
    den = w_inter * jnp.sum(qf * n_row, axis=1, keepdims=True) + jnp.sum(sc, axis=1, keepdims=True)
    o_ref[...] = (num / jnp.maximum(jnp.abs(den), jnp.exp(-m_out))).astype(o_ref.dtype)
    g_col = tot - cum_col + i_col
    m_next = jnp.maximum(tot + m_prev, jnp.max(g_col, axis=0, keepdims=True))
    decay = jnp.exp(tot + m_prev - m_next)
    wg = jnp.exp(g_col - m_next)
    ct_ref[...] = decay * ct + _tn_dot(k, (wg * vf).astype(BF16))
    n_ref[...] = decay * n_row + jnp.sum(wg * kf, axis=0, keepdims=True)
    m_ref[...] = m_next


def _scan_kernel(qf_ref, kf_ref, vf_ref, gcf_ref, grf_ref, qb_ref, kb_ref, vb_ref, gcb_ref, grb_ref,
                 of_ref, ob_ref, ctf_ref, nf_ref, mf_ref, ctb_ref, nb_ref, mb_ref):
    @pl.when(pl.program_id(2) == 0)
    def _():
        for r in (ctf_ref, nf_ref, mf_ref, ctb_ref, nb_ref, mb_ref):
            r[...] = jnp.zeros(r.shape, r.dtype)

    t_i = lax.broadcasted_iota(I32, (CHUNK, CHUNK), 0)
    s_i = lax.broadcasted_iota(I32, (CHUNK, CHUNK), 1)
    lower = s_i <= t_i
    upper = s_i >= t_i
    _mlstm_chunk(qf_ref[...], kf_ref[...], vf_ref[...], gcf_ref[...], grf_ref[...], 0, 2,
                 lower, upper, ctf_ref, nf_ref, mf_ref, of_ref)
    _mlstm_chunk(qb_ref[...], kb_ref[...], vb_ref[...], gcb_ref[...], grb_ref[...], 1, 3,
                 upper, lower, ctb_ref, nb_ref, mb_ref, ob_ref)


def _mlstm_scan(qk, p, gcol, grow, *, b, t, d):
    ntot = qk.shape[0]
    dh = d // M_HEADS
    ncl = t // CHUNK
    ctx0 = b * ncl

    def fwd(bi, s):
        return jnp.where(s == 0, ctx0 + bi, bi * ncl + s - 1)

    def bwd(bi, s):
        return jnp.where(s == 0, ctx0 + bi, bi * ncl + ncl - s)

    def specs(order):
        return [pl.BlockSpec((CHUNK, dh), lambda bi, h, s: (order(bi, s), h)),
                pl.BlockSpec((CHUNK, dh), lambda bi, h, s: (order(bi, s), M_HEADS + h)),
                pl.BlockSpec((CHUNK, dh), lambda bi, h, s: (order(bi, s), 2 * M_HEADS + h)),
                pl.BlockSpec((None, CHUNK, 4), lambda bi, h, s: (h, order(bi, s), 0)),
                pl.BlockSpec((None, 4, CHUNK), lambda bi, h, s: (h, 0, order(bi, s)))]

    def ospec(order):
        return pl.BlockSpec((CHUNK, dh), lambda bi, h, s: (order(bi, s), h))

    state = [pltpu.VMEM((dh, dh), F32), pltpu.VMEM((1, dh), F32), pltpu.VMEM((1, 1), F32)]
    return pl.pallas_call(
        _scan_kernel,
        grid=(b, M_HEADS, ncl + 1),
        in_specs=specs(fwd) + specs(bwd),
        out_specs=[ospec(fwd), ospec(bwd)],
        out_shape=[jax.ShapeDtypeStruct((ntot, d), F32)] * 2,
        scratch_shapes=state + state,
        compiler_params=_cparams(("arbitrary", "arbitrary", "arbitrary")),
    )(qk, qk, p, gcol, grow, qk, qk, p, gcol, grow)


def _qkprep_kernel(p_ref, gq_ref, gk_ref, cos_ref, sin_ref, o_ref, *, n_lat_chunks, nblk_q, q_scale):
    is_ctx = pl.program_id(0) >= n_lat_chunks
    cos = jnp.where(is_ctx, 1.0, cos_ref[...])
    sin = jnp.where(is_ctx, 0.0, sin_ref[...])
    lane = lax.broadcasted_iota(I32, (CHUNK, LANES), 1)
    first_half = (lane % 32) < 16
    r_i = lax.broadcasted_iota(I32, (LANES, LANES), 0)
    c_i = lax.broadcasted_iota(I32, (LANES, LANES), 1)
    group = jnp.where((r_i // D_HEAD_DIM) == (c_i // D_HEAD_DIM), 1.0, 0.0).astype(BF16)
    for blk in range(2 * nblk_q):
        x = p_ref[:, blk * LANES:(blk + 1) * LANES].astype(F32)
        ss = x * x
        ss_hi = ss.astype(BF16)
        ss_lo = (ss - ss_hi.astype(F32)).astype(BF16)
        ssum = (jnp.dot(ss_hi, group, preferred_element_type=F32)
                + jnp.dot(ss_lo, group, preferred_element_type=F32))
        g = gq_ref[...] if blk < nblk_q else gk_ref[...]
        xn = x * lax.rsqrt(ssum * (1.0 / D_HEAD_DIM) + EPS) * g
        partner = jnp.where(first_half, pltpu.roll(xn, LANES - 16, 1), pltpu.roll(xn, 16, 1))
        y = xn * cos + partner * sin
        if blk < nblk_q:
            y = y * q_scale
        o_ref[:, blk * LANES:(blk + 1) * LANES] = y.astype(o_ref.dtype)


def _diff_qkprep(p, q_g, k_g, cos_t, sin_t, *, b, t, d):
    ntot = p.shape[0]
    ncl = t // CHUNK
    g2 = lambda g: jnp.tile(g.astype(F32), 2).reshape(1, LANES)
    return pl.pallas_call(
        functools.partial(_qkprep_kernel, n_lat_chunks=b * ncl, nblk_q=d // LANES,
                          q_scale=D_HEAD_DIM ** -0.5 * math.log2(math.e)),
        grid=(ntot // CHUNK,),
        in_specs=[pl.BlockSpec((CHUNK, 2 * d), lambda i: (i, 0)),
                  pl.BlockSpec((1, LANES), lambda i: (0, 0)),
                  pl.BlockSpec((1, LANES), lambda i: (0, 0)),
                  pl.BlockSpec((CHUNK, LANES), lambda i: (i % ncl, 0)),
                  pl.BlockSpec((CHUNK, LANES), lambda i: (i % ncl, 0))],
        out_specs=pl.BlockSpec((CHUNK, 2 * d), lambda i: (i, 0)),
        out_shape=jax.ShapeDtypeStruct((ntot, 2 * d), BF16),
        compiler_params=_cparams(("arbitrary",)),
    )(p, g2(q_g), g2(k_g), cos_t, sin_t)


def _rope_tables(t):
    rows = t // GRID_W
    row = jnp.repeat(jnp.arange(rows), GRID_W).astype(F32)[:, None]
    col = jnp.tile(jnp.arange(GRID_W), rows).astype(F32)[:, None]
    nf = D_HEAD_DIM // 4
    inv = ROPE_BASE ** (-jnp.arange(nf, dtype=F32) / nf)
    ar, ac = row * inv, col * inv
    cos64 = jnp.concatenate([jnp.cos(ar), jnp.cos(ar), jnp.cos(ac), jnp.cos(ac)], axis=-1)
    sin64 = jnp.concatenate([-jnp.sin(ar), jnp.sin(ar), -jnp.sin(ac), jnp.sin(ac)], axis=-1)
    return jnp.tile(cos64, (1, 2)), jnp.tile(sin64, (1, 2))


def _attn_kernel(*refs, n_lat, out_scale):
    if n_lat:
        lam_ref, q_ref, kc_ref, vtc_ref, kl_ref, vtl_ref, sg_ref, o_ref, m1, a1, m2, a2 = refs[:12]
        bufs = refs[12:]
    else:
        lam_ref, q_ref, kc_ref, vtc_ref, sg_ref, o_ref, m1, a1, m2, a2 = refs[:10]
        bufs = refs[10:]
    sets = [tuple(bufs[i + 2 * k:i + 2 * k + 2] for k in range(4)) for i in range(0, len(bufs), 8)]
    q = q_ref[...]
    lane = lax.broadcasted_iota(I32, q.shape, 1)
    zero = jnp.zeros_like(q)
    qs = (jnp.where(lane < D_HEAD_DIM, q, zero), jnp.where(lane >= D_HEAD_DIM, q, zero))
    maxes = (m1, m2)
    accs = (a1, a2)
    for m_r, a_r in zip(maxes, accs):
        m_r[...] = jnp.full(m_r.shape, NEG, F32)
        a_r[...] = jnp.zeros(a_r.shape, F32)

    def scores(keys, s_bufs, mx_bufs):
        for qm, s_ref, mx_ref in zip(qs, s_bufs, mx_bufs):
            st = _nt_dot(keys, qm)
            s_ref[...] = st
            mx = st[0:8]
            for r in range(8, st.shape[0], 8):
                mx = jnp.maximum(mx, st[r:r + 8])
            mx_ref[...] = mx

    def softmax(s_bufs, mx_bufs, p_bufs, al_bufs):
        for s_ref, mx_ref, p_ref, al_ref, m_r in zip(s_bufs, mx_bufs, p_bufs, al_bufs, maxes):
            m_old = m_r[...]
            m_new = jnp.maximum(m_old, jnp.max(mx_ref[...], axis=0, keepdims=True))
            m_r[...] = m_new
            al_ref[...] = jnp.exp2(m_old - m_new)
            for r0 in range(0, s_ref.shape[0], SOFTMAX_ROWS):
                p = jnp.exp2(s_ref[r0:r0 + SOFTMAX_ROWS, :] - m_new)
                p_ref[r0:r0 + SOFTMAX_ROWS, :] = p.astype(BF16)

    def weigh(vt, p_bufs, al_bufs):
        for p_ref, al_ref, a_r in zip(p_bufs, al_bufs, accs):
            a_r[...] = al_ref[...] * a_r[...] + jnp.dot(vt, p_ref[...], preferred_element_type=F32)

    s_c, mx_c, p_c, al_c = sets[0]
    scores(kc_ref[...], s_c, mx_c)
    softmax(s_c, mx_c, p_c, al_c)
    weigh(vtc_ref[...], p_c, al_c)

    if n_lat:
        def keys(j):
            return kl_ref[pl.ds(pl.multiple_of(j * KV_CHUNK, KV_CHUNK), KV_CHUNK), :]

        (s_a, mx_a, p_a, al_a), (s_b, mx_b, p_b, al_b) = sets[1:3]
        scores(keys(0), s_a, mx_a)
        softmax(s_a, mx_a, p_a, al_a)
        scores(keys(1), s_b, mx_b)

        def body(jj, carry):
            j = 2 * jj
            softmax(s_b, mx_b, p_b, al_b)
            scores(keys(j + 2), s_a, mx_a)
            weigh(vtl_ref[j], p_a, al_a)
            softmax(s_a, mx_a, p_a, al_a)
            scores(keys(j + 3), s_b, mx_b)
            weigh(vtl_ref[j + 1], p_b, al_b)
            return carry

        lax.fori_loop(0, n_lat // 2 - 1, body, 0)
        softmax(s_b, mx_b, p_b, al_b)
        weigh(vtl_ref[n_lat - 2], p_a, al_a)
        weigh(vtl_ref[n_lat - 1], p_b, al_b)
    o = (a1[0:LANES, :] / a1[LANES:LANES + 1, :]
         - lam_ref[0] * (a2[0:LANES, :] / a2[LANES:LANES + 1, :]))
    o = o * lax.rsqrt(jnp.mean(o * o, axis=0, keepdims=True) + EPS) * sg_ref[...] * out_scale
    o_ref[...] = o.T.astype(o_ref.dtype)


def _values_t(v):
    n, tk, d = v.shape
    nh = d // LANES
    vt = v.transpose(0, 2, 1).reshape(n, nh, LANES, tk)
    ones = jnp.ones((n, nh, 1, tk), v.dtype)
    pad = jnp.zeros((n, nh, V_EXT - LANES - 1, tk), v.dtype)
    return jnp.concatenate([vt, ones, pad], axis=2).reshape(n, nh * V_EXT, tk)


def _diff_attention(qk, vt_lat, vt_ctx, lam, sub_g, *, b, t, d, lambda_init, latent):
    nh = d // LANES
    ctx0 = b * t // CHUNK
    n_lat = t // KV_CHUNK if latent else 0
    tq = Q_TILE if latent else CHUNK
    nq = t // tq if latent else 1
    q_row = (lambda bi, c: bi * nq + c) if latent else (lambda bi, c: ctx0 + bi)
    in_specs = [pl.BlockSpec((tq, LANES), lambda bi, h, c, lam_r: (q_row(bi, c), h)),
                pl.BlockSpec((CHUNK, LANES), lambda bi, h, c, lam_r: (ctx0 + bi, nh + h)),
                pl.BlockSpec((None, V_EXT, CHUNK), lambda bi, h, c, lam_r: (bi, h, 0))]
    args = [qk, qk, vt_ctx]
    if latent:
        in_specs += [pl.BlockSpec((t, LANES), lambda bi, h, c, lam_r: (bi, nh + h)),
                     pl.BlockSpec((n_lat, V_EXT, KV_CHUNK), lambda bi, h, c, lam_r: (bi, h, 0))]
        args += [qk, vt_lat]
    in_specs.append(pl.BlockSpec((LANES, 1), lambda bi, h, c, lam_r: (0, 0)))
    args.append(sub_g.reshape(LANES, 1).astype(F32))
    stat = pltpu.VMEM((1, tq), F32)
    acc = pltpu.VMEM((V_EXT, tq), F32)

    def buffer_set(tk):
        return ([pltpu.VMEM((tk, tq), F32)] * 2 + [pltpu.VMEM((8, tq), F32)] * 2
                + [pltpu.VMEM((tk, tq), BF16)] * 2 + [stat] * 2)

    grid_spec = pltpu.PrefetchScalarGridSpec(
        num_scalar_prefetch=1,
        grid=(b, nh, nq),
        in_specs=in_specs,
        out_specs=pl.BlockSpec((tq, LANES), lambda bi, h, c, lam_r: (bi * nq + c, h)),
        scratch_shapes=[stat, acc, stat, acc] + buffer_set(CHUNK) + (buffer_set(KV_CHUNK) * 2 if latent else []))
    return pl.pallas_call(
        functools.partial(_attn_kernel, n_lat=n_lat, out_scale=1.0 - lambda_init),
        grid_spec=grid_spec,
        out_shape=jax.ShapeDtypeStruct((b * nq * tq, d), BF16),
        compiler_params=_cparams(("arbitrary", "arbitrary", "arbitrary")),
    )(lam, *args)


def _out_kernel(*refs, mlstm):
    if mlstm:
        (rf_ref, rb_ref, op_ref, og_ref, w_ref, x_ref, g2_ref, mod_ref, rwh_ref, rwl_ref, rb2_ref,
         xo_ref, h_ref, lg_ref) = refs
        r = rf_ref[...] + rb_ref[...]
        dh = r.shape[1] // M_HEADS
        parts = []
        for hh in range(M_HEADS):
            rr = r[:, hh * dh:(hh + 1) * dh]
            parts.append(rr * lax.rsqrt(jnp.mean(rr * rr, axis=-1, keepdims=True) + EPS))
        rn = jnp.concatenate(parts, axis=1) * og_ref[...]
        u = (jax.nn.sigmoid(op_ref[...].astype(F32)) * rn).astype(BF16)
    else:
        (u_ref, w_ref, x_ref, g2_ref, mod_ref, rwh_ref, rwl_ref, rb2_ref, xo_ref, h_ref, lg_ref) = refs
        u = u_ref[...]
    y = jnp.dot(u, w_ref[...], preferred_element_type=F32)
    x = x_ref[...] + mod_ref[2:3, :] * y
    xo_ref[...] = x
    hn = x * lax.rsqrt(jnp.mean(x * x, axis=-1, keepdims=True) + EPS) * g2_ref[...]
    h = hn * (1.0 + mod_ref[4:5, :]) + mod_ref[3:4, :]
    h_ref[...] = h
    h_hi = h.astype(BF16)
    h_lo = (h - h_hi.astype(F32)).astype(BF16)
    lg_ref[...] = (jnp.dot(h_hi, rwh_ref[...], preferred_element_type=F32)
                   + jnp.dot(h_lo, rwh_ref[...], preferred_element_type=F32)
                   + jnp.dot(h_hi, rwl_ref[...], preferred_element_type=F32) + rb2_ref[...])


def _out_proj(u_args, w_out, x, g2, mods, rw_hi, rw_lo, r_bias, *, b, t, mlstm):
    ntot, d = x.shape
    tm = ROW_TILE
    row = lambda i: (i, 0)
    const = lambda i: (0, 0)
    if mlstm:
        rf, rb, p, out_g = u_args
        u_specs = [pl.BlockSpec((tm, d), row), pl.BlockSpec((tm, d), row),
                   pl.BlockSpec((tm, d), lambda i: (i, 3)), pl.BlockSpec((1, d), const)]
        u_in = [rf, rb, p, out_g.reshape(1, d)]
    else:
        u_specs = [pl.BlockSpec((tm, d), row)]
        u_in = list(u_args)
    in_specs = u_specs + [pl.BlockSpec((d, d), const),
                          pl.BlockSpec((tm, d), row),
                          pl.BlockSpec((1, d), const),
                          _mod_spec(d, t // tm, b * t // tm, b),
                          pl.BlockSpec((d, LANES), const),
                          pl.BlockSpec((d, LANES), const),
                          pl.BlockSpec((1, LANES), const)]
    return pl.pallas_call(
        functools.partial(_out_kernel, mlstm=mlstm),
        grid=(ntot // tm,),
        in_specs=in_specs,
        out_specs=[pl.BlockSpec((tm, d), row), pl.BlockSpec((tm, d), row), pl.BlockSpec((tm, LANES), row)],
        out_shape=[jax.ShapeDtypeStruct((ntot, d), F32), jax.ShapeDtypeStruct((ntot, d), F32),
                   jax.ShapeDtypeStruct((ntot, LANES), F32)],
        compiler_params=_cparams(("arbitrary",)),
    )(*u_in, w_out, x, g2.reshape(1, d), mods, rw_hi, rw_lo, r_bias)


def _lane_pick(lane, pairs):
    out = jnp.zeros(lane.shape, F32)
    for idx, val in pairs:
        out = jnp.where(lane == idx, val, out)
    return out


def _route_kernel(lg_ref, rt_ref, cnt_ref):
    @pl.when(pl.program_id(0) == 0)
    def _():
        cnt_ref[...] = jnp.zeros(cnt_ref.shape, F32)

    lg = lg_ref[...]
    lane = lax.broadcasted_iota(I32, lg.shape, 1)
    lane_f = lane.astype(F32)
    big = 1e9
    is_g = lane < N_GROUPS
    gl = jnp.where(is_g, lg, NEG)
    gmax = jnp.max(gl, axis=1, keepdims=True)
    gidx = jnp.min(jnp.where(gl == gmax, lane_f, big), axis=1, keepdims=True)
    gsum = jnp.sum(jnp.where(is_g, jnp.exp(gl - gmax), 0.0), axis=1, keepdims=True)
    gp_top = 1.0 / gsum
    e_lane = lane_f - float(N_GROUPS)
    lo = gidx * float(EXPERTS_PER_GROUP)
    in_grp = jnp.where(e_lane >= lo, jnp.where(e_lane < lo + float(EXPERTS_PER_GROUP), 1.0, 0.0), 0.0) > 0.5
    el = jnp.where(in_grp, lg, NEG)
    v0 = jnp.max(el, axis=1, keepdims=True)
    i0 = jnp.min(jnp.where(el == v0, lane_f, big), axis=1, keepdims=True)
    el2 = jnp.where(lane_f == i0, NEG, el)
    v1 = jnp.max(el2, axis=1, keepdims=True)
    i1 = jnp.min(jnp.where(el2 == v1, lane_f, big), axis=1, keepdims=True)
    e1 = jnp.exp(v1 - v0)
    g0 = gp_top / (1.0 + e1)
    g1 = gp_top * e1 / (1.0 + e1)
    eid0 = i0 - float(N_GROUPS)
    eid1 = i1 - float(N_GROUPS)
    rt_ref[...] = _lane_pick(lane, ((0, eid0), (1, eid1), (2, g0), (3, g1)))
    hot = jnp.where(lane_f == eid0, 1.0, 0.0) + jnp.where(lane_f == eid1, 1.0, 0.0)
    cnt_ref[0:1, :] += jnp.sum(hot, axis=0, keepdims=True)


def _route(logits):
    ntot = logits.shape[0]
    return pl.pallas_call(
        _route_kernel,
        grid=(ntot // CHUNK,),
        in_specs=[pl.BlockSpec((CHUNK, LANES), lambda i: (i, 0))],
        out_specs=[pl.BlockSpec((CHUNK, LANES), lambda i: (i, 0)), pl.BlockSpec((8, LANES), lambda i: (0, 0))],
        out_shape=[jax.ShapeDtypeStruct((ntot, LANES), F32), jax.ShapeDtypeStruct((8, LANES), F32)],
        compiler_params=_cparams(("arbitrary",)),
    )(logits)


def _slot_kernel(rt_ref, cnt_ref, o_ref, run_ref):
    @pl.when(pl.program_id(0) == 0)
    def _():
        cnt = cnt_ref[0:1, :].astype(I32)
        padded = (((cnt + (MOE_BLOCK - 1)) // MOE_BLOCK) * MOE_BLOCK).astype(F32)
        r_i = lax.broadcasted_iota(I32, (LANES, LANES), 0)
        c_i = lax.broadcasted_iota(I32, (LANES, LANES), 1)
        col = jnp.sum(jnp.where(r_i == c_i, jnp.broadcast_to(padded, (LANES, LANES)), 0.0), axis=1, keepdims=True)
        run_ref[...] = jnp.sum(jnp.where(r_i < c_i, col, 0.0), axis=0, keepdims=True)

    rt = rt_ref[...]
    lane = lax.broadcasted_iota(I32, rt.shape, 1)
    lane_f = lane.astype(F32)
    hot0 = lane_f == rt[:, 0:1]
    hot1 = lane_f == rt[:, 1:2]
    both = jnp.where(hot0, 1.0, 0.0) + jnp.where(hot1, 1.0, 0.0)
    t_i = lax.broadcasted_iota(I32, (CHUNK, CHUNK), 0)
    s_i = lax.broadcasted_iota(I32, (CHUNK, CHUNK), 1)
    before = jnp.where(s_i < t_i, 1.0, 0.0).astype(BF16)
    base = run_ref[...] + jnp.dot(before, both.astype(BF16), preferred_element_type=F32)
    d0 = jnp.sum(jnp.where(hot0, base, 0.0), axis=1, keepdims=True)
    d1 = jnp.sum(jnp.where(hot1, base, 0.0), axis=1, keepdims=True)
    o_ref[...] = _lane_pick(lane, ((0, d0), (1, d1))).astype(I32)
    run_ref[...] += jnp.sum(both, axis=0, keepdims=True)


def _slots(route, counts):
    ntot = route.shape[0]
    return pl.pallas_call(
        _slot_kernel,
        grid=(ntot // CHUNK,),
        in_specs=[pl.BlockSpec((CHUNK, LANES), lambda i: (i, 0)), pl.BlockSpec((8, LANES), lambda i: (0, 0))],
        out_specs=pl.BlockSpec((CHUNK, LANES), lambda i: (i, 0)),
        out_shape=jax.ShapeDtypeStruct((ntot, LANES), I32),
        scratch_shapes=[pltpu.VMEM((1, LANES), F32)],
        compiler_params=_cparams(("arbitrary",)),
    )(route, counts)


def _dispatch_kernel(dest_ref, h_ref, xs_in_ref, xs_ref, sem):
    del xs_in_ref

    def row_copy(r, k):
        return pltpu.make_async_copy(h_ref.at[pl.ds(r, 1), :], xs_ref.at[pl.ds(dest_ref[0, 2 * r + k], 1), :], sem)

    def start(r, carry):
        row_copy(r, 0).start(priority=0)
        row_copy(r, 1).start(priority=1)
        return carry

    def wait(r, carry):
        row_copy(r, 0).wait()
        row_copy(r, 1).wait()
        return carry

    lax.fori_loop(0, CHUNK, start, 0, unroll=DMA_UNROLL)
    lax.fori_loop(0, CHUNK, wait, 0, unroll=DMA_UNROLL)


def _dispatch(dest, h, buf):
    ntot, d = h.shape
    cap = buf.shape[0]
    return pl.pallas_call(
        _dispatch_kernel,
        grid=(ntot // CHUNK,),
        in_specs=[pl.BlockSpec((None, 1, 2 * CHUNK), lambda i: (i, 0, 0), memory_space=pltpu.SMEM),
                  pl.BlockSpec((CHUNK, d), lambda i: (i, 0)),
                  pl.BlockSpec(memory_space=pl.ANY)],
        out_specs=pl.BlockSpec(memory_space=pl.ANY),
        out_shape=jax.ShapeDtypeStruct((cap, d), F32),
        scratch_shapes=[pltpu.SemaphoreType.DMA],
        input_output_aliases={2: 0},
        compiler_params=_cparams(("arbitrary",)),
    )(dest, h, buf)


def _expert_kernel(be_ref, nu_ref, xs_ref, wgu_ref, wdn_ref, ys_ref, wgu_bf, wdn_bf):
    j = pl.program_id(0)
    changed = jnp.logical_or(j == 0, be_ref[j] != be_ref[jnp.maximum(j - 1, 0)])

    @pl.when(changed)
    def _():
        wgu_bf[...] = wgu_ref[...].astype(BF16)
        wdn_bf[...] = wdn_ref[...].astype(BF16)

    @pl.when(j < nu_ref[0])
    def _():
        gu = jnp.dot(xs_ref[...].astype(BF16), wgu_bf[...], preferred_element_type=F32)
        g = gu[:, :D_EXPERT]
        a = (g * jax.nn.sigmoid(g) * gu[:, D_EXPERT:]).astype(BF16)
        ys_ref[...] = jnp.dot(a, wdn_bf[...], preferred_element_type=F32)

    @pl.when(j >= nu_ref[0])
    def _():
        ys_ref[...] = jnp.zeros(ys_ref.shape, ys_ref.dtype)


def _experts(blk_e, n_used, xs, w_gu, w_dn, layer):
    cap, d = xs.shape
    row = lambda j, be, nu: (jnp.minimum(j, nu[0] - 1), 0)
    grid_spec = pltpu.PrefetchScalarGridSpec(
        num_scalar_prefetch=2,
        grid=(cap // MOE_BLOCK,),
        in_specs=[pl.BlockSpec((MOE_BLOCK, d), row),
                  pl.BlockSpec((None, None, d, 2 * D_EXPERT), lambda j, be, nu: (layer, be[j], 0, 0)),
                  pl.BlockSpec((None, None, D_EXPERT, d), lambda j, be, nu: (layer, be[j], 0, 0))],
        out_specs=pl.BlockSpec((MOE_BLOCK, d), lambda j, be, nu: (j, 0)),
        scratch_shapes=[pltpu.VMEM((d, 2 * D_EXPERT), BF16), pltpu.VMEM((D_EXPERT, d), BF16)])
    return pl.pallas_call(
        _expert_kernel,
        grid_spec=grid_spec,
        out_shape=jax.ShapeDtypeStruct((cap, d), F32),
        compiler_params=_cparams(("arbitrary",)),
    )(blk_e, n_used, xs, w_gu, w_dn)


def _combine_kernel(dest_ref, rt_ref, x_ref, mod_ref, ys_ref, xo_ref, buf, sem):
    def row_copy(r, k):
        return pltpu.make_async_copy(ys_ref.at[pl.ds(dest_ref[0, 2 * r + k], 1), :],
                                     buf.at[pl.ds(k * CHUNK + r, 1), :], sem)

    def start(r, carry):
        row_copy(r, 0).start(priority=0)
        row_copy(r, 1).start(priority=1)
        return carry

    def wait(r, carry):
        row_copy(r, 0).wait()
        row_copy(r, 1).wait()
        return carry

    lax.fori_loop(0, CHUNK, start, 0, unroll=DMA_UNROLL)
    lax.fori_loop(0, CHUNK, wait, 0, unroll=DMA_UNROLL)
    rt = rt_ref[...]
    y = rt[:, 2:3] * buf[0:CHUNK, :] + rt[:, 3:4] * buf[CHUNK:2 * CHUNK, :]
    xo_ref[...] = x_ref[...] + mod_ref[5:6, :] * y


def _combine(dest, route, x, mods, ys, *, b, t, n_rows):
    d = x.shape[1]
    ncl = t // CHUNK
    return pl.pallas_call(
        _combine_kernel,
        grid=(n_rows // CHUNK,),
        in_specs=[pl.BlockSpec((None, 1, 2 * CHUNK), lambda i: (i, 0, 0), memory_space=pltpu.SMEM),
                  pl.BlockSpec((CHUNK, LANES), lambda i: (i, 0)),
                  pl.BlockSpec((CHUNK, d), lambda i: (i, 0)),
                  _mod_spec(d, ncl, b * ncl, b),
                  pl.BlockSpec(memory_space=pl.ANY)],
        out_specs=pl.BlockSpec((CHUNK, d), lambda i: (i, 0)),
        out_shape=jax.ShapeDtypeStruct((n_rows, d), F32),
        scratch_shapes=[pltpu.VMEM((2 * CHUNK, d), F32), pltpu.SemaphoreType.DMA],
        compiler_params=_cparams(("arbitrary",)),
    )(dest, route, x, mods, ys)


def _moe_capacity(ntot):
    return -(-2 * ntot // MOE_BLOCK) * MOE_BLOCK + N_EXPERTS * MOE_BLOCK


def _moe(x, h, logits, mods, w_gu, w_dn, layer, slot_buf, *, b, t, n_rows):
    ntot = x.shape[0]
    route, counts = _route(logits)
    slot = _slots(route, counts)
    dest = slot[:, :2].reshape(ntot // CHUNK, 1, 2 * CHUNK)
    nblk = slot_buf.shape[0] // MOE_BLOCK
    cnt = counts[0, :N_EXPERTS].astype(I32)
    pad_end = jnp.cumsum((cnt + MOE_BLOCK - 1) // MOE_BLOCK * MOE_BLOCK)
    n_used = pad_end[N_EXPERTS - 1:] // MOE_BLOCK
    blk_start = jnp.minimum(jnp.arange(nblk, dtype=I32), n_used - 1) * MOE_BLOCK
    blk_e = jnp.minimum(jnp.sum((pad_end[None, :] <= blk_start[:, None]).astype(I32), axis=1), N_EXPERTS - 1)
    xs = _dispatch(dest, h, slot_buf)
    ys = _experts(blk_e.astype(I32), n_used.astype(I32), xs, w_gu, w_dn, layer)
    return _combine(dest, route, x, mods, ys, b=b, t=t, n_rows=n_rows), xs


def kernel(x, c, ctx, c_ctx, ada_w, ada_b, norm1_g, norm2_g, m_w_in, m_conv_w, m_conv_b, m_gate_b, m_out_g,
           m_w_out, d_w_in, d_q_g, d_k_g, d_lq1, d_lk1, d_lq2, d_lk2, d_sub_g, d_w_out, r_grp_w, r_grp_b,
           r_exp_w, r_exp_b, e_w_gu, e_w_dn):
    b, t, d = x.shape
    cl = ctx.shape[1]
    depth = ada_w.shape[0]
    assert cl == CHUNK and d == 8 * LANES and b + 1 <= 8
    assert t % Q_TILE == 0 and t % (4 * KV_CHUNK) == 0 and t % ROW_TILE == 0 and (b * cl) % ROW_TILE == 0
    n_lat = b * t
    ntot = n_lat + b * cl

    xs = jnp.concatenate([x.reshape(n_lat, d), ctx.reshape(b * cl, d)], axis=0)
    cond = jnp.zeros((8, d), F32).at[:b].set(c).at[b].set(c_ctx)
    mods_all = _modulation(cond, ada_w, ada_b)
    cos_t, sin_t = _rope_tables(t)
    slot_buf = jnp.zeros((_moe_capacity(ntot), d), F32)

    n_route = N_GROUPS + N_EXPERTS
    for i in range(depth):
        jm = i // 2
        mods = mods_all[i]
        rw = jnp.zeros((d, LANES), F32).at[:, :N_GROUPS].set(r_grp_w[i]).at[:, N_GROUPS:n_route].set(r_exp_w[i])
        rw_hi = rw.astype(BF16)
        rw_lo = (rw - rw_hi.astype(F32)).astype(BF16)
        r_bias = jnp.zeros((1, LANES), F32).at[0, :N_GROUPS].set(r_grp_b[i]).at[0, N_GROUPS:n_route].set(r_exp_b[i])
        if i % 2 == 0:
            w_in = m_w_in[jm]
            wg = jnp.zeros((d, LANES), F32).at[:, :4 * M_HEADS].set(w_in[:, 4 * d:]).astype(BF16)
            gb = jnp.zeros((1, LANES), F32).at[0, :4 * M_HEADS].set(m_gate_b[jm])
            p, gates = _in_proj(xs, norm1_g[i], mods, w_in[:, :4 * d].astype(BF16), b=b, t=t, tn=1024,
                                gates=(wg, gb))
            qk = _mlstm_conv(p, m_conv_w[jm], m_conv_b[jm], b=b, t=t, d=d)
            g4 = gates[:, :4 * M_HEADS].reshape(ntot, 4, M_HEADS)
            rf, rb = _mlstm_scan(qk, p, g4.transpose(2, 0, 1), g4.transpose(2, 1, 0), b=b, t=t, d=d)
            xs, h, logits = _out_proj((rf, rb, p, m_out_g[jm]), m_w_out[jm].astype(BF16), xs, norm2_g[i], mods,
                                      rw_hi, rw_lo, r_bias, b=b, t=t, mlstm=True)
        else:
            lambda_init = 0.8 - 0.6 * math.exp(-0.3 * i)
            p = _in_proj(xs, norm1_g[i], mods, d_w_in[jm].astype(BF16), b=b, t=t, tn=1024)
            qk = _diff_qkprep(p, d_q_g[jm], d_k_g[jm], cos_t, sin_t, b=b, t=t, d=d)
            lam = (jnp.exp(jnp.sum(d_lq1[jm] * d_lk1[jm])) - jnp.exp(jnp.sum(d_lq2[jm] * d_lk2[jm]))
                   + lambda_init).reshape(1).astype(F32)
            v = p[:, 2 * d:]
            vt_lat = _values_t(v[:n_lat].reshape(n_lat // KV_CHUNK, KV_CHUNK, d))
            vt_ctx = _values_t(v[n_lat:].reshape(b, cl, d))
            attn = functools.partial(_diff_attention, qk, vt_lat, vt_ctx, lam, d_sub_g[jm], b=b, t=t, d=d,
                                     lambda_init=lambda_init)
            o = jnp.concatenate([attn(latent=True), attn(latent=False)], axis=0)
            xs, h, logits = _out_proj((o,), d_w_out[jm].astype(BF16), xs, norm2_g[i], mods,
                                      rw_hi, rw_lo, r_bias, b=b, t=t, mlstm=False)
        last = i == depth - 1
        xs, slot_buf = _moe(xs, h, logits, mods, e_w_gu, e_w_dn, i, slot_buf, b=b, t=t,
                            n_rows=n_lat if last else ntot)
    return xs.reshape(b, t, d)
```

```python
import functools
import math

import jax
import jax.numpy as jnp
from jax import lax
from jax.experimental import pallas as pl
from jax.experimental.pallas import tpu as pltpu

F32 = jnp.float32
BF16 = jnp.bfloat16
I32 = jnp.int32
HIGHEST = lax.Precision.HIGHEST

EPS = 1e-6
GRID_W = 64
ROPE_BASE = 10000.0
M_HEADS = 4
M_CONV_W = 5
D_HEADS = 8
D_HEAD_DIM = 64
N_GROUPS = 4
EXPERTS_PER_GROUP = 8
N_EXPERTS = 32
D_EXPERT = 512

LANES = 128
CHUNK = 256
ROW_TILE = 512
Q_TILE = 512
KV_CHUNK = 1024
SOFTMAX_ROWS = 64
V_EXT = LANES + 16
MOE_BLOCK = 256
DMA_UNROLL = 8
NEG = -1e30
VMEM_LIMIT = 56 * 1024 * 1024


def _cparams(sem):
    return pltpu.CompilerParams(dimension_semantics=sem, vmem_limit_bytes=VMEM_LIMIT)


def _nt_dot(a, b):
    return lax.dot_general(a, b, (((1,), (1,)), ((), ())), preferred_element_type=F32)


def _tn_dot(a, b):
    return lax.dot_general(a, b, (((0,), (0,)), ((), ())), preferred_element_type=F32)


def _mod_spec(d, tiles_per_batch, n_lat_tiles, b):
    return pl.BlockSpec((None, 6, d), lambda i, *_: (jnp.where(i < n_lat_tiles, i // tiles_per_batch, b), 0, 0))


def _mod_kernel(s_ref, w_ref, b_ref, o_ref):
    s = s_ref[...]
    s = s * jax.nn.sigmoid(s)
    o_ref[...] = jnp.dot(s, w_ref[...], precision=HIGHEST, preferred_element_type=F32) + b_ref[...]


def _modulation(cond, ada_w, ada_b):
    depth, d, n = ada_w.shape
    tn = n // 4
    out = pl.pallas_call(
        _mod_kernel,
        grid=(depth, n // tn),
        in_specs=[pl.BlockSpec((8, d), lambda l, j: (0, 0)),
                  pl.BlockSpec((None, d, tn), lambda l, j: (l, 0, j)),
                  pl.BlockSpec((None, 1, tn), lambda l, j: (l, 0, j))],
        out_specs=pl.BlockSpec((None, 8, tn), lambda l, j: (l, 0, j)),
        out_shape=jax.ShapeDtypeStruct((depth, 8, n), F32),
        compiler_params=_cparams(("arbitrary", "arbitrary")),
    )(cond, ada_w, ada_b.reshape(depth, 1, n))
    return out.reshape(depth, 8, 6, d)


def _in_kernel(*refs, has_gates, tn):
    if has_gates:
        x_ref, g_ref, mod_ref, w_ref, wg_ref, gb_ref, o_ref, go_ref, h_ref = refs
    else:
        x_ref, g_ref, mod_ref, w_ref, o_ref, h_ref = refs

    x = x_ref[...]
    y = x * lax.rsqrt(jnp.mean(x * x, axis=-1, keepdims=True) + EPS) * g_ref[...]
    hb = (y * (1.0 + mod_ref[1:2, :]) + mod_ref[0:1, :]).astype(BF16)
    h_ref[...] = hb
    if has_gates:
        go_ref[...] = jnp.dot(hb, wg_ref[...], preferred_element_type=F32) + gb_ref[...]
    for c0 in range(0, w_ref.shape[1], tn):
        o_ref[:, c0:c0 + tn] = jnp.dot(h_ref[...], w_ref[:, c0:c0 + tn],
                                       preferred_element_type=F32).astype(o_ref.dtype)


def _in_proj(x, g, mods, w, *, b, t, tn, gates=None):
    ntot, d = x.shape
    n = w.shape[1]
    tm = ROW_TILE
    has_gates = gates is not None
    in_specs = [pl.BlockSpec((tm, d), lambda i: (i, 0)),
                pl.BlockSpec((1, d), lambda i: (0, 0)),
                _mod_spec(d, t // tm, b * t // tm, b),
                pl.BlockSpec((d, n), lambda i: (0, 0))]
    args = [x, g.reshape(1, d), mods, w]
    out_specs = pl.BlockSpec((tm, n), lambda i: (i, 0))
    out_shape = jax.ShapeDtypeStruct((ntot, n), BF16)
    if has_gates:
        wg, gb = gates
        in_specs += [pl.BlockSpec((d, LANES), lambda i: (0, 0)),
                     pl.BlockSpec((1, LANES), lambda i: (0, 0))]
        args += [wg, gb]
        out_specs = [out_specs, pl.BlockSpec((tm, LANES), lambda i: (i, 0))]
        out_shape = [out_shape, jax.ShapeDtypeStruct((ntot, LANES), F32)]
    return pl.pallas_call(
        functools.partial(_in_kernel, has_gates=has_gates, tn=tn),
        grid=(ntot // tm,),
        in_specs=in_specs,
        out_specs=out_specs,
        out_shape=out_shape,
        scratch_shapes=[pltpu.VMEM((tm, d), BF16)],
        compiler_params=_cparams(("arbitrary",)),
    )(*args)


def _conv_kernel(xm_ref, xp_ref, xn_ref, w_ref, b_ref, o_ref, *, ncl, n_lat_chunks, ncol_q, k_scale):
    i = pl.program_id(0)
    j = pl.program_id(1)
    c = i % ncl
    is_lat = i < n_lat_chunks
    left_ok = jnp.logical_and(is_lat, c >= 1)
    right_ok = jnp.logical_and(is_lat, c < ncl - 1)
    xm = xm_ref[...].astype(F32)
    xp = jnp.where(left_ok, xp_ref[...].astype(F32), 0.0)[14:16, :]
    xn = jnp.where(right_ok, xn_ref[...].astype(F32), 0.0)[0:2, :]
    ext = jnp.concatenate([xp, xm, xn], axis=0)
    rows = xm.shape[0]
    w = w_ref[...]
    y = b_ref[...] + w[0:1, :] * ext[0:rows, :]
    for k in range(1, M_CONV_W):
        y = y + w[k:k + 1, :] * ext[k:k + rows, :]
    y = y * jax.nn.sigmoid(y)
    y = y * jnp.where(j >= ncol_q, k_scale, 1.0)
    o_ref[...] = y.astype(o_ref.dtype)


def _mlstm_conv(p, conv_w, conv_b, *, b, t, d):
    ntot = p.shape[0]
    tn = 512
    nrb = ntot // 16
    return pl.pallas_call(
        functools.partial(_conv_kernel, ncl=t // CHUNK, n_lat_chunks=b * t // CHUNK, ncol_q=d // tn,
                          k_scale=(d // M_HEADS) ** -0.5),
        grid=(ntot // CHUNK, 2 * d // tn),
        in_specs=[pl.BlockSpec((CHUNK, tn), lambda i, j: (i, j)),
                  pl.BlockSpec((16, tn), lambda i, j: (jnp.maximum(i * (CHUNK // 16) - 1, 0), j)),
                  pl.BlockSpec((16, tn), lambda i, j: (jnp.minimum((i + 1) * (CHUNK // 16), nrb - 1), j)),
                  pl.BlockSpec((M_CONV_W, tn), lambda i, j: (0, j)),
                  pl.BlockSpec((1, tn), lambda i, j: (0, j))],
        out_specs=pl.BlockSpec((CHUNK, tn), lambda i, j: (i, j)),
        out_shape=jax.ShapeDtypeStruct((ntot, 2 * d), BF16),
        compiler_params=_cparams(("arbitrary", "arbitrary")),
    )(p, p, p, conv_w, conv_b.reshape(1, 2 * d))


def _log_sigmoid(x):
    return jnp.minimum(x, 0.0) - jnp.log(1.0 + jnp.exp(-jnp.abs(x)))


def _mlstm_chunk(q, k, v, gcol, grow, igate, fgate, mask, mask_t, ct_ref, n_ref, m_ref, o_ref):
    i_row = grow[igate:igate + 1, :]
    i_col = gcol[:, igate:igate + 1]
    f_row = _log_sigmoid(grow[fgate:fgate + 1, :])
    f_col = _log_sigmoid(gcol[:, fgate:fgate + 1])
    cum_col = jnp.sum(jnp.where(mask, f_row, 0.0), axis=1, keepdims=True)
    cum_row = jnp.sum(jnp.where(mask_t, f_col, 0.0), axis=0, keepdims=True)
    tot = jnp.sum(f_col, axis=0, keepdims=True)
    m_prev = m_ref[...]
    dmat = jnp.where(mask, cum_col - cum_row + i_row, NEG)
    m_inter = cum_col + m_prev
    m_out = jnp.maximum(m_inter, jnp.max(dmat, axis=1, keepdims=True))
    w_intra = jnp.exp(dmat - m_out)
    w_inter = jnp.exp(m_inter - m_out)
    sc = _nt_dot(q, k) * w_intra
    ct = ct_ref[...]
    n_row = n_ref[...]
    qf = q.astype(F32)
    kf = k.astype(F32)
    vf = v.astype(F32)
    num = (w_inter * jnp.dot(q, ct.astype(BF16), preferred_element_type=F32)
           + jnp.dot(sc.astype(BF16), v, preferred_element_type=F32))
    den = w_inter * jnp.sum(qf * n_row, axis=1, keepdims=True) + jnp.sum(sc, axis=1, keepdims=True)
    o_ref[...] = (num / jnp.maximum(jnp.abs(den), jnp.exp(-m_out))).astype(o_ref.dtype)
    g_col = tot - cum_col + i_col
    m_next = jnp.maximum(tot + m_prev, jnp.max(g_col, axis=0, keepdims=True))
    decay = jnp.exp(tot + m_prev - m_next)
    wg = jnp.exp(g_col - m_next)
    ct_ref[...] = decay * ct + _tn_dot(k, (wg * vf).astype(BF16))
    n_ref[...] = decay * n_row + jnp.sum(wg * kf, axis=0, keepdims=True)
    m_ref[...] = m_next


def _scan_kernel(qf_ref, kf_ref, vf_ref, gcf_ref, grf_ref, qb_ref, kb_ref, vb_ref, gcb_ref, grb_ref,
                 of_ref, ob_ref, ctf_ref, nf_ref, mf_ref, ctb_ref, nb_ref, mb_ref):
    @pl.when(pl.program_id(2) == 0)
    def _():
        for r in (ctf_ref, nf_ref, mf_ref, ctb_ref, nb_ref, mb_ref):
            r[...] = jnp.zeros(r.shape, r.dtype)

    t_i = lax.broadcasted_iota(I32, (CHUNK, CHUNK), 0)
    s_i = lax.broadcasted_iota(I32, (CHUNK, CHUNK), 1)
    lower = s_i <= t_i
    upper = s_i >= t_i
    _mlstm_chunk(qf_ref[...], kf_ref[...], vf_ref[...], gcf_ref[...], grf_ref[...], 0, 2,
                 lower, upper, ctf_ref, nf_ref, mf_ref, of_ref)
    _mlstm_chunk(qb_ref[...], kb_ref[...], vb_ref[...], gcb_ref[...], grb_ref[...], 1, 3,
                 upper, lower, ctb_ref, nb_ref, mb_ref, ob_ref)


def _mlstm_scan(qk, p, gcol, grow, *, b, t, d):
    ntot = qk.shape[0]
    dh = d // M_HEADS
    ncl = t // CHUNK
    ctx0 = b * ncl

    def fwd(bi, s):
        return jnp.where(s == 0, ctx0 + bi, bi * ncl + s - 1)

    def bwd(bi, s):
        return jnp.where(s == 0, ctx0 + bi, bi * ncl + ncl - s)

    def specs(order):
        return [pl.BlockSpec((CHUNK, dh), lambda bi, h, s: (order(bi, s), h)),
                pl.BlockSpec((CHUNK, dh), lambda bi, h, s: (order(bi, s), M_HEADS + h)),
                pl.BlockSpec((CHUNK, dh), lambda bi, h, s: (order(bi, s), 2 * M_HEADS + h)),
                pl.BlockSpec((None, CHUNK, 4), lambda bi, h, s: (h, order(bi, s), 0)),
                pl.BlockSpec((None, 4, CHUNK), lambda bi, h, s: (h, 0, order(bi, s)))]

    def ospec(order):
        return pl.BlockSpec((CHUNK, dh), lambda bi, h, s: (order(bi, s), h))

    state = [pltpu.VMEM((dh, dh), F32), pltpu.VMEM((1, dh), F32), pltpu.VMEM((1, 1), F32)]
    return pl.pallas_call(
        _scan_kernel,
        grid=(b, M_HEADS, ncl + 1),
        in_specs=specs(fwd) + specs(bwd),
        out_specs=[ospec(fwd), ospec(bwd)],
        out_shape=[jax.ShapeDtypeStruct((ntot, d), F32)] * 2,
        scratch_shapes=state + state,
        compiler_params=_cparams(("arbitrary", "arbitrary", "arbitrary")),
    )(qk, qk, p, gcol, grow, qk, qk, p, gcol, grow)


def _qkprep_kernel(p_ref, gq_ref, gk_ref, cos_ref, sin_ref, o_ref, *, n_lat_chunks, nblk_q, q_scale):
    is_ctx = pl.program_id(0) >= n_lat_chunks
    cos = jnp.where(is_ctx, 1.0, cos_ref[...])
    sin = jnp.where(is_ctx, 0.0, sin_ref[...])
    lane = lax.broadcasted_iota(I32, (CHUNK, LANES), 1)
    first_half = (lane % 32) < 16
    r_i = lax.broadcasted_iota(I32, (LANES, LANES), 0)
    c_i = lax.broadcasted_iota(I32, (LANES, LANES), 1)
    group = jnp.where((r_i // D_HEAD_DIM) == (c_i // D_HEAD_DIM), 1.0, 0.0).astype(BF16)
    for blk in range(2 * nblk_q):
        x = p_ref[:, blk * LANES:(blk + 1) * LANES].astype(F32)
        ss = x * x
        ss_hi = ss.astype(BF16)
        ss_lo = (ss - ss_hi.astype(F32)).astype(BF16)
        ssum = (jnp.dot(ss_hi, group, preferred_element_type=F32)
                + jnp.dot(ss_lo, group, preferred_element_type=F32))
        g = gq_ref[...] if blk < nblk_q else gk_ref[...]
        xn = x * lax.rsqrt(ssum * (1.0 / D_HEAD_DIM) + EPS) * g
        partner = jnp.where(first_half, pltpu.roll(xn, LANES - 16, 1), pltpu.roll(xn, 16, 1))
        y = xn * cos + partner * sin
        if blk < nblk_q:
            y = y * q_scale
        o_ref[:, blk * LANES:(blk + 1) * LANES] = y.astype(o_ref.dtype)


def _diff_qkprep(p, q_g, k_g, cos_t, sin_t, *, b, t, d):
    ntot = p.shape[0]
    ncl = t // CHUNK
    g2 = lambda g: jnp.tile(g.astype(F32), 2).reshape(1, LANES)
    return pl.pallas_call(
        functools.partial(_qkprep_kernel, n_lat_chunks=b * ncl, nblk_q=d // LANES,
                          q_scale=D_HEAD_DIM ** -0.5 * math.log2(math.e)),
        grid=(ntot // CHUNK,),
        in_specs=[pl.BlockSpec((CHUNK, 2 * d), lambda i: (i, 0)),
                  pl.BlockSpec((1, LANES), lambda i: (0, 0)),
                  pl.BlockSpec((1, LANES), lambda i: (0, 0)),
                  pl.BlockSpec((CHUNK, LANES), lambda i: (i % ncl, 0)),
                  pl.BlockSpec((CHUNK, LANES), lambda i: (i % ncl, 0))],
        out_specs=pl.BlockSpec((CHUNK, 2 * d), lambda i: (i, 0)),
        out_shape=jax.ShapeDtypeStruct((ntot, 2 * d), BF16),
        compiler_params=_cparams(("arbitrary",)),
    )(p, g2(q_g), g2(k_g), cos_t, sin_t)


def _rope_tables(t):
    rows = t // GRID_W
    row = jnp.repeat(jnp.arange(rows), GRID_W).astype(F32)[:, None]
    col = jnp.tile(jnp.arange(GRID_W), rows).astype(F32)[:, None]
    nf = D_HEAD_DIM // 4
    inv = ROPE_BASE ** (-jnp.arange(nf, dtype=F32) / nf)
    ar, ac = row * inv, col * inv
    cos64 = jnp.concatenate([jnp.cos(ar), jnp.cos(ar), jnp.cos(ac), jnp.cos(ac)], axis=-1)
    sin64 = jnp.concatenate([-jnp.sin(ar), jnp.sin(ar), -jnp.sin(ac), jnp.sin(ac)], axis=-1)
    return jnp.tile(cos64, (1, 2)), jnp.tile(sin64, (1, 2))


def _attn_kernel(*refs, n_lat, out_scale):
    if n_lat:
        lam_ref, q_ref, kc_ref, vtc_ref, kl_ref, vtl_ref, sg_ref, o_ref, m1, a1, m2, a2 = refs[:12]
        bufs = refs[12:]
    else:
        lam_ref, q_ref, kc_ref, vtc_ref, sg_ref, o_ref, m1, a1, m2, a2 = refs[:10]
        bufs = refs[10:]
    sets = [tuple(bufs[i + 2 * k:i + 2 * k + 2] for k in range(4)) for i in range(0, len(bufs), 8)]
    q = q_ref[...]
    lane = lax.broadcasted_iota(I32, q.shape, 1)
    zero = jnp.zeros_like(q)
    qs = (jnp.where(lane < D_HEAD_DIM, q, zero), jnp.where(lane >= D_HEAD_DIM, q, zero))
    maxes = (m1, m2)
    accs = (a1, a2)
    for m_r, a_r in zip(maxes, accs):
        m_r[...] = jnp.full(m_r.shape, NEG, F32)
        a_r[...] = jnp.zeros(a_r.shape, F32)

    def scores(keys, s_bufs, mx_bufs):
        for qm, s_ref, mx_ref in zip(qs, s_bufs, mx_bufs):
            st = _nt_dot(keys, qm)
            s_ref[...] = st
            mx = st[0:8]
            for r in range(8, st.shape[0], 8):
                mx = jnp.maximum(mx, st[r:r + 8])
            mx_ref[...] = mx

    def softmax(s_bufs, mx_bufs, p_bufs, al_bufs):
        for s_ref, mx_ref, p_ref, al_ref, m_r in zip(s_bufs, mx_bufs, p_bufs, al_bufs, maxes):
            m_old = m_r[...]
            m_new = jnp.maximum(m_old, jnp.max(mx_ref[...], axis=0, keepdims=True))
            m_r[...] = m_new
            al_ref[...] = jnp.exp2(m_old - m_new)
            for r0 in range(0, s_ref.shape[0], SOFTMAX_ROWS):
                p = jnp.exp2(s_ref[r0:r0 + SOFTMAX_ROWS, :] - m_new)
                p_ref[r0:r0 + SOFTMAX_ROWS, :] = p.astype(BF16)

    def weigh(vt, p_bufs, al_bufs):
        for p_ref, al_ref, a_r in zip(p_bufs, al_bufs, accs):
            a_r[...] = al_ref[...] * a_r[...] + jnp.dot(vt, p_ref[...], preferred_element_type=F32)

    s_c, mx_c, p_c, al_c = sets[0]
    scores(kc_ref[...], s_c, mx_c)
    softmax(s_c, mx_c, p_c, al_c)
    weigh(vtc_ref[...], p_c, al_c)

    if n_lat:
        def keys(j):
            return kl_ref[pl.ds(pl.multiple_of(j * KV_CHUNK, KV_CHUNK), KV_CHUNK), :]

        (s_a, mx_a, p_a, al_a), (s_b, mx_b, p_b, al_b) = sets[1:3]
        scores(keys(0), s_a, mx_a)
        softmax(s_a, mx_a, p_a, al_a)
        scores(keys(1), s_b, mx_b)

        def body(jj, carry):
            j = 2 * jj
            softmax(s_b, mx_b, p_b, al_b)
            scores(keys(j + 2), s_a, mx_a)
            weigh(vtl_ref[j], p_a, al_a)
            softmax(s_a, mx_a, p_a, al_a)
            scores(keys(j + 3), s_b, mx_b)
            weigh(vtl_ref[j + 1], p_b, al_b)
            return carry

        lax.fori_loop(0, n_lat // 2 - 1, body, 0)
        softmax(s_b, mx_b, p_b, al_b)
        weigh(vtl_ref[n_lat - 2], p_a, al_a)
        weigh(vtl_ref[n_lat - 1], p_b, al_b)
    o = (a1[0:LANES, :] / a1[LANES:LANES + 1, :]
         - lam_ref[0] * (a2[0:LANES, :] / a2[LANES:LANES + 1, :]))
    o = o * lax.rsqrt(jnp.mean(o * o, axis=0, keepdims=True) + EPS) * sg_ref[...] * out_scale
    o_ref[...] = o.T.astype(o_ref.dtype)


def _values_t(v):
    n, tk, d = v.shape
    nh = d // LANES
    vt = v.transpose(0, 2, 1).reshape(n, nh, LANES, tk)
    ones = jnp.ones((n, nh, 1, tk), v.dtype)
    pad = jnp.zeros((n, nh, V_EXT - LANES - 1, tk), v.dtype)
    return jnp.concatenate([vt, ones, pad], axis=2).reshape(n, nh * V_EXT, tk)


def _diff_attention(qk, vt_lat, vt_ctx, lam, sub_g, *, b, t, d, lambda_init, latent):
    nh = d // LANES
    ctx0 = b * t // CHUNK
    n_lat = t // KV_CHUNK if latent else 0
    tq = Q_TILE if latent else CHUNK
    nq = t // tq if latent else 1
    q_row = (lambda bi, c: bi * nq + c) if latent else (lambda bi, c: ctx0 + bi)
    in_specs = [pl.BlockSpec((tq, LANES), lambda bi, h, c, lam_r: (q_row(bi, c), h)),
                pl.BlockSpec((CHUNK, LANES), lambda bi, h, c, lam_r: (ctx0 + bi, nh + h)),
                pl.BlockSpec((None, V_EXT, CHUNK), lambda bi, h, c, lam_r: (bi, h, 0))]
    args = [qk, qk, vt_ctx]
    if latent:
        in_specs += [pl.BlockSpec((t, LANES), lambda bi, h, c, lam_r: (bi, nh + h)),
                     pl.BlockSpec((n_lat, V_EXT, KV_CHUNK), lambda bi, h, c, lam_r: (bi, h, 0))]
        args += [qk, vt_lat]
    in_specs.append(pl.BlockSpec((LANES, 1), lambda bi, h, c, lam_r: (0, 0)))
    args.append(sub_g.reshape(LANES, 1).astype(F32))
    stat = pltpu.VMEM((1, tq), F32)
    acc = pltpu.VMEM((V_EXT, tq), F32)

    def buffer_set(tk):
        return ([pltpu.VMEM((tk, tq), F32)] * 2 + [pltpu.VMEM((8, tq), F32)] * 2
                + [pltpu.VMEM((tk, tq), BF16)] * 2 + [stat] * 2)

    grid_spec = pltpu.PrefetchScalarGridSpec(
        num_scalar_prefetch=1,
        grid=(b, nh, nq),
        in_specs=in_specs,
        out_specs=pl.BlockSpec((tq, LANES), lambda bi, h, c, lam_r: (bi * nq + c, h)),
        scratch_shapes=[stat, acc, stat, acc] + buffer_set(CHUNK) + (buffer_set(KV_CHUNK) * 2 if latent else []))
    return pl.pallas_call(
        functools.partial(_attn_kernel, n_lat=n_lat, out_scale=1.0 - lambda_init),
        grid_spec=grid_spec,
        out_shape=jax.ShapeDtypeStruct((b * nq * tq, d), BF16),
        compiler_params=_cparams(("arbitrary", "arbitrary", "arbitrary")),
    )(lam, *args)


def _out_kernel(*refs, mlstm):
    if mlstm:
        (rf_ref, rb_ref, op_ref, og_ref, w_ref, x_ref, g2_ref, mod_ref, rwh_ref, rwl_ref, rb2_ref,
         xo_ref, h_ref, lg_ref) = refs
        r = rf_ref[...] + rb_ref[...]
        dh = r.shape[1] // M_HEADS
        parts = []
        for hh in range(M_HEADS):
            rr = r[:, hh * dh:(hh + 1) * dh]
            parts.append(rr * lax.rsqrt(jnp.mean(rr * rr, axis=-1, keepdims=True) + EPS))
        rn = jnp.concatenate(parts, axis=1) * og_ref[...]
        u = (jax.nn.sigmoid(op_ref[...].astype(F32)) * rn).astype(BF16)
    else:
        (u_ref, w_ref, x_ref, g2_ref, mod_ref, rwh_ref, rwl_ref, rb2_ref, xo_ref, h_ref, lg_ref) = refs
        u = u_ref[...]
    y = jnp.dot(u, w_ref[...], preferred_element_type=F32)
    x = x_ref[...] + mod_ref[2:3, :] * y
    xo_ref[...] = x
    hn = x * lax.rsqrt(jnp.mean(x * x, axis=-1, keepdims=True) + EPS) * g2_ref[...]
    h = hn * (1.0 + mod_ref[4:5, :]) + mod_ref[3:4, :]
    h_ref[...] = h
    h_hi = h.astype(BF16)
    h_lo = (h - h_hi.astype(F32)).astype(BF16)
    lg_ref[...] = (jnp.dot(h_hi, rwh_ref[...], preferred_element_type=F32)
                   + jnp.dot(h_lo, rwh_ref[...], preferred_element_type=F32)
                   + jnp.dot(h_hi, rwl_ref[...], preferred_element_type=F32) + rb2_ref[...])


def _out_proj(u_args, w_out, x, g2, mods, rw_hi, rw_lo, r_bias, *, b, t, mlstm):
    ntot, d = x.shape
    tm = ROW_TILE
    row = lambda i: (i, 0)
    const = lambda i: (0, 0)
    if mlstm:
        rf, rb, p, out_g = u_args
        u_specs = [pl.BlockSpec((tm, d), row), pl.BlockSpec((tm, d), row),
                   pl.BlockSpec((tm, d), lambda i: (i, 3)), pl.BlockSpec((1, d), const)]
        u_in = [rf, rb, p, out_g.reshape(1, d)]
    else:
        u_specs = [pl.BlockSpec((tm, d), row)]
        u_in = list(u_args)
    in_specs = u_specs + [pl.BlockSpec((d, d), const),
                          pl.BlockSpec((tm, d), row),
                          pl.BlockSpec((1, d), const),
                          _mod_spec(d, t // tm, b * t // tm, b),
                          pl.BlockSpec((d, LANES), const),
                          pl.BlockSpec((d, LANES), const),
                          pl.BlockSpec((1, LANES), const)]
    return pl.pallas_call(
        functools.partial(_out_kernel, mlstm=mlstm),
        grid=(ntot // tm,),
        in_specs=in_specs,
        out_specs=[pl.BlockSpec((tm, d), row), pl.BlockSpec((tm, d), row), pl.BlockSpec((tm, LANES), row)],
        out_shape=[jax.ShapeDtypeStruct((ntot, d), F32), jax.ShapeDtypeStruct((ntot, d), F32),
                   jax.ShapeDtypeStruct((ntot, LANES), F32)],
        compiler_params=_cparams(("arbitrary",)),
    )(*u_in, w_out, x, g2.reshape(1, d), mods, rw_hi, rw_lo, r_bias)


def _lane_pick(lane, pairs):
    out = jnp.zeros(lane.shape, F32)
    for idx, val in pairs:
        out = jnp.where(lane == idx, val, out)
    return out


def _route_kernel(lg_ref, rt_ref, cnt_ref):
    @pl.when(pl.program_id(0) == 0)
    def _():
        cnt_ref[...] = jnp.zeros(cnt_ref.shape, F32)

    lg = lg_ref[...]
    lane = lax.broadcasted_iota(I32, lg.shape, 1)
    lane_f = lane.astype(F32)
    big = 1e9
    is_g = lane < N_GROUPS
    gl = jnp.where(is_g, lg, NEG)
    gmax = jnp.max(gl, axis=1, keepdims=True)
    gidx = jnp.min(jnp.where(gl == gmax, lane_f, big), axis=1, keepdims=True)
    gsum = jnp.sum(jnp.where(is_g, jnp.exp(gl - gmax), 0.0), axis=1, keepdims=True)
    gp_top = 1.0 / gsum
    e_lane = lane_f - float(N_GROUPS)
    lo = gidx * float(EXPERTS_PER_GROUP)
    in_grp = jnp.where(e_lane >= lo, jnp.where(e_lane < lo + float(EXPERTS_PER_GROUP), 1.0, 0.0), 0.0) > 0.5
    el = jnp.where(in_grp, lg, NEG)
    v0 = jnp.max(el, axis=1, keepdims=True)
    i0 = jnp.min(jnp.where(el == v0, lane_f, big), axis=1, keepdims=True)
    el2 = jnp.where(lane_f == i0, NEG, el)
    v1 = jnp.max(el2, axis=1, keepdims=True)
    i1 = jnp.min(jnp.where(el2 == v1, lane_f, big), axis=1, keepdims=True)
    e1 = jnp.exp(v1 - v0)
    g0 = gp_top / (1.0 + e1)
    g1 = gp_top * e1 / (1.0 + e1)
    eid0 = i0 - float(N_GROUPS)
    eid1 = i1 - float(N_GROUPS)
    rt_ref[...] = _lane_pick(lane, ((0, eid0), (1, eid1), (2, g0), (3, g1)))
    hot = jnp.where(lane_f == eid0, 1.0, 0.0) + jnp.where(lane_f == eid1, 1.0, 0.0)
    cnt_ref[0:1, :] += jnp.sum(hot, axis=0, keepdims=True)


def _route(logits):
    ntot = logits.shape[0]
    return pl.pallas_call(
        _route_kernel,
        grid=(ntot // CHUNK,),
        in_specs=[pl.BlockSpec((CHUNK, LANES), lambda i: (i, 0))],
        out_specs=[pl.BlockSpec((CHUNK, LANES), lambda i: (i, 0)), pl.BlockSpec((8, LANES), lambda i: (0, 0))],
        out_shape=[jax.ShapeDtypeStruct((ntot, LANES), F32), jax.ShapeDtypeStruct((8, LANES), F32)],
        compiler_params=_cparams(("arbitrary",)),
    )(logits)


def _slot_kernel(rt_ref, cnt_ref, o_ref, run_ref):
    @pl.when(pl.program_id(0) == 0)
    def _():
        cnt = cnt_ref[0:1, :].astype(I32)
        padded = (((cnt + (MOE_BLOCK - 1)) // MOE_BLOCK) * MOE_BLOCK).astype(F32)
        r_i = lax.broadcasted_iota(I32, (LANES, LANES), 0)
        c_i = lax.broadcasted_iota(I32, (LANES, LANES), 1)
        col = jnp.sum(jnp.where(r_i == c_i, jnp.broadcast_to(padded, (LANES, LANES)), 0.0), axis=1, keepdims=True)
        run_ref[...] = jnp.sum(jnp.where(r_i < c_i, col, 0.0), axis=0, keepdims=True)

    rt = rt_ref[...]
    lane = lax.broadcasted_iota(I32, rt.shape, 1)
    lane_f = lane.astype(F32)
    hot0 = lane_f == rt[:, 0:1]
    hot1 = lane_f == rt[:, 1:2]
    both = jnp.where(hot0, 1.0, 0.0) + jnp.where(hot1, 1.0, 0.0)
    t_i = lax.broadcasted_iota(I32, (CHUNK, CHUNK), 0)
    s_i = lax.broadcasted_iota(I32, (CHUNK, CHUNK), 1)
    before = jnp.where(s_i < t_i, 1.0, 0.0).astype(BF16)
    base = run_ref[...] + jnp.dot(before, both.astype(BF16), preferred_element_type=F32)
    d0 = jnp.sum(jnp.where(hot0, base, 0.0), axis=1, keepdims=True)
    d1 = jnp.sum(jnp.where(hot1, base, 0.0), axis=1, keepdims=True)
    o_ref[...] = _lane_pick(lane, ((0, d0), (1, d1))).astype(I32)
    run_ref[...] += jnp.sum(both, axis=0, keepdims=True)


def _slots(route, counts):
    ntot = route.shape[0]
    return pl.pallas_call(
        _slot_kernel,
        grid=(ntot // CHUNK,),
        in_specs=[pl.BlockSpec((CHUNK, LANES), lambda i: (i, 0)), pl.BlockSpec((8, LANES), lambda i: (0, 0))],
        out_specs=pl.BlockSpec((CHUNK, LANES), lambda i: (i, 0)),
        out_shape=jax.ShapeDtypeStruct((ntot, LANES), I32),
        scratch_shapes=[pltpu.VMEM((1, LANES), F32)],
        compiler_params=_cparams(("arbitrary",)),
    )(route, counts)


def _dispatch_kernel(dest_ref, h_ref, xs_in_ref, xs_ref, sem):
    del xs_in_ref

    def row_copy(r, k):
        return pltpu.make_async_copy(h_ref.at[pl.ds(r, 1), :], xs_ref.at[pl.ds(dest_ref[0, 2 * r + k], 1), :], sem)

    def start(r, carry):
        row_copy(r, 0).start(priority=0)
        row_copy(r, 1).start(priority=1)
        return carry

    def wait(r, carry):
        row_copy(r, 0).wait()
        row_copy(r, 1).wait()
        return carry

    lax.fori_loop(0, CHUNK, start, 0, unroll=DMA_UNROLL)
    lax.fori_loop(0, CHUNK, wait, 0, unroll=DMA_UNROLL)


def _dispatch(dest, h, buf):
    ntot, d = h.shape
    cap = buf.shape[0]
    return pl.pallas_call(
        _dispatch_kernel,
        grid=(ntot // CHUNK,),
        in_specs=[pl.BlockSpec((None, 1, 2 * CHUNK), lambda i: (i, 0, 0), memory_space=pltpu.SMEM),
                  pl.BlockSpec((CHUNK, d), lambda i: (i, 0)),
                  pl.BlockSpec(memory_space=pl.ANY)],
        out_specs=pl.BlockSpec(memory_space=pl.ANY),
        out_shape=jax.ShapeDtypeStruct((cap, d), F32),
        scratch_shapes=[pltpu.SemaphoreType.DMA],
        input_output_aliases={2: 0},
        compiler_params=_cparams(("arbitrary",)),
    )(dest, h, buf)


def _expert_kernel(be_ref, nu_ref, xs_ref, wgu_ref, wdn_ref, ys_ref, wgu_bf, wdn_bf):
    j = pl.program_id(0)
    changed = jnp.logical_or(j == 0, be_ref[j] != be_ref[jnp.maximum(j - 1, 0)])

    @pl.when(changed)
    def _():
        wgu_bf[...] = wgu_ref[...].astype(BF16)
        wdn_bf[...] = wdn_ref[...].astype(BF16)

    @pl.when(j < nu_ref[0])
    def _():
        gu = jnp.dot(xs_ref[...].astype(BF16), wgu_bf[...], preferred_element_type=F32)
        g = gu[:, :D_EXPERT]
        a = (g * jax.nn.sigmoid(g) * gu[:, D_EXPERT:]).astype(BF16)
        ys_ref[...] = jnp.dot(a, wdn_bf[...], preferred_element_type=F32)

    @pl.when(j >= nu_ref[0])
    def _():
        ys_ref[...] = jnp.zeros(ys_ref.shape, ys_ref.dtype)


def _experts(blk_e, n_used, xs, w_gu, w_dn, layer):
    cap, d = xs.shape
    row = lambda j, be, nu: (jnp.minimum(j, nu[0] - 1), 0)
    grid_spec = pltpu.PrefetchScalarGridSpec(
        num_scalar_prefetch=2,
        grid=(cap // MOE_BLOCK,),
        in_specs=[pl.BlockSpec((MOE_BLOCK, d), row),
                  pl.BlockSpec((None, None, d, 2 * D_EXPERT), lambda j, be, nu: (layer, be[j], 0, 0)),
                  pl.BlockSpec((None, None, D_EXPERT, d), lambda j, be, nu: (layer, be[j], 0, 0))],
        out_specs=pl.BlockSpec((MOE_BLOCK, d), lambda j, be, nu: (j, 0)),
        scratch_shapes=[pltpu.VMEM((d, 2 * D_EXPERT), BF16), pltpu.VMEM((D_EXPERT, d), BF16)])
    return pl.pallas_call(
        _expert_kernel,
        grid_spec=grid_spec,
        out_shape=jax.ShapeDtypeStruct((cap, d), F32),
        compiler_params=_cparams(("arbitrary",)),
    )(blk_e, n_used, xs, w_gu, w_dn)


def _combine_kernel(dest_ref, rt_ref, x_ref, mod_ref, ys_ref, xo_ref, buf, sem):
    def row_copy(r, k):
        return pltpu.make_async_copy(ys_ref.at[pl.ds(dest_ref[0, 2 * r + k], 1), :],
                                     buf.at[pl.ds(k * CHUNK + r, 1), :], sem)

    def start(r, carry):
        row_copy(r, 0).start(priority=0)
        row_copy(r, 1).start(priority=1)
        return carry

    def wait(r, carry):
        row_copy(r, 0).wait()
        row_copy(r, 1).wait()
        return carry

    lax.fori_loop(0, CHUNK, start, 0, unroll=DMA_UNROLL)
    lax.fori_loop(0, CHUNK, wait, 0, unroll=DMA_UNROLL)
    rt = rt_ref[...]
    y = rt[:, 2:3] * buf[0:CHUNK, :] + rt[:, 3:4] * buf[CHUNK:2 * CHUNK, :]
    xo_ref[...] = x_ref[...] + mod_ref[5:6, :] * y


def _combine(dest, route, x, mods, ys, *, b, t, n_rows):
    d = x.shape[1]
    ncl = t // CHUNK
    return pl.pallas_call(
        _combine_kernel,
        grid=(n_rows // CHUNK,),
        in_specs=[pl.BlockSpec((None, 1, 2 * CHUNK), lambda i: (i, 0, 0), memory_space=pltpu.SMEM),
                  pl.BlockSpec((CHUNK, LANES), lambda i: (i, 0)),
                  pl.BlockSpec((CHUNK, d), lambda i: (i, 0)),
                  _mod_spec(d, ncl, b * ncl, b),
                  pl.BlockSpec(memory_space=pl.ANY)],
        out_specs=pl.BlockSpec((CHUNK, d), lambda i: (i, 0)),
        out_shape=jax.ShapeDtypeStruct((n_rows, d), F32),
        scratch_shapes=[pltpu.VMEM((2 * CHUNK, d), F32), pltpu.SemaphoreType.DMA],
        compiler_params=_cparams(("arbitrary",)),
    )(dest, route, x, mods, ys)


def _moe_capacity(ntot):
    return -(-2 * ntot // MOE_BLOCK) * MOE_BLOCK + N_EXPERTS * MOE_BLOCK


def _moe(x, h, logits, mods, w_gu, w_dn, layer, slot_buf, *, b, t, n_rows):
    ntot = x.shape[0]
    route, counts = _route(logits)
    slot = _slots(route, counts)
    dest = slot[:, :2].reshape(ntot // CHUNK, 1, 2 * CHUNK)
    nblk = slot_buf.shape[0] // MOE_BLOCK
    cnt = counts[0, :N_EXPERTS].astype(I32)
    pad_end = jnp.cumsum((cnt + MOE_BLOCK - 1) // MOE_BLOCK * MOE_BLOCK)
    n_used = pad_end[N_EXPERTS - 1:] // MOE_BLOCK
    blk_start = jnp.minimum(jnp.arange(nblk, dtype=I32), n_used - 1) * MOE_BLOCK
    blk_e = jnp.minimum(jnp.sum((pad_end[None, :] <= blk_start[:, None]).astype(I32), axis=1), N_EXPERTS - 1)
    xs = _dispatch(dest, h, slot_buf)
    ys = _experts(blk_e.astype(I32), n_used.astype(I32), xs, w_gu, w_dn, layer)
    return _combine(dest, route, x, mods, ys, b=b, t=t, n_rows=n_rows), xs


def kernel(x, c, ctx, c_ctx, ada_w, ada_b, norm1_g, norm2_g, m_w_in, m_conv_w, m_conv_b, m_gate_b, m_out_g,
           m_w_out, d_w_in, d_q_g, d_k_g, d_lq1, d_lk1, d_lq2, d_lk2, d_sub_g, d_w_out, r_grp_w, r_grp_b,
           r_exp_w, r_exp_b, e_w_gu, e_w_dn):
    b, t, d = x.shape
    cl = ctx.shape[1]
    depth = ada_w.shape[0]
    assert cl == CHUNK and d == 8 * LANES and b + 1 <= 8
    assert t % Q_TILE == 0 and t % (4 * KV_CHUNK) == 0 and t % ROW_TILE == 0 and (b * cl) % ROW_TILE == 0
    n_lat = b * t
    ntot = n_lat + b * cl

    xs = jnp.concatenate([x.reshape(n_lat, d), ctx.reshape(b * cl, d)], axis=0)
    cond = jnp.zeros((8, d), F32).at[:b].set(c).at[b].set(c_ctx)
    mods_all = _modulation(cond, ada_w, ada_b)
    cos_t, sin_t = _rope_tables(t)
    slot_buf = jnp.zeros((_moe_capacity(ntot), d), F32)

    n_route = N_GROUPS + N_EXPERTS
    for i in range(depth):
        jm = i // 2
        mods = mods_all[i]
        rw = jnp.zeros((d, LANES), F32).at[:, :N_GROUPS].set(r_grp_w[i]).at[:, N_GROUPS:n_route].set(r_exp_w[i])
        rw_hi = rw.astype(BF16)
        rw_lo = (rw - rw_hi.astype(F32)).astype(BF16)
        r_bias = jnp.zeros((1, LANES), F32).at[0, :N_GROUPS].set(r_grp_b[i]).at[0, N_GROUPS:n_route].set(r_exp_b[i])
        if i % 2 == 0:
            w_in = m_w_in[jm]
            wg = jnp.zeros((d, LANES), F32).at[:, :4 * M_HEADS].set(w_in[:, 4 * d:]).astype(BF16)
            gb = jnp.zeros((1, LANES), F32).at[0, :4 * M_HEADS].set(m_gate_b[jm])
            p, gates = _in_proj(xs, norm1_g[i], mods, w_in[:, :4 * d].astype(BF16), b=b, t=t, tn=1024,
                                gates=(wg, gb))
            qk = _mlstm_conv(p, m_conv_w[jm], m_conv_b[jm], b=b, t=t, d=d)
            g4 = gates[:, :4 * M_HEADS].reshape(ntot, 4, M_HEADS)
            rf, rb = _mlstm_scan(qk, p, g4.transpose(2, 0, 1), g4.transpose(2, 1, 0), b=b, t=t, d=d)
            xs, h, logits = _out_proj((rf, rb, p, m_out_g[jm]), m_w_out[jm].astype(BF16), xs, norm2_g[i], mods,
                                      rw_hi, rw_lo, r_bias, b=b, t=t, mlstm=True)
        else:
            lambda_init = 0.8 - 0.6 * math.exp(-0.3 * i)
            p = _in_proj(xs, norm1_g[i], mods, d_w_in[jm].astype(BF16), b=b, t=t, tn=1024)
            qk = _diff_qkprep(p, d_q_g[jm], d_k_g[jm], cos_t, sin_t, b=b, t=t, d=d)
            lam = (jnp.exp(jnp.sum(d_lq1[jm] * d_lk1[jm])) - jnp.exp(jnp.sum(d_lq2[jm] * d_lk2[jm]))
                   + lambda_init).reshape(1).astype(F32)
            v = p[:, 2 * d:]
            vt_lat = _values_t(v[:n_lat].reshape(n_lat // KV_CHUNK, KV_CHUNK, d))
            vt_ctx = _values_t(v[n_lat:].reshape(b, cl, d))
            attn = functools.partial(_diff_attention, qk, vt_lat, vt_ctx, lam, d_sub_g[jm], b=b, t=t, d=d,
                                     lambda_init=lambda_init)
            o = jnp.concatenate([attn(latent=True), attn(latent=False)], axis=0)
            xs, h, logits = _out_proj((o,), d_w_out[jm].astype(BF16), xs, norm2_g[i], mods,
                                      rw_hi, rw_lo, r_bias, b=b, t=t, mlstm=False)
        last = i == depth - 1
        xs, slot_buf = _moe(xs, h, logits, mods, e_w_gu, e_w_dn, i, slot_buf, b=b, t=t,
                            n_rows=n_lat if last else ntot)
    return xs.reshape(b, t, d)
```

```python
import functools
import math

import jax
import jax.numpy as jnp
from jax import lax
from jax.experimental import pallas as pl
from jax.experimental.pallas import tpu as pltpu

F32 = jnp.float32
BF16 = jnp.bfloat16
I32 = jnp.int32
HIGHEST = lax.Precision.HIGHEST

EPS = 1e-6
GRID_W = 64
ROPE_BASE = 10000.0
M_HEADS = 4
M_CONV_W = 5
D_HEADS = 8
D_HEAD_DIM = 64
N_GROUPS = 4
EXPERTS_PER_GROUP = 8
N_EXPERTS = 32
D_EXPERT = 512

LANES = 128
CHUNK = 256
ROW_TILE = 512
Q_TILE = 512
KV_CHUNK = 1024
SOFTMAX_ROWS = 64
V_EXT = LANES + 16
MOE_BLOCK = 256
DMA_UNROLL = 8
NEG = -1e30
VMEM_LIMIT = 56 * 1024 * 1024


def _cparams(sem):
    return pltpu.CompilerParams(dimension_semantics=sem, vmem_limit_bytes=VMEM_LIMIT)


def _nt_dot(a, b):
    return lax.dot_general(a, b, (((1,), (1,)), ((), ())), preferred_element_type=F32)


def _tn_dot(a, b):
    return lax.dot_general(a, b, (((0,), (0,)), ((), ())), preferred_element_type=F32)


def _mod_spec(d, tiles_per_batch, n_lat_tiles, b):
    return pl.BlockSpec((None, 6, d), lambda i, *_: (jnp.where(i < n_lat_tiles, i // tiles_per_batch, b), 0, 0))


def _mod_kernel(s_ref, w_ref, b_ref, o_ref):
    s = s_ref[...]
    s = s * jax.nn.sigmoid(s)
    o_ref[...] = jnp.dot(s, w_ref[...], precision=HIGHEST, preferred_element_type=F32) + b_ref[...]


def _modulation(cond, ada_w, ada_b):
    depth, d, n = ada_w.shape
    tn = n // 4
    out = pl.pallas_call(
        _mod_kernel,
        grid=(depth, n // tn),
        in_specs=[pl.BlockSpec((8, d), lambda l, j: (0, 0)),
                  pl.BlockSpec((None, d, tn), lambda l, j: (l, 0, j)),
                  pl.BlockSpec((None, 1, tn), lambda l, j: (l, 0, j))],
        out_specs=pl.BlockSpec((None, 8, tn), lambda l, j: (l, 0, j)),
        out_shape=jax.ShapeDtypeStruct((depth, 8, n), F32),
        compiler_params=_cparams(("arbitrary", "arbitrary")),
    )(cond, ada_w, ada_b.reshape(depth, 1, n))
    return out.reshape(depth, 8, 6, d)


def _in_kernel(*refs, has_gates, tn):
    if has_gates:
        x_ref, g_ref, mod_ref, w_ref, wg_ref, gb_ref, o_ref, go_ref, h_ref = refs
    else:
        x_ref, g_ref, mod_ref, w_ref, o_ref, h_ref = refs

    x = x_ref[...]
    y = x * lax.rsqrt(jnp.mean(x * x, axis=-1, keepdims=True) + EPS) * g_ref[...]
    hb = (y * (1.0 + mod_ref[1:2, :]) + mod_ref[0:1, :]).astype(BF16)
    h_ref[...] = hb
    if has_gates:
        go_ref[...] = jnp.dot(hb, wg_ref[...], preferred_element_type=F32) + gb_ref[...]
    for c0 in range(0, w_ref.shape[1], tn):
        o_ref[:, c0:c0 + tn] = jnp.dot(h_ref[...], w_ref[:, c0:c0 + tn],
                                       preferred_element_type=F32).astype(o_ref.dtype)


def _in_proj(x, g, mods, w, *, b, t, tn, gates=None):
    ntot, d = x.shape
    n = w.shape[1]
    tm = ROW_TILE
    has_gates = gates is not None
    in_specs = [pl.BlockSpec((tm, d), lambda i: (i, 0)),
                pl.BlockSpec((1, d), lambda i: (0, 0)),
                _mod_spec(d, t // tm, b * t // tm, b),
                pl.BlockSpec((d, n), lambda i: (0, 0))]
    args = [x, g.reshape(1, d), mods, w]
    out_specs = pl.BlockSpec((tm, n), lambda i: (i, 0))
    out_shape = jax.ShapeDtypeStruct((ntot, n), BF16)
    if has_gates:
        wg, gb = gates
        in_specs += [pl.BlockSpec((d, LANES), lambda i: (0, 0)),
                     pl.BlockSpec((1, LANES), lambda i: (0, 0))]
        args += [wg, gb]
        out_specs = [out_specs, pl.BlockSpec((tm, LANES), lambda i: (i, 0))]
        out_shape = [out_shape, jax.ShapeDtypeStruct((ntot, LANES), F32)]
    return pl.pallas_call(
        functools.partial(_in_kernel, has_gates=has_gates, tn=tn),
        grid=(ntot // tm,),
        in_specs=in_specs,
        out_specs=out_specs,
        out_shape=out_shape,
        scratch_shapes=[pltpu.VMEM((tm, d), BF16)],
        compiler_params=_cparams(("arbitrary",)),
    )(*args)


def _conv_kernel(xm_ref, xp_ref, xn_ref, w_ref, b_ref, o_ref, *, ncl, n_lat_chunks, ncol_q, k_scale):
    i = pl.program_id(0)
    j = pl.program_id(1)
    c = i % ncl
    is_lat = i < n_lat_chunks
    left_ok = jnp.logical_and(is_lat, c >= 1)
    right_ok = jnp.logical_and(is_lat, c < ncl - 1)
    xm = xm_ref[...].astype(F32)
    xp = jnp.where(left_ok, xp_ref[...].astype(F32), 0.0)[14:16, :]
    xn = jnp.where(right_ok, xn_ref[...].astype(F32), 0.0)[0:2, :]
    ext = jnp.concatenate([xp, xm, xn], axis=0)
    rows = xm.shape[0]
    w = w_ref[...]
    y = b_ref[...] + w[0:1, :] * ext[0:rows, :]
    for k in range(1, M_CONV_W):
        y = y + w[k:k + 1, :] * ext[k:k + rows, :]
    y = y * jax.nn.sigmoid(y)
    y = y * jnp.where(j >= ncol_q, k_scale, 1.0)
    o_ref[...] = y.astype(o_ref.dtype)


def _mlstm_conv(p, conv_w, conv_b, *, b, t, d):
    ntot = p.shape[0]
    tn = 512
    nrb = ntot // 16
    return pl.pallas_call(
        functools.partial(_conv_kernel, ncl=t // CHUNK, n_lat_chunks=b * t // CHUNK, ncol_q=d // tn,
                          k_scale=(d // M_HEADS) ** -0.5),
        grid=(ntot // CHUNK, 2 * d // tn),
        in_specs=[pl.BlockSpec((CHUNK, tn), lambda i, j: (i, j)),
                  pl.BlockSpec((16, tn), lambda i, j: (jnp.maximum(i * (CHUNK // 16) - 1, 0), j)),
                  pl.BlockSpec((16, tn), lambda i, j: (jnp.minimum((i + 1) * (CHUNK // 16), nrb - 1), j)),
                  pl.BlockSpec((M_CONV_W, tn), lambda i, j: (0, j)),
                  pl.BlockSpec((1, tn), lambda i, j: (0, j))],
        out_specs=pl.BlockSpec((CHUNK, tn), lambda i, j: (i, j)),
        out_shape=jax.ShapeDtypeStruct((ntot, 2 * d), BF16),
        compiler_params=_cparams(("arbitrary", "arbitrary")),
    )(p, p, p, conv_w, conv_b.reshape(1, 2 * d))


def _log_sigmoid(x):
    return jnp.minimum(x, 0.0) - jnp.log(1.0 + jnp.exp(-jnp.abs(x)))


def _mlstm_chunk(q, k, v, gcol, grow, igate, fgate, mask, mask_t, ct_ref, n_ref, m_ref, o_ref):
    i_row = grow[igate:igate + 1, :]
    i_col = gcol[:, igate:igate + 1]
    f_row = _log_sigmoid(grow[fgate:fgate + 1, :])
    f_col = _log_sigmoid(gcol[:, fgate:fgate + 1])
    cum_col = jnp.sum(jnp.where(mask, f_row, 0.0), axis=1, keepdims=True)
    cum_row = jnp.sum(jnp.where(mask_t, f_col, 0.0), axis=0, keepdims=True)
    tot = jnp.sum(f_col, axis=0, keepdims=True)
    m_prev = m_ref[...]
    dmat = jnp.where(mask, cum_col - cum_row + i_row, NEG)
    m_inter = cum_col + m_prev
    m_out = jnp.maximum(m_inter, jnp.max(dmat, axis=1, keepdims=True))
    w_intra = jnp.exp(dmat - m_out)
    w_inter = jnp.exp(m_inter - m_out)
    sc = _nt_dot(q, k) * w_intra
    ct = ct_ref[...]
    n_row = n_ref[...]
    qf = q.astype(F32)
    kf = k.astype(F32)
    vf = v.astype(F32)
    num = (w_inter * jnp.dot(q, ct.astype(BF16), preferred_element_type=F32)
           + jnp.dot(sc.astype(BF16), v, preferred_element_type=F32))
    den = w_inter * jnp.sum(qf * n_row, axis=1, keepdims=True) + jnp.sum(sc, axis=1, keepdims=True)
    o_ref[...] = (num / jnp.maximum(jnp.abs(den), jnp.exp(-m_out))).astype(o_ref.dtype)
    g_col = tot - cum_col + i_col
    m_next = jnp.maximum(tot + m_prev, jnp.max(g_col, axis=0, keepdims=True))
    decay = jnp.exp(tot + m_prev - m_next)
    wg = jnp.exp(g_col - m_next)
    ct_ref[...] = decay * ct + _tn_dot(k, (wg * vf).astype(BF16))
    n_ref[...] = decay * n_row + jnp.sum(wg * kf, axis=0, keepdims=True)
    m_ref[...] = m_next


def _scan_kernel(qf_ref, kf_ref, vf_ref, gcf_ref, grf_ref, qb_ref, kb_ref, vb_ref, gcb_ref, grb_ref,
                 of_ref, ob_ref, ctf_ref, nf_ref, mf_ref, ctb_ref, nb_ref, mb_ref):
    @pl.when(pl.program_id(2) == 0)
    def _():
        for r in (ctf_ref, nf_ref, mf_ref, ctb_ref, nb_ref, mb_ref):
            r[...] = jnp.zeros(r.shape, r.dtype)

    t_i = lax.broadcasted_iota(I32, (CHUNK, CHUNK), 0)
    s_i = lax.broadcasted_iota(I32, (CHUNK, CHUNK), 1)
    lower = s_i <= t_i
    upper = s_i >= t_i
    _mlstm_chunk(qf_ref[...], kf_ref[...], vf_ref[...], gcf_ref[...], grf_ref[...], 0, 2,
                 lower, upper, ctf_ref, nf_ref, mf_ref, of_ref)
    _mlstm_chunk(qb_ref[...], kb_ref[...], vb_ref[...], gcb_ref[...], grb_ref[...], 1, 3,
                 upper, lower, ctb_ref, nb_ref, mb_ref, ob_ref)


def _mlstm_scan(qk, p, gcol, grow, *, b, t, d):
    ntot = qk.shape[0]
    dh = d // M_HEADS
    ncl = t // CHUNK
    ctx0 = b * ncl

    def fwd(bi, s):
        return jnp.where(s == 0, ctx0 + bi, bi * ncl + s - 1)

    def bwd(bi, s):
        return jnp.where(s == 0, ctx0 + bi, bi * ncl + ncl - s)

    def specs(order):
        return [pl.BlockSpec((CHUNK, dh), lambda bi, h, s: (order(bi, s), h)),
                pl.BlockSpec((CHUNK, dh), lambda bi, h, s: (order(bi, s), M_HEADS + h)),
                pl.BlockSpec((CHUNK, dh), lambda bi, h, s: (order(bi, s), 2 * M_HEADS + h)),
                pl.BlockSpec((None, CHUNK, 4), lambda bi, h, s: (h, order(bi, s), 0)),
                pl.BlockSpec((None, 4, CHUNK), lambda bi, h, s: (h, 0, order(bi, s)))]

    def ospec(order):
        return pl.BlockSpec((CHUNK, dh), lambda bi, h, s: (order(bi, s), h))

    state = [pltpu.VMEM((dh, dh), F32), pltpu.VMEM((1, dh), F32), pltpu.VMEM((1, 1), F32)]
    return pl.pallas_call(
        _scan_kernel,
        grid=(b, M_HEADS, ncl + 1),
        in_specs=specs(fwd) + specs(bwd),
        out_specs=[ospec(fwd), ospec(bwd)],
        out_shape=[jax.ShapeDtypeStruct((ntot, d), BF16)] * 2,
        scratch_shapes=state + state,
        compiler_params=_cparams(("arbitrary", "arbitrary", "arbitrary")),
    )(qk, qk, p, gcol, grow, qk, qk, p, gcol, grow)


def _qkprep_kernel(p_ref, gq_ref, gk_ref, cos_ref, sin_ref, o_ref, *, n_lat_chunks, nblk_q, q_scale):
    is_ctx = pl.program_id(0) >= n_lat_chunks
    cos = jnp.where(is_ctx, 1.0, cos_ref[...])
    sin = jnp.where(is_ctx, 0.0, sin_ref[...])
    lane = lax.broadcasted_iota(I32, (CHUNK, LANES), 1)
    first_half = (lane % 32) < 16
    r_i = lax.broadcasted_iota(I32, (LANES, LANES), 0)
    c_i = lax.broadcasted_iota(I32, (LANES, LANES), 1)
    group = jnp.where((r_i // D_HEAD_DIM) == (c_i // D_HEAD_DIM), 1.0, 0.0).astype(BF16)
    for blk in range(2 * nblk_q):
        x = p_ref[:, blk * LANES:(blk + 1) * LANES].astype(F32)
        ss = x * x
        ss_hi = ss.astype(BF16)
        ss_lo = (ss - ss_hi.astype(F32)).astype(BF16)
        ssum = (jnp.dot(ss_hi, group, preferred_element_type=F32)
                + jnp.dot(ss_lo, group, preferred_element_type=F32))
        g = gq_ref[...] if blk < nblk_q else gk_ref[...]
        xn = x * lax.rsqrt(ssum * (1.0 / D_HEAD_DIM) + EPS) * g
        partner = jnp.where(first_half, pltpu.roll(xn, LANES - 16, 1), pltpu.roll(xn, 16, 1))
        y = xn * cos + partner * sin
        if blk < nblk_q:
            y = y * q_scale
        o_ref[:, blk * LANES:(blk + 1) * LANES] = y.astype(o_ref.dtype)


def _diff_qkprep(p, q_g, k_g, cos_t, sin_t, *, b, t, d):
    ntot = p.shape[0]
    ncl = t // CHUNK
    g2 = lambda g: jnp.tile(g.astype(F32), 2).reshape(1, LANES)
    return pl.pallas_call(
        functools.partial(_qkprep_kernel, n_lat_chunks=b * ncl, nblk_q=d // LANES,
                          q_scale=D_HEAD_DIM ** -0.5 * math.log2(math.e)),
        grid=(ntot // CHUNK,),
        in_specs=[pl.BlockSpec((CHUNK, 2 * d), lambda i: (i, 0)),
                  pl.BlockSpec((1, LANES), lambda i: (0, 0)),
                  pl.BlockSpec((1, LANES), lambda i: (0, 0)),
                  pl.BlockSpec((CHUNK, LANES), lambda i: (i % ncl, 0)),
                  pl.BlockSpec((CHUNK, LANES), lambda i: (i % ncl, 0))],
        out_specs=pl.BlockSpec((CHUNK, 2 * d), lambda i: (i, 0)),
        out_shape=jax.ShapeDtypeStruct((ntot, 2 * d), BF16),
        compiler_params=_cparams(("arbitrary",)),
    )(p, g2(q_g), g2(k_g), cos_t, sin_t)


def _rope_tables(t):
    rows = t // GRID_W
    row = jnp.repeat(jnp.arange(rows), GRID_W).astype(F32)[:, None]
    col = jnp.tile(jnp.arange(GRID_W), rows).astype(F32)[:, None]
    nf = D_HEAD_DIM // 4
    inv = ROPE_BASE ** (-jnp.arange(nf, dtype=F32) / nf)
    ar, ac = row * inv, col * inv
    cos64 = jnp.concatenate([jnp.cos(ar), jnp.cos(ar), jnp.cos(ac), jnp.cos(ac)], axis=-1)
    sin64 = jnp.concatenate([-jnp.sin(ar), jnp.sin(ar), -jnp.sin(ac), jnp.sin(ac)], axis=-1)
    return jnp.tile(cos64, (1, 2)), jnp.tile(sin64, (1, 2))


def _attn_kernel(*refs, n_lat, out_scale):
    if n_lat:
        lam_ref, q_ref, kc_ref, vtc_ref, kl_ref, vtl_ref, sg_ref, o_ref, m1, a1, m2, a2 = refs[:12]
        bufs = refs[12:]
    else:
        lam_ref, q_ref, kc_ref, vtc_ref, sg_ref, o_ref, m1, a1, m2, a2 = refs[:10]
        bufs = refs[10:]
    sets = [tuple(bufs[i + 2 * k:i + 2 * k + 2] for k in range(4)) for i in range(0, len(bufs), 8)]
    q = q_ref[...]
    lane = lax.broadcasted_iota(I32, q.shape, 1)
    zero = jnp.zeros_like(q)
    qs = (jnp.where(lane < D_HEAD_DIM, q, zero), jnp.where(lane >= D_HEAD_DIM, q, zero))
    maxes = (m1, m2)
    accs = (a1, a2)
    for m_r, a_r in zip(maxes, accs):
        m_r[...] = jnp.full(m_r.shape, NEG, F32)
        a_r[...] = jnp.zeros(a_r.shape, F32)

    def scores(keys, s_bufs, mx_bufs):
        for qm, s_ref, mx_ref in zip(qs, s_bufs, mx_bufs):
            st = _nt_dot(keys, qm)
            s_ref[...] = st
            mx = st[0:8]
            for r in range(8, st.shape[0], 8):
                mx = jnp.maximum(mx, st[r:r + 8])
            mx_ref[...] = mx

    def softmax(s_bufs, mx_bufs, p_bufs, al_bufs):
        for s_ref, mx_ref, p_ref, al_ref, m_r in zip(s_bufs, mx_bufs, p_bufs, al_bufs, maxes):
            m_old = m_r[...]
            m_new = jnp.maximum(m_old, jnp.max(mx_ref[...], axis=0, keepdims=True))
            m_r[...] = m_new
            al_ref[...] = jnp.exp2(m_old - m_new)
            for r0 in range(0, s_ref.shape[0], SOFTMAX_ROWS):
                p = jnp.exp2(s_ref[r0:r0 + SOFTMAX_ROWS, :] - m_new)
                p_ref[r0:r0 + SOFTMAX_ROWS, :] = p.astype(BF16)

    def weigh(vt, p_bufs, al_bufs):
        for p_ref, al_ref, a_r in zip(p_bufs, al_bufs, accs):
            a_r[...] = al_ref[...] * a_r[...] + jnp.dot(vt, p_ref[...], preferred_element_type=F32)

    s_c, mx_c, p_c, al_c = sets[0]
    scores(kc_ref[...], s_c, mx_c)
    softmax(s_c, mx_c, p_c, al_c)
    weigh(vtc_ref[...], p_c, al_c)

    if n_lat:
        def keys(j):
            return kl_ref[pl.ds(pl.multiple_of(j * KV_CHUNK, KV_CHUNK), KV_CHUNK), :]

        (s_a, mx_a, p_a, al_a), (s_b, mx_b, p_b, al_b) = sets[1:3]
        scores(keys(0), s_a, mx_a)
        softmax(s_a, mx_a, p_a, al_a)
        scores(keys(1), s_b, mx_b)

        def body(jj, carry):
            j = 2 * jj
            softmax(s_b, mx_b, p_b, al_b)
            scores(keys(j + 2), s_a, mx_a)
            weigh(vtl_ref[j], p_a, al_a)
            softmax(s_a, mx_a, p_a, al_a)
            scores(keys(j + 3), s_b, mx_b)
            weigh(vtl_ref[j + 1], p_b, al_b)
            return carry

        lax.fori_loop(0, n_lat // 2 - 1, body, 0)
        softmax(s_b, mx_b, p_b, al_b)
        weigh(vtl_ref[n_lat - 2], p_a, al_a)
        weigh(vtl_ref[n_lat - 1], p_b, al_b)
    o = (a1[0:LANES, :] / a1[LANES:LANES + 1, :]
         - lam_ref[0] * (a2[0:LANES, :] / a2[LANES:LANES + 1, :]))
    o = o * lax.rsqrt(jnp.mean(o * o, axis=0, keepdims=True) + EPS) * sg_ref[...] * out_scale
    o_ref[...] = o.T.astype(o_ref.dtype)


def _values_t(v):
    n, tk, d = v.shape
    nh = d // LANES
    vt = v.transpose(0, 2, 1).reshape(n, nh, LANES, tk)
    ones = jnp.ones((n, nh, 1, tk), v.dtype)
    pad = jnp.zeros((n, nh, V_EXT - LANES - 1, tk), v.dtype)
    return jnp.concatenate([vt, ones, pad], axis=2).reshape(n, nh * V_EXT, tk)


def _diff_attention(qk, vt_lat, vt_ctx, lam, sub_g, *, b, t, d, lambda_init, latent):
    nh = d // LANES
    ctx0 = b * t // CHUNK
    n_lat = t // KV_CHUNK if latent else 0
    tq = Q_TILE if latent else CHUNK
    nq = t // tq if latent else 1
    q_row = (lambda bi, c: bi * nq + c) if latent else (lambda bi, c: ctx0 + bi)
    in_specs = [pl.BlockSpec((tq, LANES), lambda bi, h, c, lam_r: (q_row(bi, c), h)),
                pl.BlockSpec((CHUNK, LANES), lambda bi, h, c, lam_r: (ctx0 + bi, nh + h)),
                pl.BlockSpec((None, V_EXT, CHUNK), lambda bi, h, c, lam_r: (bi, h, 0))]
    args = [qk, qk, vt_ctx]
    if latent:
        in_specs += [pl.BlockSpec((t, LANES), lambda bi, h, c, lam_r: (bi, nh + h)),
                     pl.BlockSpec((n_lat, V_EXT, KV_CHUNK), lambda bi, h, c, lam_r: (bi, h, 0))]
        args += [qk, vt_lat]
    in_specs.append(pl.BlockSpec((LANES, 1), lambda bi, h, c, lam_r: (0, 0)))
    args.append(sub_g.reshape(LANES, 1).astype(F32))
    stat = pltpu.VMEM((1, tq), F32)
    acc = pltpu.VMEM((V_EXT, tq), F32)

    def buffer_set(tk):
        return ([pltpu.VMEM((tk, tq), F32)] * 2 + [pltpu.VMEM((8, tq), F32)] * 2
                + [pltpu.VMEM((tk, tq), BF16)] * 2 + [stat] * 2)

    grid_spec = pltpu.PrefetchScalarGridSpec(
        num_scalar_prefetch=1,
        grid=(b, nh, nq),
        in_specs=in_specs,
        out_specs=pl.BlockSpec((tq, LANES), lambda bi, h, c, lam_r: (bi * nq + c, h)),
        scratch_shapes=[stat, acc, stat, acc] + buffer_set(CHUNK) + (buffer_set(KV_CHUNK) * 2 if latent else []))
    return pl.pallas_call(
        functools.partial(_attn_kernel, n_lat=n_lat, out_scale=1.0 - lambda_init),
        grid_spec=grid_spec,
        out_shape=jax.ShapeDtypeStruct((b * nq * tq, d), BF16),
        compiler_params=_cparams(("arbitrary", "arbitrary", "arbitrary")),
    )(lam, *args)


def _out_kernel(*refs, mlstm):
    if mlstm:
        (rf_ref, rb_ref, op_ref, og_ref, w_ref, x_ref, g2_ref, mod_ref, rwh_ref, rwl_ref, rb2_ref,
         xo_ref, h_ref, lg_ref) = refs
        r = rf_ref[...].astype(F32) + rb_ref[...].astype(F32)
        dh = r.shape[1] // M_HEADS
        parts = []
        for hh in range(M_HEADS):
            rr = r[:, hh * dh:(hh + 1) * dh]
            parts.append(rr * lax.rsqrt(jnp.mean(rr * rr, axis=-1, keepdims=True) + EPS))
        rn = jnp.concatenate(parts, axis=1) * og_ref[...]
        u = (jax.nn.sigmoid(op_ref[...].astype(F32)) * rn).astype(BF16)
    else:
        (u_ref, w_ref, x_ref, g2_ref, mod_ref, rwh_ref, rwl_ref, rb2_ref, xo_ref, h_ref, lg_ref) = refs
        u = u_ref[...]
    y = jnp.dot(u, w_ref[...], preferred_element_type=F32)
    x = x_ref[...] + mod_ref[2:3, :] * y
    xo_ref[...] = x
    hn = x * lax.rsqrt(jnp.mean(x * x, axis=-1, keepdims=True) + EPS) * g2_ref[...]
    h = hn * (1.0 + mod_ref[4:5, :]) + mod_ref[3:4, :]
    h_ref[...] = h
    h_hi = h.astype(BF16)
    h_lo = (h - h_hi.astype(F32)).astype(BF16)
    lg_ref[...] = (jnp.dot(h_hi, rwh_ref[...], preferred_element_type=F32)
                   + jnp.dot(h_lo, rwh_ref[...], preferred_element_type=F32)
                   + jnp.dot(h_hi, rwl_ref[...], preferred_element_type=F32) + rb2_ref[...])


def _out_proj(u_args, w_out, x, g2, mods, rw_hi, rw_lo, r_bias, *, b, t, mlstm):
    ntot, d = x.shape
    tm = ROW_TILE
    row = lambda i: (i, 0)
    const = lambda i: (0, 0)
    if mlstm:
        rf, rb, p, out_g = u_args
        u_specs = [pl.BlockSpec((tm, d), row),
                   pl.BlockSpec((tm, d), row),
                   pl.BlockSpec((tm, d), lambda i: (i, 3)),
                   pl.BlockSpec((1, d), const)]
        u_in = [rf, rb, p, out_g.reshape(1, d)]
    else:
        u_specs = [pl.BlockSpec((tm, d), row)]
        u_in = list(u_args)
    in_specs = u_specs + [pl.BlockSpec((d, d), const),
                          pl.BlockSpec((tm, d), row),
                          pl.BlockSpec((1, d), const),
                          _mod_spec(d, t // tm, b * t // tm, b),
                          pl.BlockSpec((d, LANES), const),
                          pl.BlockSpec((d, LANES), const),
                          pl.BlockSpec((1, LANES), const)]
    return pl.pallas_call(
        functools.partial(_out_kernel, mlstm=mlstm),
        grid=(ntot // tm,),
        in_specs=in_specs,
        out_specs=[pl.BlockSpec((tm, d), row), pl.BlockSpec((tm, d), row), pl.BlockSpec((tm, LANES), row)],
        out_shape=[jax.ShapeDtypeStruct((ntot, d), F32), jax.ShapeDtypeStruct((ntot, d), F32),
                   jax.ShapeDtypeStruct((ntot, LANES), F32)],
        compiler_params=_cparams(("arbitrary",)),
    )(*u_in, w_out, x, g2.reshape(1, d), mods, rw_hi, rw_lo, r_bias)


def _lane_pick(lane, pairs):
    out = jnp.zeros(lane.shape, F32)
    for idx, val in pairs:
        out = jnp.where(lane == idx, val, out)
    return out


def _route_kernel(lg_ref, rt_ref, cnt_ref):
    @pl.when(pl.program_id(0) == 0)
    def _():
        cnt_ref[...] = jnp.zeros(cnt_ref.shape, F32)

    lg = lg_ref[...]
    lane = lax.broadcasted_iota(I32, lg.shape, 1)
    lane_f = lane.astype(F32)
    big = 1e9
    is_g = lane < N_GROUPS
    gl = jnp.where(is_g, lg, NEG)
    gmax = jnp.max(gl, axis=1, keepdims=True)
    gidx = jnp.min(jnp.where(gl == gmax, lane_f, big), axis=1, keepdims=True)
    gsum = jnp.sum(jnp.where(is_g, jnp.exp(gl - gmax), 0.0), axis=1, keepdims=True)
    gp_top = 1.0 / gsum
    e_lane = lane_f - float(N_GROUPS)
    lo = gidx * float(EXPERTS_PER_GROUP)
    in_grp = jnp.where(e_lane >= lo, jnp.where(e_lane < lo + float(EXPERTS_PER_GROUP), 1.0, 0.0), 0.0) > 0.5
    el = jnp.where(in_grp, lg, NEG)
    v0 = jnp.max(el, axis=1, keepdims=True)
    i0 = jnp.min(jnp.where(el == v0, lane_f, big), axis=1, keepdims=True)
    el2 = jnp.where(lane_f == i0, NEG, el)
    v1 = jnp.max(el2, axis=1, keepdims=True)
    i1 = jnp.min(jnp.where(el2 == v1, lane_f, big), axis=1, keepdims=True)
    e1 = jnp.exp(v1 - v0)
    g0 = gp_top / (1.0 + e1)
    g1 = gp_top * e1 / (1.0 + e1)
    eid0 = i0 - float(N_GROUPS)
    eid1 = i1 - float(N_GROUPS)
    rt_ref[...] = _lane_pick(lane, ((0, eid0), (1, eid1), (2, g0), (3, g1)))
    hot = jnp.where(lane_f == eid0, 1.0, 0.0) + jnp.where(lane_f == eid1, 1.0, 0.0)
    cnt_ref[0:1, :] += jnp.sum(hot, axis=0, keepdims=True)


def _route(logits):
    ntot = logits.shape[0]
    return pl.pallas_call(
        _route_kernel,
        grid=(ntot // CHUNK,),
        in_specs=[pl.BlockSpec((CHUNK, LANES), lambda i: (i, 0))],
        out_specs=[pl.BlockSpec((CHUNK, LANES), lambda i: (i, 0)), pl.BlockSpec((8, LANES), lambda i: (0, 0))],
        out_shape=[jax.ShapeDtypeStruct((ntot, LANES), F32), jax.ShapeDtypeStruct((8, LANES), F32)],
        compiler_params=_cparams(("arbitrary",)),
    )(logits)


def _slot_kernel(rt_ref, cnt_ref, o_ref, run_ref):
    @pl.when(pl.program_id(0) == 0)
    def _():
        cnt = cnt_ref[0:1, :].astype(I32)
        padded = (((cnt + (MOE_BLOCK - 1)) // MOE_BLOCK) * MOE_BLOCK).astype(F32)
        r_i = lax.broadcasted_iota(I32, (LANES, LANES), 0)
        c_i = lax.broadcasted_iota(I32, (LANES, LANES), 1)
        col = jnp.sum(jnp.where(r_i == c_i, jnp.broadcast_to(padded, (LANES, LANES)), 0.0), axis=1, keepdims=True)
        run_ref[...] = jnp.sum(jnp.where(r_i < c_i, col, 0.0), axis=0, keepdims=True)

    rt = rt_ref[...]
    lane = lax.broadcasted_iota(I32, rt.shape, 1)
    lane_f = lane.astype(F32)
    hot0 = lane_f == rt[:, 0:1]
    hot1 = lane_f == rt[:, 1:2]
    both = jnp.where(hot0, 1.0, 0.0) + jnp.where(hot1, 1.0, 0.0)
    t_i = lax.broadcasted_iota(I32, (CHUNK, CHUNK), 0)
    s_i = lax.broadcasted_iota(I32, (CHUNK, CHUNK), 1)
    before = jnp.where(s_i < t_i, 1.0, 0.0).astype(BF16)
    base = run_ref[...] + jnp.dot(before, both.astype(BF16), preferred_element_type=F32)
    d0 = jnp.sum(jnp.where(hot0, base, 0.0), axis=1, keepdims=True)
    d1 = jnp.sum(jnp.where(hot1, base, 0.0), axis=1, keepdims=True)
    o_ref[...] = _lane_pick(lane, ((0, d0), (1, d1))).astype(I32)
    run_ref[...] += jnp.sum(both, axis=0, keepdims=True)


def _slots(route, counts):
    ntot = route.shape[0]
    return pl.pallas_call(
        _slot_kernel,
        grid=(ntot // CHUNK,),
        in_specs=[pl.BlockSpec((CHUNK, LANES), lambda i: (i, 0)), pl.BlockSpec((8, LANES), lambda i: (0, 0))],
        out_specs=pl.BlockSpec((CHUNK, LANES), lambda i: (i, 0)),
        out_shape=jax.ShapeDtypeStruct((ntot, LANES), I32),
        scratch_shapes=[pltpu.VMEM((1, LANES), F32)],
        compiler_params=_cparams(("arbitrary",)),
    )(route, counts)


def _dispatch_kernel(dest_ref, h_ref, xs_in_ref, xs_ref, sem):
    del xs_in_ref

    def row_copy(r, k):
        return pltpu.make_async_copy(h_ref.at[pl.ds(r, 1), :], xs_ref.at[pl.ds(dest_ref[0, 2 * r + k], 1), :], sem)

    def start(r, carry):
        row_copy(r, 0).start(priority=0)
        row_copy(r, 1).start(priority=1)
        return carry

    def wait(r, carry):
        row_copy(r, 0).wait()
        row_copy(r, 1).wait()
        return carry

    lax.fori_loop(0, CHUNK, start, 0, unroll=DMA_UNROLL)
    lax.fori_loop(0, CHUNK, wait, 0, unroll=DMA_UNROLL)


def _dispatch(dest, h, buf):
    ntot, d = h.shape
    cap = buf.shape[0]
    return pl.pallas_call(
        _dispatch_kernel,
        grid=(ntot // CHUNK,),
        in_specs=[pl.BlockSpec((None, 1, 2 * CHUNK), lambda i: (i, 0, 0), memory_space=pltpu.SMEM),
                  pl.BlockSpec((CHUNK, d), lambda i: (i, 0)),
                  pl.BlockSpec(memory_space=pl.ANY)],
        out_specs=pl.BlockSpec(memory_space=pl.ANY),
        out_shape=jax.ShapeDtypeStruct((cap, d), F32),
        scratch_shapes=[pltpu.SemaphoreType.DMA],
        input_output_aliases={2: 0},
        compiler_params=_cparams(("arbitrary",)),
    )(dest, h, buf)


def _expert_kernel(be_ref, nu_ref, xs_ref, wgu_ref, wdn_ref, ys_ref, wgu_bf, wdn_bf):
    j = pl.program_id(0)
    changed = jnp.logical_or(j == 0, be_ref[j] != be_ref[jnp.maximum(j - 1, 0)])

    @pl.when(changed)
    def _():
        wgu_bf[...] = wgu_ref[...].astype(BF16)
        wdn_bf[...] = wdn_ref[...].astype(BF16)

    @pl.when(j < nu_ref[0])
    def _():
        gu = jnp.dot(xs_ref[...].astype(BF16), wgu_bf[...], preferred_element_type=F32)
        g = gu[:, :D_EXPERT]
        a = (g * jax.nn.sigmoid(g) * gu[:, D_EXPERT:]).astype(BF16)
        ys_ref[...] = jnp.dot(a, wdn_bf[...], preferred_element_type=F32)

    @pl.when(j >= nu_ref[0])
    def _():
        ys_ref[...] = jnp.zeros(ys_ref.shape, ys_ref.dtype)


def _experts(blk_e, n_used, xs, w_gu, w_dn, layer):
    cap, d = xs.shape
    row = lambda j, be, nu: (jnp.minimum(j, nu[0] - 1), 0)
    grid_spec = pltpu.PrefetchScalarGridSpec(
        num_scalar_prefetch=2,
        grid=(cap // MOE_BLOCK,),
        in_specs=[pl.BlockSpec((MOE_BLOCK, d), row),
                  pl.BlockSpec((None, None, d, 2 * D_EXPERT), lambda j, be, nu: (layer, be[j], 0, 0)),
                  pl.BlockSpec((None, None, D_EXPERT, d), lambda j, be, nu: (layer, be[j], 0, 0))],
        out_specs=pl.BlockSpec((MOE_BLOCK, d), lambda j, be, nu: (j, 0)),
        scratch_shapes=[pltpu.VMEM((d, 2 * D_EXPERT), BF16), pltpu.VMEM((D_EXPERT, d), BF16)])
    return pl.pallas_call(
        _expert_kernel,
        grid_spec=grid_spec,
        out_shape=jax.ShapeDtypeStruct((cap, d), F32),
        compiler_params=_cparams(("arbitrary",)),
    )(blk_e, n_used, xs, w_gu, w_dn)


def _combine_kernel(dest_ref, rt_ref, x_ref, mod_ref, ys_ref, xo_ref, buf, sem):
    def row_copy(r, k):
        return pltpu.make_async_copy(ys_ref.at[pl.ds(dest_ref[0, 2 * r + k], 1), :],
                                     buf.at[pl.ds(k * CHUNK + r, 1), :], sem)

    def start(r, carry):
        row_copy(r, 0).start(priority=0)
        row_copy(r, 1).start(priority=1)
        return carry

    def wait(r, carry):
        row_copy(r, 0).wait()
        row_copy(r, 1).wait()
        return carry

    lax.fori_loop(0, CHUNK, start, 0, unroll=DMA_UNROLL)
    lax.fori_loop(0, CHUNK, wait, 0, unroll=DMA_UNROLL)
    rt = rt_ref[...]
    y = rt[:, 2:3] * buf[0:CHUNK, :] + rt[:, 3:4] * buf[CHUNK:2 * CHUNK, :]
    xo_ref[...] = x_ref[...] + mod_ref[5:6, :] * y


def _combine(dest, route, x, mods, ys, *, b, t, n_rows):
    d = x.shape[1]
    ncl = t // CHUNK
    return pl.pallas_call(
        _combine_kernel,
        grid=(n_rows // CHUNK,),
        in_specs=[pl.BlockSpec((None, 1, 2 * CHUNK), lambda i: (i, 0, 0), memory_space=pltpu.SMEM),
                  pl.BlockSpec((CHUNK, LANES), lambda i: (i, 0)),
                  pl.BlockSpec((CHUNK, d), lambda i: (i, 0)),
                  _mod_spec(d, ncl, b * ncl, b),
                  pl.BlockSpec(memory_space=pl.ANY)],
        out_specs=pl.BlockSpec((CHUNK, d), lambda i: (i, 0)),
        out_shape=jax.ShapeDtypeStruct((n_rows, d), F32),
        scratch_shapes=[pltpu.VMEM((2 * CHUNK, d), F32), pltpu.SemaphoreType.DMA],
        compiler_params=_cparams(("arbitrary",)),
    )(dest, route, x, mods, ys)


def _moe_capacity(ntot):
    return -(-2 * ntot // MOE_BLOCK) * MOE_BLOCK + N_EXPERTS * MOE_BLOCK


def _moe(x, h, logits, mods, w_gu, w_dn, layer, slot_buf, *, b, t, n_rows):
    ntot = x.shape[0]
    route, counts = _route(logits)
    slot = _slots(route, counts)
    dest = slot[:, :2].reshape(ntot // CHUNK, 1, 2 * CHUNK)
    nblk = slot_buf.shape[0] // MOE_BLOCK
    cnt = counts[0, :N_EXPERTS].astype(I32)
    pad_end = jnp.cumsum((cnt + MOE_BLOCK - 1) // MOE_BLOCK * MOE_BLOCK)
    n_used = pad_end[N_EXPERTS - 1:] // MOE_BLOCK
    blk_start = jnp.minimum(jnp.arange(nblk, dtype=I32), n_used - 1) * MOE_BLOCK
    blk_e = jnp.minimum(jnp.sum((pad_end[None, :] <= blk_start[:, None]).astype(I32), axis=1), N_EXPERTS - 1)
    xs = _dispatch(dest, h, slot_buf)
    ys = _experts(blk_e.astype(I32), n_used.astype(I32), xs, w_gu, w_dn, layer)
    return _combine(dest, route, x, mods, ys, b=b, t=t, n_rows=n_rows), xs


def kernel(x, c, ctx, c_ctx, ada_w, ada_b, norm1_g, norm2_g, m_w_in, m_conv_w, m_conv_b, m_gate_b, m_out_g,
           m_w_out, d_w_in, d_q_g, d_k_g, d_lq1, d_lk1, d_lq2, d_lk2, d_sub_g, d_w_out, r_grp_w, r_grp_b,
           r_exp_w, r_exp_b, e_w_gu, e_w_dn):
    b, t, d = x.shape
    cl = ctx.shape[1]
    depth = ada_w.shape[0]
    assert cl == CHUNK and d == 8 * LANES and b + 1 <= 8
    assert t % Q_TILE == 0 and t % (4 * KV_CHUNK) == 0 and t % ROW_TILE == 0 and (b * cl) % ROW_TILE == 0
    n_lat = b * t
    ntot = n_lat + b * cl

    xs = jnp.concatenate([x.reshape(n_lat, d), ctx.reshape(b * cl, d)], axis=0)
    cond = jnp.zeros((8, d), F32).at[:b].set(c).at[b].set(c_ctx)
    mods_all = _modulation(cond, ada_w, ada_b)
    cos_t, sin_t = _rope_tables(t)
    slot_buf = jnp.zeros((_moe_capacity(ntot), d), F32)

    n_route = N_GROUPS + N_EXPERTS
    for i in range(depth):
        jm = i // 2
        mods = mods_all[i]
        rw = jnp.zeros((d, LANES), F32).at[:, :N_GROUPS].set(r_grp_w[i]).at[:, N_GROUPS:n_route].set(r_exp_w[i])
        rw_hi = rw.astype(BF16)
        rw_lo = (rw - rw_hi.astype(F32)).astype(BF16)
        r_bias = jnp.zeros((1, LANES), F32).at[0, :N_GROUPS].set(r_grp_b[i]).at[0, N_GROUPS:n_route].set(r_exp_b[i])
        if i % 2 == 0:
            w_in = m_w_in[jm]
            wg = jnp.zeros((d, LANES), F32).at[:, :4 * M_HEADS].set(w_in[:, 4 * d:]).astype(BF16)
            gb = jnp.zeros((1, LANES), F32).at[0, :4 * M_HEADS].set(m_gate_b[jm])
            p, gates = _in_proj(xs, norm1_g[i], mods, w_in[:, :4 * d].astype(BF16), b=b, t=t, tn=1024,
                                gates=(wg, gb))
            qk = _mlstm_conv(p, m_conv_w[jm], m_conv_b[jm], b=b, t=t, d=d)
            g4 = gates[:, :4 * M_HEADS].reshape(ntot, 4, M_HEADS)
            rf, rb = _mlstm_scan(qk, p, g4.transpose(2, 0, 1), g4.transpose(2, 1, 0), b=b, t=t, d=d)
            xs, h, logits = _out_proj((rf, rb, p, m_out_g[jm]), m_w_out[jm].astype(BF16), xs, norm2_g[i], mods,
                                      rw_hi, rw_lo, r_bias, b=b, t=t, mlstm=True)
        else:
            lambda_init = 0.8 - 0.6 * math.exp(-0.3 * i)
            p = _in_proj(xs, norm1_g[i], mods, d_w_in[jm].astype(BF16), b=b, t=t, tn=1024)
            qk = _diff_qkprep(p, d_q_g[jm], d_k_g[jm], cos_t, sin_t, b=b, t=t, d=d)
            lam = (jnp.exp(jnp.sum(d_lq1[jm] * d_lk1[jm])) - jnp.exp(jnp.sum(d_lq2[jm] * d_lk2[jm]))
                   + lambda_init).reshape(1).astype(F32)
            v = p[:, 2 * d:]
            vt_lat = _values_t(v[:n_lat].reshape(n_lat // KV_CHUNK, KV_CHUNK, d))
            vt_ctx = _values_t(v[n_lat:].reshape(b, cl, d))
            attn = functools.partial(_diff_attention, qk, vt_lat, vt_ctx, lam, d_sub_g[jm], b=b, t=t, d=d,
                                     lambda_init=lambda_init)
            o = jnp.concatenate([attn(latent=True), attn(latent=False)], axis=0)
            xs, h, logits = _out_proj((o,), d_w_out[jm].astype(BF16), xs, norm2_g[i], mods,
                                      rw_hi, rw_lo, r_bias, b=b, t=t, mlstm=False)
        last = i == depth - 1
        xs, slot_buf = _moe(xs, h, logits, mods, e_w_gu, e_w_dn, i, slot_buf, b=b, t=t,
                            n_rows=n_lat if last else ntot)
    return xs.reshape(b, t, d)
```
